```python
import math
import jax, jax.numpy as jnp
from jax import lax
import numpy as np

D_MODEL = 1024
BATCH = 32
SEQ = 256
DEPTH = 2
DEC_BATCH = 4
DEC_SEQ = 4096
PAST_LEN = 512

GRID_W = 64
N_MIXERS = 4
GROUP_WIDTH = D_MODEL // N_MIXERS
HEAD_DIM = 64
N_HEADS = GROUP_WIDTH // HEAD_DIM
WIN_KV_HEADS = 2
WIN_GROUPS = N_HEADS // WIN_KV_HEADS
WINDOW = 128
WIN_BLOCK = 128
Q_BLOCK = 128
DIFF_DIM = HEAD_DIM // 2
REC_CHUNK = 16
FFN_HIDDEN = ((8 * D_MODEL // 3 + 255) // 256) * 256
ROPE_BASE = 10000.0
EPS = 1e-6
MASK_VALUE = -1e30
SPLIT_SIZES = ((GROUP_WIDTH,) * 4
               + (GROUP_WIDTH, WIN_KV_HEADS * HEAD_DIM, WIN_KV_HEADS * HEAD_DIM)
               + (GROUP_WIDTH,) * 3
               + (GROUP_WIDTH,) * 5)
IN_WIDTH = sum(SPLIT_SIZES)
SPLIT_POINTS = tuple(int(s) for s in np.cumsum(SPLIT_SIZES)[:-1])

kernel_name = 'hybrid_prefix_diffusion_step'

F32 = jnp.float32


def rmsnorm(x, g):
    xf = x.astype(F32)
    y = xf * lax.rsqrt(jnp.mean(xf * xf, axis=-1, keepdims=True) + EPS)
    return (y * g.astype(F32)).astype(x.dtype)


def rope_axis(x, pos):
    half = x.shape[-1] // 2
    inv = ROPE_BASE ** (-jnp.arange(half, dtype=F32) / half)
    ang = pos.astype(F32)[:, None] * inv[None, :]
    cos, sin = jnp.cos(ang), jnp.sin(ang)
    xf = x.astype(F32)
    x1, x2 = xf[..., :half], xf[..., half:]
    return jnp.concatenate([x1 * cos - x2 * sin, x2 * cos + x1 * sin], axis=-1).astype(x.dtype)


def rope_2d(x, row, col):
    d = x.shape[-1] // 2
    return jnp.concatenate([rope_axis(x[..., :d], row), rope_axis(x[..., d:], col)], axis=-1)


def grid_positions(n_tokens):
    n_rows = n_tokens // GRID_W
    row = jnp.repeat(jnp.arange(n_rows), GRID_W)
    col = jnp.tile(jnp.arange(GRID_W), n_rows)
    return row, col


def split_heads(x, n):
    b, l, _ = x.shape
    return x.reshape(b, l, n, -1).transpose(0, 2, 1, 3)


def merge_heads(x):
    b, n, l, d = x.shape
    return x.transpose(0, 2, 1, 3).reshape(b, l, n * d)


def chunk_recurrence(q, k, v, log_f, s0):
    b, h, l, dk = q.shape
    dv = v.shape[-1]
    n = l // REC_CHUNK
    shp = lambda a: a.astype(F32).reshape(b, h, n, REC_CHUNK, a.shape[-1])
    qc, kc, vc, lf = shp(q), shp(k), shp(v), shp(log_f)
    cum = jnp.cumsum(lf, axis=3)
    last = cum[:, :, :, -1:, :]
    causal = jnp.tril(jnp.ones((REC_CHUNK, REC_CHUNK), bool))[:, :, None]
    rel = cum[:, :, :, :, None, :] - cum[:, :, :, None, :, :]
    decay = jnp.where(causal, jnp.exp(jnp.where(causal, rel, 0.0)), 0.0)
    attn = jnp.einsum('bhntd,bhnsd,bhntsd->bhnts', qc, kc, decay)
    o_intra = jnp.einsum('bhnts,bhnsv->bhntv', attn, vc)
    u = jnp.einsum('bhnsd,bhnsv->bhndv', kc * jnp.exp(last - cum), vc)
    dec = jnp.exp(last[:, :, :, 0])

    def step(s, inp):
        d_n, u_n = inp
        return d_n[..., None] * s + u_n, s

    s_fin, s_in = lax.scan(step, s0.astype(F32), (jnp.moveaxis(dec, 2, 0), jnp.moveaxis(u, 2, 0)))
    s_in = jnp.moveaxis(s_in, 0, 2)
    o_inter = jnp.einsum('bhntd,bhndv->bhntv', qc * jnp.exp(cum), s_in)
    o = (o_intra + o_inter).reshape(b, h, l, dv).astype(v.dtype)
    return o, s_fin.astype(v.dtype)


def bidir_recurrence(q, k_f, k_b, v, lf_f, lf_b, s0_f, s0_b):
    flip = lambda a: jnp.flip(a, axis=2)
    o_f, s_f = chunk_recurrence(q, k_f, v, lf_f, s0_f)
    o_b, s_b = chunk_recurrence(flip(q), flip(k_b), flip(v), flip(lf_b), s0_b)
    return o_f + flip(o_b), jnp.stack([s_f, s_b], axis=1)


def block_attention(q, k, v, sink=None):
    b, hk, g, lq, d = q.shape
    nb = lq // Q_BLOCK
    qb = jnp.moveaxis(q.reshape(b, hk, g, nb, Q_BLOCK, d), 3, 0)
    kf = k.astype(F32)
    scale = d ** -0.5

    def attend(qi):
        s = jnp.einsum('bhgqd,bhkd->bhgqk', qi.astype(F32), kf) * scale
        if sink is not None:
            s_sink = jnp.broadcast_to(sink.astype(F32)[None, :, :, None, None], s.shape[:-1] + (1,))
            p = jax.nn.softmax(jnp.concatenate([s_sink, s], axis=-1), axis=-1)[..., 1:]
        else:
            p = jax.nn.softmax(s, axis=-1)
        return jnp.einsum('bhgqk,bhkv->bhgqv', p.astype(v.dtype), v)

    o = lax.map(attend, qb)
    return jnp.moveaxis(o, 0, 3).reshape(b, hk, g, lq, v.shape[-1])


def window_attention(q, k, v, k_ctx, v_ctx, sink):
    b, hk, g, l, d = q.shape
    nb = l // WIN_BLOCK
    pad = ((0, 0), (0, 0), (WIN_BLOCK, WIN_BLOCK), (0, 0))

    def bands(a):
        ap = jnp.pad(a, pad).reshape(b, hk, nb + 2, WIN_BLOCK, a.shape[-1])
        return jnp.concatenate([ap[:, :, :-2], ap[:, :, 1:-1], ap[:, :, 2:]], axis=3)

    kb, vb = bands(k), bands(v)
    qb = q.reshape(b, hk, g, nb, WIN_BLOCK, d).astype(F32)
    scale = d ** -0.5
    s_loc = jnp.einsum('bhgnqd,bhnkd->bhgnqk', qb, kb.astype(F32)) * scale
    q_pos = jnp.arange(l).reshape(nb, WIN_BLOCK)
    k_pos = (jnp.arange(nb)[:, None] - 1) * WIN_BLOCK + jnp.arange(3 * WIN_BLOCK)[None, :]
    valid = ((k_pos[:, None, :] >= 0) & (k_pos[:, None, :] < l)
             & (jnp.abs(q_pos[:, :, None] - k_pos[:, None, :]) <= WINDOW))
    s_loc = jnp.where(valid, s_loc, MASK_VALUE)
    s_ctx = jnp.einsum('bhgnqd,bhkd->bhgnqk', qb, k_ctx.astype(F32)) * scale
    s_sink = jnp.broadcast_to(sink.astype(F32)[None, :, :, None, None, None], s_loc.shape[:-1] + (1,))
    probs = jax.nn.softmax(jnp.concatenate([s_sink, s_loc, s_ctx], axis=-1), axis=-1)
    n_loc = 3 * WIN_BLOCK
    p_loc = jnp.where(valid, probs[..., 1:1 + n_loc], 0.0).astype(v.dtype)
    p_ctx = probs[..., 1 + n_loc:].astype(v.dtype)
    o = (jnp.einsum('bhgnqk,bhnkv->bhgnqv', p_loc, vb)
         + jnp.einsum('bhgnqk,bhkv->bhgnqv', p_ctx, v_ctx.astype(v.dtype)))
    return o.reshape(b, hk, g, l, -1)


def hgrn_gate(z, lb):
    lbh = lb.astype(F32).reshape(N_HEADS, 1, HEAD_DIM)
    zf = z.astype(F32)
    f = lbh + (1.0 - lbh) * jax.nn.sigmoid(zf)
    log_f = jnp.log(jnp.maximum(f, 1e-30))
    k = (1.0 - lbh) * jax.nn.sigmoid(-zf)
    return log_f, k


def token_mixing(h, p, lam_init, lb, ctx, pos):
    b, l, _ = h.shape
    latent = ctx is not None
    (rq, rk, rv, rg, wq, wk, wv, dq, dk, dv, hq, hzf, hzb, hi, hg) = jnp.split(
        h @ p['w_in'], SPLIT_POINTS, axis=-1)

    rq = split_heads(rq, N_HEADS)
    rk = split_heads(rk, N_HEADS) * (HEAD_DIM ** -0.5)
    rv = split_heads(rv, N_HEADS)
    if latent:
        rq, rk = rope_2d(rq, *pos), rope_2d(rk, *pos)
    log_gamma = -jnp.exp(p['ret_decay'].astype(F32))
    lg_f = jnp.broadcast_to(log_gamma[0][None, :, None, None], rq.shape)
    lg_b = jnp.broadcast_to(log_gamma[1][None, :, None, None], rq.shape)
    s0 = ctx['ret'] if latent else jnp.zeros((b, 2, N_HEADS, HEAD_DIM, HEAD_DIM), F32)
    o_ret, st_ret = bidir_recurrence(rq, rk, rk, rv, lg_f, lg_b, s0[:, 0], s0[:, 1])
    o_ret = merge_heads(rmsnorm(o_ret, p['ret_norm'].reshape(N_HEADS, 1, HEAD_DIM))) * jax.nn.silu(rg)

    wq = rmsnorm(split_heads(wq, N_HEADS), p['win_qn'])
    wk = rmsnorm(split_heads(wk, WIN_KV_HEADS), p['win_kn'])
    wv = split_heads(wv, WIN_KV_HEADS)
    sink = p['win_sink'].reshape(WIN_KV_HEADS, WIN_GROUPS)
    if latent:
        wq, wk = rope_2d(wq, *pos), rope_2d(wk, *pos)
        o_win = window_attention(wq.reshape(b, WIN_KV_HEADS, WIN_GROUPS, l, HEAD_DIM), wk, wv,
                                 ctx['win_k'], ctx['win_v'], sink)
    else:
        o_win = block_attention(wq.reshape(b, WIN_KV_HEADS, WIN_GROUPS, l, HEAD_DIM), wk, wv, sink)
    o_win = merge_heads(o_win.reshape(b, N_HEADS, l, HEAD_DIM))

    dq = rmsnorm(dq.reshape(b, l, N_HEADS, 2, DIFF_DIM).transpose(0, 2, 3, 1, 4), p['diff_qn'])
    dk = rmsnorm(dk.reshape(b, l, N_HEADS, 2, DIFF_DIM).transpose(0, 2, 3, 1, 4), p['diff_kn'])
    dv = split_heads(dv, N_HEADS)
    if latent:
        dq, dk = rope_2d(dq, *pos), rope_2d(dk, *pos)
        keys = jnp.concatenate([dk, ctx['diff_k'].astype(dk.dtype)], axis=3)
        vals = jnp.concatenate([dv, ctx['diff_v'].astype(dv.dtype)], axis=2)
    else:
        keys, vals = dk, dv
    o1 = block_attention(dq[:, :, 0:1], keys[:, :, 0], vals)
    o2 = block_attention(dq[:, :, 1:2], keys[:, :, 1], vals)
    lq1, lk1, lq2, lk2 = p['diff_lambda'].astype(F32)
    lam = jnp.exp(jnp.sum(lq1 * lk1)) - jnp.exp(jnp.sum(lq2 * lk2)) + lam_init
    o_diff = (o1[:, :, 0].astype(F32) - lam * o2[:, :, 0].astype(F32)).astype(h.dtype)
    o_diff = merge_heads(rmsnorm(o_diff, p['diff_norm'].reshape(N_HEADS, 1, HEAD_DIM))) * (1.0 - lam_init)

    hq = split_heads(hq, N_HEADS)
    hi = split_heads(hi, N_HEADS)
    lf_f, k_f = hgrn_gate(split_heads(hzf, N_HEADS), lb[0])
    lf_b, k_b = hgrn_gate(split_heads(hzb, N_HEADS), lb[1])
    s0 = ctx['hgrn'] if latent else jnp.zeros((b, 2, N_HEADS, HEAD_DIM, HEAD_DIM), F32)
    o_h, st_h = bidir_recurrence(hq, k_f, k_b, hi, lf_f, lf_b, s0[:, 0], s0[:, 1])
    o_h = merge_heads(rmsnorm(o_h, p['hgrn_norm'].reshape(N_HEADS, 1, HEAD_DIM))) * jax.nn.silu(hg)

    out = jnp.concatenate([o_ret, o_win, o_diff, o_h], axis=-1) @ p['w_out']
    new = None if latent else dict(ret=st_ret, win_k=wk, win_v=wv, diff_k=dk, diff_v=dv, hgrn=st_h)
    return out, new


def trunk_layer(x, cond, p, lam_init, lb, ctx, pos):
    mod = (jax.nn.silu(cond) @ p['w_ada'] + p['b_ada']).reshape(cond.shape[0], 6, 1, D_MODEL)
    sh1, sc1, g1, sh2, sc2, g2 = [mod[:, i] for i in range(6)]
    h = rmsnorm(x, p['norm1']) * (1.0 + sc1) + sh1
    mix, new = token_mixing(h, p, lam_init, lb, ctx, pos)
    x = x + g1 * mix
    h = rmsnorm(x, p['norm2']) * (1.0 + sc2) + sh2
    gate, up = jnp.split(h @ p['w_ffn_in'], 2, axis=-1)
    x = x + g2 * ((jax.nn.silu(gate) * up) @ p['w_ffn_out'])
    return x, new


def setup_inputs(seed: int = 0) -> dict:
    key = jax.random.key(seed)
    ks = iter(jax.random.split(key, 40))
    nrm = lambda shape, s: s * jax.random.normal(next(ks), shape, F32)
    gain = lambda shape: 1.0 + nrm(shape, 0.05)
    ret_base = jnp.asarray(np.log(-np.log(1.0 - 2.0 ** (-5.0 - np.arange(N_HEADS)))), F32)
    return {
        'x_prompt': nrm((BATCH, SEQ, D_MODEL), 1.0),
        'x_sample': nrm((DEC_BATCH, DEC_SEQ, D_MODEL), 1.0),
        'state_ret': nrm((DEC_BATCH, DEPTH, 2, N_HEADS, HEAD_DIM, HEAD_DIM), 0.3),
        'cache_win_k': nrm((DEC_BATCH, DEPTH, WIN_KV_HEADS, PAST_LEN, HEAD_DIM), 1.0),
        'cache_win_v': nrm((DEC_BATCH, DEPTH, WIN_KV_HEADS, PAST_LEN, HEAD_DIM), 1.0),
        'cache_diff_k': nrm((DEC_BATCH, DEPTH, N_HEADS, 2, PAST_LEN, DIFF_DIM), 1.0),
        'cache_diff_v': nrm((DEC_BATCH, DEPTH, N_HEADS, PAST_LEN, HEAD_DIM), 1.0),
        'state_hgrn': nrm((DEC_BATCH, DEPTH, 2, N_HEADS, HEAD_DIM, HEAD_DIM), 0.3),
        'c': nrm((DEC_BATCH, D_MODEL), 1.0),
        'c_ctx': nrm((D_MODEL,), 1.0),
        'norm1_g': gain((DEPTH, D_MODEL)),
        'norm2_g': gain((DEPTH, D_MODEL)),
        'w_ada': nrm((DEPTH, D_MODEL, 6 * D_MODEL), 0.5 * D_MODEL ** -0.5),
        'b_ada': nrm((DEPTH, 6 * D_MODEL), 0.01),
        'w_in': nrm((DEPTH, D_MODEL, IN_WIDTH), D_MODEL ** -0.5),
        'ret_decay': ret_base[None, None, :] + nrm((DEPTH, 2, N_HEADS), 0.05),
        'ret_norm_g': gain((DEPTH, GROUP_WIDTH)),
        'win_q_norm': gain((DEPTH, HEAD_DIM)),
        'win_k_norm': gain((DEPTH, HEAD_DIM)),
        'win_sink': nrm((DEPTH, N_HEADS), 0.5),
        'diff_q_norm': gain((DEPTH, DIFF_DIM)),
        'diff_k_norm': gain((DEPTH, DIFF_DIM)),
        'diff_lambda': nrm((DEPTH, 4, DIFF_DIM), 0.1),
        'diff_norm_g': gain((DEPTH, GROUP_WIDTH)),
        'hgrn_lb_logits': nrm((DEPTH, 2, GROUP_WIDTH), 0.5),
        'hgrn_norm_g': gain((DEPTH, GROUP_WIDTH)),
        'w_out': nrm((DEPTH, D_MODEL, D_MODEL), D_MODEL ** -0.5),
        'w_ffn_in': nrm((DEPTH, D_MODEL, 2 * FFN_HIDDEN), D_MODEL ** -0.5),
        'w_ffn_out': nrm((DEPTH, FFN_HIDDEN, D_MODEL), FFN_HIDDEN ** -0.5),
    }


def reference(x_prompt, x_sample, state_ret, cache_win_k, cache_win_v, cache_diff_k, cache_diff_v,
              state_hgrn, c, c_ctx, norm1_g, norm2_g, w_ada, b_ada, w_in, ret_decay, ret_norm_g,
              win_q_norm, win_k_norm, win_sink, diff_q_norm, diff_k_norm, diff_lambda, diff_norm_g,
              hgrn_lb_logits, hgrn_norm_g, w_out, w_ffn_in, w_ffn_out):
    lb_p = jax.nn.softmax(hgrn_lb_logits.astype(F32), axis=0)
    lb_all = jnp.cumsum(lb_p, axis=0) - lb_p
    pos = grid_positions(x_sample.shape[1])
    y_p, y_s = x_prompt, x_sample
    n_ret, n_wk, n_wv, n_dk, n_dv, n_hg = [], [], [], [], [], []
    for l in range(DEPTH):
        p = dict(norm1=norm1_g[l], norm2=norm2_g[l], w_ada=w_ada[l], b_ada=b_ada[l], w_in=w_in[l],
                 ret_decay=ret_decay[l], ret_norm=ret_norm_g[l], win_qn=win_q_norm[l],
                 win_kn=win_k_norm[l], win_sink=win_sink[l], diff_qn=diff_q_norm[l],
                 diff_kn=diff_k_norm[l], diff_lambda=diff_lambda[l], diff_norm=diff_norm_g[l],
                 hgrn_norm=hgrn_norm_g[l], w_out=w_out[l], w_ffn_in=w_ffn_in[l], w_ffn_out=w_ffn_out[l])
        lam_init = 0.8 - 0.6 * math.exp(-0.3 * l)
        y_p, st = trunk_layer(y_p, c_ctx[None, :], p, lam_init, lb_all[l], None, None)
        n_ret.append(st['ret']); n_wk.append(st['win_k']); n_wv.append(st['win_v'])
        n_dk.append(st['diff_k']); n_dv.append(st['diff_v']); n_hg.append(st['hgrn'])
        ctx = dict(ret=state_ret[:, l], win_k=cache_win_k[:, l], win_v=cache_win_v[:, l],
                   diff_k=cache_diff_k[:, l], diff_v=cache_diff_v[:, l], hgrn=state_hgrn[:, l])
        y_s, _ = trunk_layer(y_s, c, p, lam_init, lb_all[l], ctx, pos)
    return (y_p, y_s, jnp.stack(n_ret, axis=1), jnp.stack(n_wk, axis=1), jnp.stack(n_wv, axis=1),
            jnp.stack(n_dk, axis=1), jnp.stack(n_dv, axis=1), jnp.stack(n_hg, axis=1))
```

```python
import functools
import math

import numpy as np
import jax
import jax.numpy as jnp
from jax import lax
from jax.experimental import pallas as pl
from jax.experimental.pallas import tpu as pltpu

F32 = jnp.float32
BF16 = jnp.bfloat16

GROUP = 256
HEAD_DIM = 64
N_HEADS = 4
KV_HEADS = 2
DIFF_DIM = 32
WINDOW = 128
WIN_BLOCK = 128
GRID_W = 64
ROPE_BASE = 10000.0
EPS = 1e-6
MASK_VALUE = -1e30
REC_BLOCK = 16
SEG = 256
LANES = 128
V7X_VMEM_LIMIT_BYTES = 56 * 1024 * 1024

(CB_RQ, CB_RK, CB_RV, CB_RG, CB_WQ, CB_WKV, CB_DQ, CB_DK, CB_DV,
 CB_HQ, CB_HZF, CB_HZB, CB_HI, CB_HG) = range(14)

_NT = (((1,), (1,)), ((), ()))
_TN = (((0,), (0,)), ((), ()))


def _sds(shape, dtype):
    return jax.ShapeDtypeStruct(shape, dtype)


def _cparams(*sem):
    return pltpu.CompilerParams(dimension_semantics=sem, vmem_limit_bytes=V7X_VMEM_LIMIT_BYTES)


def _const_spec(shape):
    return pl.BlockSpec(shape, lambda *_: (0,) * len(shape))


def _dot(a, b):
    return jnp.dot(a, b, preferred_element_type=F32)


def _dot_nt(a, b):
    return lax.dot_general(a, b, _NT, preferred_element_type=F32)


def _dot_tn(a, b):
    return lax.dot_general(a, b, _TN, preferred_element_type=F32)


def _silu(x):
    return x * jax.nn.sigmoid(x)


def _group_matrix(width, gsize):
    i = np.arange(width)
    return jnp.asarray((i[:, None] // gsize) == (i[None, :] // gsize), BF16)


def _block_matrices(n, blk):
    r, c = np.arange(n)[:, None], np.arange(n)[None, :]
    same = (r // blk) == (c // blk)
    return (jnp.asarray(same & (c <= r), BF16), jnp.asarray(same & (c >= r), BF16), jnp.asarray(same, BF16))


def _kv_dup_matrices():
    dk = np.zeros((GROUP, GROUP), np.float32)
    dv = np.zeros((GROUP, GROUP), np.float32)
    for h in range(N_HEADS):
        kv = h // (N_HEADS // KV_HEADS)
        for j in range(HEAD_DIM):
            dk[kv * HEAD_DIM + j, h * HEAD_DIM + j] = 1.0
            dv[LANES + kv * HEAD_DIM + j, h * HEAD_DIM + j] = 1.0
    return jnp.asarray(dk, BF16), jnp.asarray(dv, BF16)


def _rope_tables(n_tokens, head_dim):
    d = head_dim // 2
    half = d // 2
    inv = ROPE_BASE ** (-jnp.arange(half, dtype=F32) / half)
    t = jnp.arange(n_tokens)
    row = (t // GRID_W).astype(F32)
    col = (t % GRID_W).astype(F32)
    j = np.arange(LANES) % head_dim
    w = j % d
    use_row = jnp.asarray((j // d) == 0)
    pos = jnp.where(use_row[None, :], row[:, None], col[:, None])
    ang = pos * inv[w % half][None, :]
    sign = jnp.asarray(np.where(w >= half, 1.0, -1.0), F32)
    return jnp.cos(ang), jnp.sin(ang) * sign[None, :]


def _group_rms(x, g_mat, gsize, gain):
    ss = _dot((x * x).astype(BF16), g_mat)
    return x * lax.rsqrt(ss * (1.0 / gsize) + EPS) * gain


def _rope(y, cos, sin, half):
    lane = lax.broadcasted_iota(jnp.int32, (1, LANES), 1)
    second = (lane % (2 * half)) >= half
    outs = []
    for p in range(y.shape[1] // LANES):
        z = y[:, p * LANES:(p + 1) * LANES]
        partner = jnp.where(second, pltpu.roll(z, half, 1), pltpu.roll(z, LANES - half, 1))
        outs.append(z * cos + partner * sin)
    return outs[0] if len(outs) == 1 else jnp.concatenate(outs, axis=1)


def _head_of_lane(width=GROUP):
    return lax.broadcasted_iota(jnp.int32, (1, width), 1) // HEAD_DIM


def _stack_heads(q, n_heads=N_HEADS):
    head = _head_of_lane()
    zero = jnp.zeros_like(q)
    return jnp.concatenate([jnp.where(head == h, q, zero) for h in range(n_heads)], axis=0)


def _unstack_heads(o4, rows):
    head = _head_of_lane()
    out = jnp.where(head == 0, o4[:rows], 0.0)
    for h in range(1, N_HEADS):
        out = out + jnp.where(head == h, o4[h * rows:(h + 1) * rows], 0.0)
    return out


def _split3(x):
    hi = x.astype(BF16)
    r = x - hi.astype(F32)
    mid = r.astype(BF16)
    lo = (r - mid.astype(F32)).astype(BF16)
    return hi, mid, lo


def _sum_rows(m01, x):
    hi, mid, lo = _split3(x)
    return _dot(m01, hi) + _dot(m01, mid) + _dot(m01, lo)


def _adaln_body(c_ref, w_ref, b_ref, o_ref):
    c = c_ref[...]
    o_ref[...] = jnp.dot(_silu(c), w_ref[...], preferred_element_type=F32,
                         precision=lax.Precision.HIGHEST) + b_ref[...]


def _adaln(cond, w_ada, b_ada):
    depth, d, n = w_ada.shape
    rows = cond.shape[0]
    tn = n // 4
    return pl.pallas_call(
        _adaln_body,
        grid=(depth, n // tn),
        in_specs=[pl.BlockSpec((rows, d), lambda l, j: (0, 0)),
                  pl.BlockSpec((None, d, tn), lambda l, j: (l, 0, j)),
                  pl.BlockSpec((None, 1, tn), lambda l, j: (l, 0, j))],
        out_specs=pl.BlockSpec((None, rows, tn), lambda l, j: (l, 0, j)),
        out_shape=_sds((depth, rows, n), F32),
        compiler_params=_cparams("parallel", "parallel"),
        name="adaln",
    )(cond, w_ada, b_ada.reshape(depth, 1, n))


def _inproj_body(x_ref, g_ref, sc_ref, sh_ref, w_ref, o_ref):
    x = x_ref[...]
    h = x * lax.rsqrt(jnp.mean(x * x, axis=-1, keepdims=True) + EPS) * g_ref[...]
    h = h * (1.0 + sc_ref[...]) + sh_ref[...]
    o_ref[...] = _dot(h.astype(BF16), w_ref[...])


def _mod_row_map(tm, seq_s, n_s):
    return lambda i: (jnp.minimum((i * tm) // seq_s, n_s), 0, 0)


def _inproj(x, g, sc, sh, w, seq_s, n_s, tm=512):
    t, d = x.shape
    n = w.shape[1]
    mod_map = _mod_row_map(tm, seq_s, n_s)
    return pl.pallas_call(
        _inproj_body,
        grid=(t // tm,),
        in_specs=[pl.BlockSpec((tm, d), lambda i: (i, 0)),
                  _const_spec((1, d)),
                  pl.BlockSpec((None, 1, d), mod_map),
                  pl.BlockSpec((None, 1, d), mod_map),
                  _const_spec((d, n))],
        out_specs=pl.BlockSpec((tm, n), lambda i: (i, 0)),
        out_shape=_sds((t, n), F32),
        compiler_params=_cparams("parallel"),
        name="inproj",
    )(x, g, sc, sh, w)


def _prep_body(rope, emit_f32, *refs):
    (wq_ref, wkv_ref, dq_ref, dk_ref, dv_ref, qn_ref, kn_ref, dqn_ref, dkn_ref, g64_ref, g32_ref) = refs[:11]
    refs = refs[11:]
    if rope:
        c64_ref, s64_ref, c32_ref, s32_ref = refs[:4]
        refs = refs[4:]
    wq_o, wkv_o, dq_o, dk_o, dv_o = refs[:5]
    g64 = g64_ref[...]
    g32 = g32_ref[...]

    wq = _group_rms(wq_ref[...], g64, HEAD_DIM, qn_ref[...])
    wkv = wkv_ref[...]
    wk = _group_rms(wkv[:, :LANES], g64[:LANES, :LANES], HEAD_DIM, kn_ref[...])
    dq = _group_rms(dq_ref[...], g32, DIFF_DIM, dqn_ref[...])
    dk = _group_rms(dk_ref[...], g32, DIFF_DIM, dkn_ref[...])
    if rope:
        c64, s64, c32, s32 = c64_ref[...], s64_ref[...], c32_ref[...], s32_ref[...]
        wq = _rope(wq, c64, s64, HEAD_DIM // 4)
        wk = _rope(wk, c64, s64, HEAD_DIM // 4)
        dq = _rope(dq, c32, s32, DIFF_DIM // 4)
        dk = _rope(dk, c32, s32, DIFF_DIM // 4)
    wq_o[...] = (wq * HEAD_DIM ** -0.5).astype(BF16)
    wkv_o[...] = jnp.concatenate([wk, wkv[:, LANES:]], axis=1).astype(BF16)
    dq_o[...] = (dq * DIFF_DIM ** -0.5).astype(BF16)
    dk_o[...] = dk.astype(BF16)
    dv_o[...] = dv_ref[...].astype(BF16)
    if emit_f32:
        wk_f, dk_f = refs[5:7]
        wk_f[...] = wk
        dk_f[...] = dk


def _prep(proj, row0, rows, gains, consts, tables, emit_f32, tm=256):
    rope = tables is not None
    nb0 = row0 // tm

    def col(cb):
        return pl.BlockSpec((tm, GROUP), lambda i, cb=cb: (nb0 + i, cb))

    in_specs = [col(CB_WQ), col(CB_WKV), col(CB_DQ), col(CB_DK), col(CB_DV),
                _const_spec((1, GROUP)), _const_spec((1, LANES)), _const_spec((1, GROUP)), _const_spec((1, GROUP)),
                _const_spec((GROUP, GROUP)), _const_spec((GROUP, GROUP))]
    args = [proj] * 5 + list(gains) + [consts["g64"], consts["g32"]]
    if rope:
        seq = tables[0].shape[0]
        nper = seq // tm
        in_specs += [pl.BlockSpec((tm, LANES), lambda i: (i % nper, 0))] * 4
        args += list(tables)
    out = pl.BlockSpec((tm, GROUP), lambda i: (i, 0))
    out_specs = [out] * 5
    out_shape = [_sds((rows, GROUP), BF16)] * 5
    if emit_f32:
        out_specs += [pl.BlockSpec((tm, LANES), lambda i: (i, 0)), out]
        out_shape += [_sds((rows, LANES), F32), _sds((rows, GROUP), F32)]
    return pl.pallas_call(
        functools.partial(_prep_body, rope, emit_f32),
        grid=(rows // tm,),
        in_specs=in_specs, out_specs=out_specs, out_shape=out_shape,
        compiler_params=_cparams("parallel"),
        name="prep_rope" if rope else "prep",
    )(*args)


def _softmax_pv(parts, sink, den_extra=None):
    m = parts[0][0].max(axis=-1, keepdims=True)
    for s, _ in parts[1:]:
        m = jnp.maximum(m, s.max(axis=-1, keepdims=True))
    if sink is not None:
        m = jnp.maximum(m, sink)
    den = None
    acc = None
    for s, v in parts:
        p = jnp.exp(s - m)
        d = p.sum(axis=-1, keepdims=True)
        den = d if den is None else den + d
        o = _dot(p.astype(BF16), v)
        acc = o if acc is None else acc + o
    if sink is not None:
        den = den + jnp.exp(sink - m)
    return acc / den


def _win_sample_body(q_ref, kvp_ref, kvc_ref, kvn_ref, ck_ref, cv_ref, sink_ref, dupk_ref, dupv_ref, o_ref):
    n = pl.program_id(1)
    seq = pl.num_programs(1) * WIN_BLOCK
    dupk, dupv = dupk_ref[...], dupv_ref[...]
    kv = jnp.concatenate([kvp_ref[...], kvc_ref[...], kvn_ref[...]], axis=0)
    k_loc = _dot(kv, dupk).astype(BF16)
    v_loc = _dot(kv, dupv).astype(BF16)
    k_ctx = _dot(ck_ref[...], dupk[:LANES]).astype(BF16)
    v_ctx = _dot(cv_ref[...], dupk[:LANES]).astype(BF16)
    q4 = _stack_heads(q_ref[...])
    s_loc = _dot_nt(q4, k_loc)
    s_ctx = _dot_nt(q4, k_ctx)
    shape = s_loc.shape
    t = n * WIN_BLOCK + lax.broadcasted_iota(jnp.int32, shape, 0) % WIN_BLOCK
    kpos = (n - 1) * WIN_BLOCK + lax.broadcasted_iota(jnp.int32, shape, 1)
    valid = (kpos >= 0) & (kpos < seq) & (jnp.abs(t - kpos) <= WINDOW)
    s_loc = jnp.where(valid, s_loc, MASK_VALUE)
    o4 = _softmax_pv([(s_loc, v_loc), (s_ctx, v_ctx)], sink_ref[...])
    o_ref[...] = _unstack_heads(o4, WIN_BLOCK).astype(BF16)


def _win_sample(wq, wkv, ctx_k, ctx_v, sink_col, consts):
    b, p, _ = ctx_k.shape
    nq = wq.shape[0] // b // WIN_BLOCK
    blk = (WIN_BLOCK, GROUP)
    return pl.pallas_call(
        _win_sample_body,
        grid=(b, nq),
        in_specs=[pl.BlockSpec(blk, lambda i, n: (i * nq + n, 0)),
                  pl.BlockSpec(blk, lambda i, n: (i * nq + jnp.maximum(n - 1, 0), 0)),
                  pl.BlockSpec(blk, lambda i, n: (i * nq + n, 0)),
                  pl.BlockSpec(blk, lambda i, n: (i * nq + jnp.minimum(n + 1, nq - 1), 0)),
                  pl.BlockSpec((None, p, LANES), lambda i, n: (i, 0, 0)),
                  pl.BlockSpec((None, p, LANES), lambda i, n: (i, 0, 0)),
                  _const_spec((N_HEADS * WIN_BLOCK, 1)),
                  _const_spec((GROUP, GROUP)), _const_spec((GROUP, GROUP))],
        out_specs=pl.BlockSpec(blk, lambda i, n: (i * nq + n, 0)),
        out_shape=_sds(wq.shape, BF16),
        compiler_params=_cparams("parallel", "parallel"),
        name="win_sample",
    )(wq, wkv, wkv, wkv, ctx_k, ctx_v, sink_col, consts["dupk"], consts["dupv"])


def _win_prompt_body(q_ref, kv_ref, sink_ref, dupk_ref, dupv_ref, o_ref):
    kv = kv_ref[...]
    k = _dot(kv, dupk_ref[...]).astype(BF16)
    v = _dot(kv, dupv_ref[...]).astype(BF16)
    q4 = _stack_heads(q_ref[...])
    o4 = _softmax_pv([(_dot_nt(q4, k), v)], sink_ref[...])
    o_ref[...] = _unstack_heads(o4, q_ref.shape[0]).astype(BF16)


def _win_prompt(wq, wkv, seq, sink_col, consts):
    b = wq.shape[0] // seq
    blk = (seq, GROUP)
    return pl.pallas_call(
        _win_prompt_body,
        grid=(b,),
        in_specs=[pl.BlockSpec(blk, lambda i: (i, 0)), pl.BlockSpec(blk, lambda i: (i, 0)),
                  _const_spec((N_HEADS * seq, 1)),
                  _const_spec((GROUP, GROUP)), _const_spec((GROUP, GROUP))],
        out_specs=pl.BlockSpec(blk, lambda i: (i, 0)),
        out_shape=_sds(wq.shape, BF16),
        compiler_params=_cparams("parallel"),
        name="win_prompt",
    )(wq, wkv, sink_col, consts["dupk"], consts["dupv"])


def _diff_body(has_ctx, *refs):
    if has_ctx:
        q_ref, k_ref, v_ref, ck_ref, cv_ref, lam_ref, gain_ref, g64_ref, o_ref, acc_ref = refs
    else:
        q_ref, k_ref, v_ref, lam_ref, gain_ref, g64_ref, o_ref, acc_ref = refs
    h = pl.program_id(2)
    tq = q_ref.shape[0]
    q = q_ref[...]
    sub = lax.broadcasted_iota(jnp.int32, (1, GROUP), 1) // DIFF_DIM
    zero = jnp.zeros_like(q)
    q2 = jnp.concatenate([jnp.where(sub == 2 * h, q, zero), jnp.where(sub == 2 * h + 1, q, zero)], axis=0)
    parts = [(_dot_nt(q2, k_ref[...]), v_ref[...])]
    if has_ctx:
        parts.append((_dot_nt(q2, ck_ref[...]), cv_ref[...]))
    o2 = _softmax_pv(parts, None)
    od = o2[:tq] - lam_ref[0, 0] * o2[tq:]
    contrib = jnp.where(_head_of_lane() == h, od, 0.0)

    @pl.when(h == 0)
    def _():
        acc_ref[...] = contrib

    @pl.when(h > 0)
    def _():
        acc_ref[...] += contrib

    @pl.when(h == N_HEADS - 1)
    def _():
        y = _group_rms(acc_ref[...], g64_ref[...], HEAD_DIM, gain_ref[...]) * lam_ref[0, 1]
        o_ref[...] = y.astype(BF16)


def _diff_attention(dq, dk, dv, ctx, seq, lam2, gain, consts, tq=256):
    b = dq.shape[0] // seq
    tq = min(tq, seq)
    nq = seq // tq
    in_specs = [pl.BlockSpec((tq, GROUP), lambda i, n, h: (i * nq + n, 0)),
                pl.BlockSpec((seq, GROUP), lambda i, n, h: (i, 0)),
                pl.BlockSpec((seq, GROUP), lambda i, n, h: (i, 0))]
    args = [dq, dk, dv]
    if ctx is not None:
        p = ctx[0].shape[1]
        in_specs += [pl.BlockSpec((None, p, GROUP), lambda i, n, h: (i, 0, 0))] * 2
        args += list(ctx)
    in_specs += [pl.BlockSpec(memory_space=pltpu.SMEM), _const_spec((1, GROUP)), _const_spec((GROUP, GROUP))]
    args += [lam2, gain, consts["g64"]]
    return pl.pallas_call(
        functools.partial(_diff_body, ctx is not None),
        grid=(b, nq, N_HEADS),
        in_specs=in_specs,
        out_specs=pl.BlockSpec((tq, GROUP), lambda i, n, h: (i * nq + n, 0)),
        out_shape=_sds(dq.shape, BF16),
        scratch_shapes=[pltpu.VMEM((tq, GROUP), F32)],
        compiler_params=_cparams("parallel", "parallel", "arbitrary"),
        name="diff_ctx" if ctx is not None else "diff",
    )(*args)


def _ret_operands(rope, rq, rk, cos, sin):
    q = rq
    k = rk * HEAD_DIM ** -0.5
    if rope:
        q = _rope(q, cos, sin, HEAD_DIM // 4)
        k = _rope(k, cos, sin, HEAD_DIM // 4)
    return q, k


def _hgrn_gate(z, lb):
    f = lb + (1.0 - lb) * jax.nn.sigmoid(z)
    return jnp.log(jnp.maximum(f, 1e-30)), (1.0 - lb) * jax.nn.sigmoid(-z)


def _row_index(rows):
    return lax.broadcasted_iota(jnp.int32, (rows, 1), 0)


def _segment_state_step(st_ref, k, v, cum, total, bd):
    ku = k * jnp.exp(total - cum)
    st_ref[...] = st_ref[...] * jnp.exp(total) + _dot_tn(v.astype(BF16), ku.astype(BF16)) * bd


def _states_body(gated, *refs):
    if gated:
        (zf_ref, vf_ref, zb_ref, vb_ref, lb_ref, tril_ref, triu_ref, s0f_ref, s0b_ref, bd_ref,
         ef_ref, eb_ref, stf, stb) = refs
    else:
        (kf_ref, vf_ref, kb_ref, vb_ref, cf_ref, sf_ref, cb_ref, sb_ref, lg_ref, s0f_ref, s0b_ref, bd_ref,
         ef_ref, eb_ref, stf, stb) = refs
    s = pl.program_id(1)

    @pl.when(s == 0)
    def _():
        stf[...] = s0f_ref[...]
        stb[...] = s0b_ref[...]

    ef_ref[...] = stf[...]
    eb_ref[...] = stb[...]
    bd = bd_ref[...]
    rows = vf_ref.shape[0]
    if gated:
        lff, kf = _hgrn_gate(zf_ref[...], lb_ref[0:1, :])
        lfb, kb = _hgrn_gate(zb_ref[...], lb_ref[1:2, :])
        cumf = _sum_rows(tril_ref[...], lff)
        cumb = _sum_rows(triu_ref[...], lfb)
        totf, totb = cumf[rows - 1:rows, :], cumb[0:1, :]
    else:
        _, kf = _ret_operands(True, kf_ref[...], kf_ref[...], cf_ref[...], sf_ref[...])
        _, kb = _ret_operands(True, kb_ref[...], kb_ref[...], cb_ref[...], sb_ref[...])
        i = _row_index(rows).astype(F32)
        lgf, lgb = lg_ref[0:1, :], lg_ref[1:2, :]
        cumf, totf = (i + 1.0) * lgf, rows * lgf
        cumb, totb = (rows - i) * lgb, rows * lgb
    _segment_state_step(stf, kf, vf_ref[...], cumf, totf, bd)
    _segment_state_step(stb, kb, vb_ref[...], cumb, totb, bd)


def _segment_states(gated, proj, n_b, seq, s0f, s0b, lane_rows, consts, tables):
    ns = seq // SEG
    blk = (SEG, GROUP)

    def fwd(cb):
        return pl.BlockSpec(blk, lambda i, s, cb=cb: (i * ns + s, cb))

    def bwd(cb):
        return pl.BlockSpec(blk, lambda i, s, cb=cb: (i * ns + ns - 1 - s, cb))

    st_spec = pl.BlockSpec((None, GROUP, GROUP), lambda i, s: (i, 0, 0))
    if gated:
        in_specs = [fwd(CB_HZF), fwd(CB_HI), bwd(CB_HZB), bwd(CB_HI), _const_spec((2, GROUP)),
                    _const_spec((SEG, SEG)), _const_spec((SEG, SEG))]
        args = [proj] * 4 + [lane_rows, consts["tril_seg"], consts["triu_seg"]]
    else:
        tf = pl.BlockSpec((SEG, LANES), lambda i, s: (s, 0))
        tb = pl.BlockSpec((SEG, LANES), lambda i, s: (ns - 1 - s, 0))
        in_specs = [fwd(CB_RK), fwd(CB_RV), bwd(CB_RK), bwd(CB_RV), tf, tf, tb, tb, _const_spec((2, GROUP))]
        args = [proj] * 4 + [tables[0], tables[1], tables[0], tables[1], lane_rows]
    in_specs += [st_spec, st_spec, _const_spec((GROUP, GROUP))]
    args += [s0f, s0b, consts["bd"]]
    e_shape = _sds((n_b, ns, GROUP, GROUP), F32)
    return pl.pallas_call(
        functools.partial(_states_body, gated),
        grid=(n_b, ns),
        in_specs=in_specs,
        out_specs=[pl.BlockSpec((None, None, GROUP, GROUP), lambda i, s: (i, s, 0, 0)),
                   pl.BlockSpec((None, None, GROUP, GROUP), lambda i, s: (i, ns - 1 - s, 0, 0))],
        out_shape=[e_shape, e_shape],
        scratch_shapes=[pltpu.VMEM((GROUP, GROUP), F32), pltpu.VMEM((GROUP, GROUP), F32)],
        compiler_params=_cparams("parallel", "arbitrary"),
        name="hgrn_states" if gated else "ret_states",
    )(*args)


def _ret_out_body(rope, has_state, emit_state, *refs):
    rq_ref, rk_ref, rv_ref, rg_ref = refs[:4]
    refs = refs[4:]
    cos = sin = None
    if rope:
        cos, sin = refs[0][...], refs[1][...]
        refs = refs[2:]
    lgs_ref, lg_ref, gain_ref, g64_ref = refs[:4]
    refs = refs[4:]
    if has_state:
        ef_ref, eb_ref = refs[:2]
        refs = refs[2:]
    o_ref = refs[0]
    rows = rq_ref.shape[0]
    q, k = _ret_operands(rope, rq_ref[...], rk_ref[...], cos, sin)
    v = rv_ref[...]
    kb16, vb16 = k.astype(BF16), v.astype(BF16)
    s4 = _dot_nt(_stack_heads(q.astype(BF16)), kb16)
    d = (lax.broadcasted_iota(jnp.int32, (rows, rows), 0) - lax.broadcasted_iota(jnp.int32, (rows, rows), 1)).astype(F32)
    slabs = []
    for h in range(N_HEADS):
        m = (jnp.where(d >= 0, jnp.exp(jnp.maximum(d, 0.0) * lgs_ref[0, h]), 0.0)
             + jnp.where(d <= 0, jnp.exp(jnp.maximum(-d, 0.0) * lgs_ref[1, h]), 0.0))
        slabs.append((s4[h * rows:(h + 1) * rows] * m).astype(BF16))
    o = _unstack_heads(_dot(jnp.concatenate(slabs, axis=0), vb16), rows)
    i = _row_index(rows).astype(F32)
    lgf, lgb = lg_ref[0:1, :], lg_ref[1:2, :]
    if has_state:
        o = o + _dot_nt((q * jnp.exp((i + 1.0) * lgf)).astype(BF16), ef_ref[...].astype(BF16))
        o = o + _dot_nt((q * jnp.exp((rows - i) * lgb)).astype(BF16), eb_ref[...].astype(BF16))
    y = _group_rms(o, g64_ref[...], HEAD_DIM, gain_ref[...]) * _silu(rg_ref[...])
    o_ref[...] = y.astype(BF16)
    if emit_state:
        sf_ref, sb_ref = refs[1:3]
        sf_ref[...] = _dot_tn(vb16, (k * jnp.exp((rows - 1.0 - i) * lgf)).astype(BF16))
        sb_ref[...] = _dot_tn(vb16, (k * jnp.exp(i * lgb)).astype(BF16))


def _ret_out(proj, row0, n_b, seq, lg_smem, lg_rows, gain, consts, tables, states, emit_state):
    ns = seq // SEG
    nb0 = row0 // SEG
    rope = tables is not None
    blk = (SEG, GROUP)

    def col(cb):
        return pl.BlockSpec(blk, lambda i, s, cb=cb: (nb0 + i * ns + s, cb))

    in_specs = [col(CB_RQ), col(CB_RK), col(CB_RV), col(CB_RG)]
    args = [proj] * 4
    if rope:
        in_specs += [pl.BlockSpec((SEG, LANES), lambda i, s: (s, 0))] * 2
        args += list(tables)
    in_specs += [pl.BlockSpec(memory_space=pltpu.SMEM), _const_spec((2, GROUP)), _const_spec((1, GROUP)),
                 _const_spec((GROUP, GROUP))]
    args += [lg_smem, lg_rows, gain, consts["g64"]]
    if states is not None:
        in_specs += [pl.BlockSpec((None, None, GROUP, GROUP), lambda i, s: (i, s, 0, 0))] * 2
        args += list(states)
    out_specs = [pl.BlockSpec(blk, lambda i, s: (i * ns + s, 0))]
    out_shape = [_sds((n_b * seq, GROUP), BF16)]
    if emit_state:
        assert ns == 1
        out_specs += [pl.BlockSpec((None, GROUP, GROUP), lambda i, s: (i, 0, 0))] * 2
        out_shape += [_sds((n_b, GROUP, GROUP), F32)] * 2
    return pl.pallas_call(
        functools.partial(_ret_out_body, rope, states is not None, emit_state),
        grid=(n_b, ns),
        in_specs=in_specs, out_specs=out_specs, out_shape=out_shape,
        compiler_params=_cparams("parallel", "parallel"),
        name="ret_out_ctx" if states is not None else "ret_out",
    )(*args)


def _hgrn_direction(q, k, v, cum, tot, st, g64, bd, forward):
    rows = q.shape[0]
    nblk = rows // REC_BLOCK
    rowmod = _row_index(rows) % REC_BLOCK
    o = None
    for lag in range(REC_BLOCK):
        if lag == 0:
            p = q * k
            vl = v
        else:
            shift = lag if forward else rows - lag
            mask = (rowmod >= lag) if forward else (rowmod <= REC_BLOCK - 1 - lag)
            kl = pltpu.roll(k, shift, 0)
            cl = pltpu.roll(cum, shift, 0)
            vl = pltpu.roll(v, shift, 0)
            e = jnp.exp(jnp.where(mask, cum - cl, 0.0))
            p = jnp.where(mask, q * kl * e, 0.0)
        a = _dot(p.astype(BF16), g64)
        o = a * vl if o is None else o + a * vl
    qd = (q * jnp.exp(cum)).astype(BF16)
    ku = (k * jnp.exp(tot - cum)).astype(BF16)
    dec = jnp.exp(tot)
    vb = v.astype(BF16)
    inter = [None] * nblk
    for n in (range(nblk) if forward else range(nblk - 1, -1, -1)):
        r0, r1 = n * REC_BLOCK, (n + 1) * REC_BLOCK
        inter[n] = _dot_nt(qd[r0:r1], st.astype(BF16))
        st = st * dec[r0:r0 + 1, :] + _dot_tn(vb[r0:r1], ku[r0:r1]) * bd
    return o + jnp.concatenate(inter, axis=0), st


def _hgrn_out_body(has_state, emit_state, *refs):
    (q_ref, zf_ref, zb_ref, v_ref, g_ref, lb_ref, gain_ref, g64_ref, bd_ref,
     tril_ref, triu_ref, same_ref) = refs[:12]
    refs = refs[12:]
    bd = bd_ref[...]
    g64 = g64_ref[...]
    if has_state:
        stf, stb = refs[0][...], refs[1][...]
        refs = refs[2:]
    else:
        stf = stb = jnp.zeros((GROUP, GROUP), F32)
    o_ref = refs[0]
    q, v = q_ref[...], v_ref[...]
    lff, kf = _hgrn_gate(zf_ref[...], lb_ref[0:1, :])
    lfb, kb = _hgrn_gate(zb_ref[...], lb_ref[1:2, :])
    same = same_ref[...]
    of, stf = _hgrn_direction(q, kf, v, _sum_rows(tril_ref[...], lff), _sum_rows(same, lff), stf, g64, bd, True)
    ob, stb = _hgrn_direction(q, kb, v, _sum_rows(triu_ref[...], lfb), _sum_rows(same, lfb), stb, g64, bd, False)
    y = _group_rms(of + ob, g64, HEAD_DIM, gain_ref[...]) * _silu(g_ref[...])
    o_ref[...] = y.astype(BF16)
    if emit_state:
        refs[1][...] = stf
        refs[2][...] = stb


def _hgrn_out(proj, row0, n_b, seq, lb_rows, gain, consts, states, emit_state):
    ns = seq // SEG
    nb0 = row0 // SEG
    blk = (SEG, GROUP)

    def col(cb):
        return pl.BlockSpec(blk, lambda i, s, cb=cb: (nb0 + i * ns + s, cb))

    in_specs = [col(CB_HQ), col(CB_HZF), col(CB_HZB), col(CB_HI), col(CB_HG),
                _const_spec((2, GROUP)), _const_spec((1, GROUP)), _const_spec((GROUP, GROUP)),
                _const_spec((GROUP, GROUP)), _const_spec((SEG, SEG)), _const_spec((SEG, SEG)), _const_spec((SEG, SEG))]
    args = [proj] * 5 + [lb_rows, gain, consts["g64"], consts["bd"],
                         consts["tril_blk"], consts["triu_blk"], consts["same_blk"]]
    if states is not None:
        in_specs += [pl.BlockSpec((None, None, GROUP, GROUP), lambda i, s: (i, s, 0, 0))] * 2
        args += list(states)
    out_specs = [pl.BlockSpec(blk, lambda i, s: (i * ns + s, 0))]
    out_shape = [_sds((n_b * seq, GROUP), BF16)]
    if emit_state:
        assert ns == 1
        out_specs += [pl.BlockSpec((None, GROUP, GROUP), lambda i, s: (i, 0, 0))] * 2
        out_shape += [_sds((n_b, GROUP, GROUP), F32)] * 2
    return pl.pallas_call(
        functools.partial(_hgrn_out_body, states is not None, emit_state),
        grid=(n_b, ns),
        in_specs=in_specs, out_specs=out_specs, out_shape=out_shape,
        compiler_params=_cparams("parallel", "parallel"),
        name="hgrn_out_ctx" if states is not None else "hgrn_out",
    )(*args)


def _ffn_body(x_ref, oa_ref, ob_ref, oc_ref, od_ref, wo_ref, g1_ref, n2_ref, sc_ref, sh_ref, g2_ref,
              wg_ref, wu_ref, wd_ref, y_ref, x1_ref, h_ref, acc_ref):
    j = pl.program_id(1)

    @pl.when(j == 0)
    def _():
        mix = _dot(oa_ref[...], wo_ref[0:GROUP, :])
        mix += _dot(ob_ref[...], wo_ref[GROUP:2 * GROUP, :])
        mix += _dot(oc_ref[...], wo_ref[2 * GROUP:3 * GROUP, :])
        mix += _dot(od_ref[...], wo_ref[3 * GROUP:4 * GROUP, :])
        x1 = x_ref[...] + g1_ref[...] * mix
        x1_ref[...] = x1
        h = x1 * lax.rsqrt(jnp.mean(x1 * x1, axis=-1, keepdims=True) + EPS) * n2_ref[...]
        h_ref[...] = (h * (1.0 + sc_ref[...]) + sh_ref[...]).astype(BF16)
        acc_ref[...] = jnp.zeros_like(acc_ref)

    h = h_ref[...]
    a = _silu(_dot(h, wg_ref[...])) * _dot(h, wu_ref[...])
    acc_ref[...] += _dot(a.astype(BF16), wd_ref[...])

    @pl.when(j == pl.num_programs(1) - 1)
    def _():
        y_ref[...] = x1_ref[...] + g2_ref[...] * acc_ref[...]


def _outproj_ffn(x, mixes, w_out, g1, n2, sc2, sh2, g2, w_in, w_dn, seq_s, n_s, tm=1024, th=256):
    t, d = x.shape
    hid = w_dn.shape[0]
    nh = hid // th
    mod_map = _mod_row_map(tm, seq_s, n_s)
    mod2 = lambda i, j: mod_map(i)
    row = lambda i, j: (i, 0)
    mod_spec = pl.BlockSpec((None, 1, d), mod2)
    return pl.pallas_call(
        _ffn_body,
        grid=(t // tm, nh),
        in_specs=[pl.BlockSpec((tm, d), row)] + [pl.BlockSpec((tm, GROUP), row)] * 4
                 + [_const_spec((d, d)), mod_spec, _const_spec((1, d)), mod_spec, mod_spec, mod_spec,
                    pl.BlockSpec((d, th), lambda i, j: (0, j)),
                    pl.BlockSpec((d, th), lambda i, j: (0, nh + j)),
                    pl.BlockSpec((th, d), lambda i, j: (j, 0))],
        out_specs=pl.BlockSpec((tm, d), row),
        out_shape=_sds((t, d), F32),
        scratch_shapes=[pltpu.VMEM((tm, d), F32), pltpu.VMEM((tm, d), BF16), pltpu.VMEM((tm, d), F32)],
        compiler_params=_cparams("parallel", "arbitrary"),
        name="outproj_ffn",
    )(x, *mixes, w_out, g1, n2, sc2, sh2, g2, w_in, w_in, w_dn)


def _lane_rows(per_head):
    return jnp.repeat(per_head.astype(F32), HEAD_DIM, axis=1)


def _states_to_lanes(s):
    b = s.shape[0]
    eye = jnp.eye(N_HEADS, dtype=F32)
    st = jnp.swapaxes(s.astype(F32), -1, -2)[:, :, :, :, None, :] * eye[None, None, :, None, :, None]
    st = st.reshape(b, 2, GROUP, GROUP)
    return st[:, 0], st[:, 1]


def _lanes_to_states(sf, sb):
    def diag(x):
        b = x.shape[0]
        x = x.reshape(b, N_HEADS, HEAD_DIM, N_HEADS, HEAD_DIM)
        return jnp.stack([x[:, h, :, h, :] for h in range(N_HEADS)], axis=1).swapaxes(-1, -2)
    return jnp.stack([diag(sf), diag(sb)], axis=1)


def kernel(x_prompt, x_sample, state_ret, cache_win_k, cache_win_v, cache_diff_k, cache_diff_v, state_hgrn, c, c_ctx, norm1_g, norm2_g, w_ada, b_ada, w_in, ret_decay, ret_norm_g, win_q_norm, win_k_norm, win_sink, diff_q_norm, diff_k_norm, diff_lambda, diff_norm_g, hgrn_lb_logits, hgrn_norm_g, w_out, w_ffn_in, w_ffn_out):
    n_p, seq_p, d = x_prompt.shape
    n_s, seq_s, _ = x_sample.shape
    depth = w_in.shape[0]
    t_s, t_p = n_s * seq_s, n_p * seq_p
    assert seq_p == SEG and seq_s % SEG == 0 and d == N_HEADS * GROUP

    tril_seg, triu_seg, _ = _block_matrices(SEG, SEG)
    tril_blk, triu_blk, same_blk = _block_matrices(SEG, REC_BLOCK)
    dupk, dupv = _kv_dup_matrices()
    consts = dict(g64=_group_matrix(GROUP, HEAD_DIM), g32=_group_matrix(GROUP, DIFF_DIM),
                  bd=_group_matrix(GROUP, HEAD_DIM).astype(F32), dupk=dupk, dupv=dupv,
                  tril_seg=tril_seg, triu_seg=triu_seg, tril_blk=tril_blk, triu_blk=triu_blk, same_blk=same_blk)
    tab64 = _rope_tables(seq_s, HEAD_DIM)
    tab32 = _rope_tables(seq_s, DIFF_DIM)

    n_rows = -(-(n_s + 1) // 8) * 8
    cond = jnp.zeros((n_rows, d), F32).at[:n_s].set(c).at[n_s].set(c_ctx)
    mod = _adaln(cond, w_ada, b_ada).reshape(depth, n_rows, 6, 1, d)

    lb_p = jax.nn.softmax(hgrn_lb_logits.astype(F32), axis=0)
    lb_all = jnp.cumsum(lb_p, axis=0) - lb_p
    log_gamma = -jnp.exp(ret_decay.astype(F32))

    x = jnp.concatenate([x_sample.reshape(t_s, d), x_prompt.reshape(t_p, d)], axis=0)
    new_ret, new_wk, new_wv, new_dk, new_dv, new_hg = [], [], [], [], [], []
    for l in range(depth):
        lam_init = 0.8 - 0.6 * math.exp(-0.3 * l)
        sh1, sc1, g1, sh2, sc2, g2 = [mod[l, :, i] for i in range(6)]
        proj = _inproj(x, norm1_g[l][None], sc1, sh1, w_in[l].astype(BF16), seq_s, n_s)

        gains = (jnp.tile(win_q_norm[l], N_HEADS)[None], jnp.tile(win_k_norm[l], KV_HEADS)[None],
                 jnp.tile(diff_q_norm[l], 2 * N_HEADS)[None], jnp.tile(diff_k_norm[l], 2 * N_HEADS)[None])
        wq_s, wkv_s, dq_s, dk_s, dv_s = _prep(proj, 0, t_s, gains, consts, tab64 + tab32, False)
        wq_p, wkv_p, dq_p, dk_p, dv_p, wk_f, dk_f = _prep(proj, t_s, t_p, gains, consts, None, True)

        sink = win_sink[l].astype(F32)
        ctx_wk = cache_win_k[:, l].transpose(0, 2, 1, 3).reshape(n_s, -1, LANES).astype(BF16)
        ctx_wv = cache_win_v[:, l].transpose(0, 2, 1, 3).reshape(n_s, -1, LANES).astype(BF16)
        o_win_s = _win_sample(wq_s, wkv_s, ctx_wk, ctx_wv, jnp.repeat(sink, WIN_BLOCK)[:, None], consts)
        o_win_p = _win_prompt(wq_p, wkv_p, seq_p, jnp.repeat(sink, seq_p)[:, None], consts)

        lq1, lk1, lq2, lk2 = diff_lambda[l].astype(F32)
        lam = jnp.exp(jnp.sum(lq1 * lk1)) - jnp.exp(jnp.sum(lq2 * lk2)) + lam_init
        lam2 = jnp.stack([lam, jnp.asarray(1.0 - lam_init, F32)]).reshape(1, 2)
        dgain = diff_norm_g[l][None]
        ctx_dk = cache_diff_k[:, l].transpose(0, 3, 1, 2, 4).reshape(n_s, -1, GROUP).astype(BF16)
        ctx_dv = cache_diff_v[:, l].transpose(0, 2, 1, 3).reshape(n_s, -1, GROUP).astype(BF16)
        o_diff_s = _diff_attention(dq_s, dk_s, dv_s, (ctx_dk, ctx_dv), seq_s, lam2, dgain, consts)
        o_diff_p = _diff_attention(dq_p, dk_p, dv_p, None, seq_p, lam2, dgain, consts)

        lg = log_gamma[l]
        lg_rows = _lane_rows(lg)
        rgain = ret_norm_g[l][None]
        s0f, s0b = _states_to_lanes(state_ret[:, l])
        ret_e = _segment_states(False, proj, n_s, seq_s, s0f, s0b, lg_rows, consts, tab64)
        (o_ret_s,) = _ret_out(proj, 0, n_s, seq_s, lg, lg_rows, rgain, consts, tab64, ret_e, False)
        o_ret_p, rsf, rsb = _ret_out(proj, t_s, n_p, seq_p, lg, lg_rows, rgain, consts, None, None, True)

        lb_rows = lb_all[l]
        hgain = hgrn_norm_g[l][None]
        h0f, h0b = _states_to_lanes(state_hgrn[:, l])
        hg_e = _segment_states(True, proj, n_s, seq_s, h0f, h0b, lb_rows, consts, None)
        (o_h_s,) = _hgrn_out(proj, 0, n_s, seq_s, lb_rows, hgain, consts, hg_e, False)
        o_h_p, hsf, hsb = _hgrn_out(proj, t_s, n_p, seq_p, lb_rows, hgain, consts, None, True)

        mixes = [jnp.concatenate([a, b], axis=0) for a, b in
                 ((o_ret_s, o_ret_p), (o_win_s, o_win_p), (o_diff_s, o_diff_p), (o_h_s, o_h_p))]
        x = _outproj_ffn(x, mixes, w_out[l].astype(BF16), g1, norm2_g[l][None], sc2, sh2, g2,
                         w_ffn_in[l].astype(BF16), w_ffn_out[l].astype(BF16), seq_s, n_s)

        pp = proj[t_s:]
        new_ret.append(_lanes_to_states(rsf, rsb))
        new_hg.append(_lanes_to_states(hsf, hsb))
        new_wk.append(wk_f.reshape(n_p, seq_p, KV_HEADS, HEAD_DIM).transpose(0, 2, 1, 3))
        new_wv.append(pp[:, CB_WKV * GROUP + LANES:(CB_WKV + 1) * GROUP]
                      .reshape(n_p, seq_p, KV_HEADS, HEAD_DIM).transpose(0, 2, 1, 3))
        new_dk.append(dk_f.reshape(n_p, seq_p, N_HEADS, 2, DIFF_DIM).transpose(0, 2, 3, 1, 4))
        new_dv.append(pp[:, CB_DV * GROUP:(CB_DV + 1) * GROUP]
                      .reshape(n_p, seq_p, N_HEADS, HEAD_DIM).transpose(0, 2, 1, 3))

    y_s = x[:t_s].reshape(n_s, seq_s, d)
    y_p = x[t_s:].reshape(n_p, seq_p, d)
    return (y_p, y_s, jnp.stack(new_ret, axis=1), jnp.stack(new_wk, axis=1), jnp.stack(new_wv, axis=1),
            jnp.stack(new_dk, axis=1), jnp.stack(new_dv, axis=1), jnp.stack(new_hg, axis=1))
```

```python
import functools
import math

import numpy as np
import jax
import jax.numpy as jnp
from jax import lax
from jax.experimental import pallas as pl
from jax.experimental.pallas import tpu as pltpu

F32 = jnp.float32
BF16 = jnp.bfloat16

GROUP = 256
HEAD_DIM = 64
N_HEADS = 4
KV_HEADS = 2
DIFF_DIM = 32
WINDOW = 128
WIN_BLOCK = 128
GRID_W = 64
ROPE_BASE = 10000.0
EPS = 1e-6
MASK_VALUE = -1e30
SEG = 256
LANES = 128
BF16_SUBLANES = 16
LOG2_E = math.log2(math.e)
DIFF_V_ROWS = HEAD_DIM + BF16_SUBLANES
DIFF_LOOKAHEAD = 2
V7X_VMEM_LIMIT_BYTES = 56 * 1024 * 1024

(CB_RQ, CB_RK, CB_RV, CB_RG, CB_WQ, CB_WKV, CB_DQ, CB_DK, CB_DV,
 CB_HQ, CB_HZF, CB_HZB, CB_HI, CB_HG) = range(14)

_NT = (((1,), (1,)), ((), ()))
_TN = (((0,), (0,)), ((), ()))


def _sds(shape, dtype):
    return jax.ShapeDtypeStruct(shape, dtype)


def _cparams(*sem):
    return pltpu.CompilerParams(dimension_semantics=sem, vmem_limit_bytes=V7X_VMEM_LIMIT_BYTES)


def _const_spec(shape):
    return pl.BlockSpec(shape, lambda *_: (0,) * len(shape))


def _dot(a, b):
    return jnp.dot(a, b, preferred_element_type=F32)


def _dot_nt(a, b):
    return lax.dot_general(a, b, _NT, preferred_element_type=F32)


def _dot_tn(a, b):
    return lax.dot_general(a, b, _TN, preferred_element_type=F32)


def _silu(x):
    return x * jax.nn.sigmoid(x)


def _group_matrix(width, gsize):
    i = np.arange(width)
    return jnp.asarray((i[:, None] // gsize) == (i[None, :] // gsize), BF16)


def _cumsum_matrices(n):
    r, c = np.arange(n)[:, None], np.arange(n)[None, :]
    return jnp.asarray(c <= r, BF16), jnp.asarray(c >= r, BF16)


def _kv_dup_matrices():
    dk = np.zeros((GROUP, GROUP), np.float32)
    dv = np.zeros((GROUP, GROUP), np.float32)
    for h in range(N_HEADS):
        kv = h // (N_HEADS // KV_HEADS)
        for j in range(HEAD_DIM):
            dk[kv * HEAD_DIM + j, h * HEAD_DIM + j] = 1.0
            dv[LANES + kv * HEAD_DIM + j, h * HEAD_DIM + j] = 1.0
    return jnp.asarray(dk, BF16), jnp.asarray(dv, BF16)


def _rope_tables(n_tokens, head_dim):
    d = head_dim // 2
    half = d // 2
    inv = ROPE_BASE ** (-jnp.arange(half, dtype=F32) / half)
    t = jnp.arange(n_tokens)
    row = (t // GRID_W).astype(F32)
    col = (t % GRID_W).astype(F32)
    j = np.arange(LANES) % head_dim
    w = j % d
    use_row = jnp.asarray((j // d) == 0)
    pos = jnp.where(use_row[None, :], row[:, None], col[:, None])
    ang = pos * inv[w % half][None, :]
    sign = jnp.asarray(np.where(w >= half, 1.0, -1.0), F32)
    return jnp.cos(ang), jnp.sin(ang) * sign[None, :]


def _group_rms(x, g_mat, gsize, gain):
    ss = _dot((x * x).astype(BF16), g_mat)
    return x * lax.rsqrt(ss * (1.0 / gsize) + EPS) * gain


def _rope(y, cos, sin, half):
    lane = lax.broadcasted_iota(jnp.int32, (1, LANES), 1)
    second = (lane % (2 * half)) >= half
    outs = []
    for p in range(y.shape[1] // LANES):
        z = y[:, p * LANES:(p + 1) * LANES]
        partner = jnp.where(second, pltpu.roll(z, half, 1), pltpu.roll(z, LANES - half, 1))
        outs.append(z * cos + partner * sin)
    return outs[0] if len(outs) == 1 else jnp.concatenate(outs, axis=1)


def _head_of_lane(width=GROUP):
    return lax.broadcasted_iota(jnp.int32, (1, width), 1) // HEAD_DIM


def _stack_heads(q, n_heads=N_HEADS):
    head = _head_of_lane()
    zero = jnp.zeros_like(q)
    return jnp.concatenate([jnp.where(head == h, q, zero) for h in range(n_heads)], axis=0)


def _unstack_heads(o4, rows):
    head = _head_of_lane()
    out = jnp.where(head == 0, o4[:rows], 0.0)
    for h in range(1, N_HEADS):
        out = out + jnp.where(head == h, o4[h * rows:(h + 1) * rows], 0.0)
    return out


def _split3(x):
    hi = x.astype(BF16)
    r = x - hi.astype(F32)
    mid = r.astype(BF16)
    lo = (r - mid.astype(F32)).astype(BF16)
    return hi, mid, lo


def _sum_rows(m01, x):
    hi, mid, lo = _split3(x)
    return _dot(m01, hi) + _dot(m01, mid) + _dot(m01, lo)


def _adaln_body(c_ref, w_ref, b_ref, o_ref):
    c = c_ref[...]
    o_ref[...] = jnp.dot(_silu(c), w_ref[...], preferred_element_type=F32,
                         precision=lax.Precision.HIGHEST) + b_ref[...]


def _adaln(cond, w_ada, b_ada):
    depth, d, n = w_ada.shape
    rows = cond.shape[0]
    tn = n // 4
    return pl.pallas_call(
        _adaln_body,
        grid=(depth, n // tn),
        in_specs=[pl.BlockSpec((rows, d), lambda l, j: (0, 0)),
                  pl.BlockSpec((None, d, tn), lambda l, j: (l, 0, j)),
                  pl.BlockSpec((None, 1, tn), lambda l, j: (l, 0, j))],
        out_specs=pl.BlockSpec((None, rows, tn), lambda l, j: (l, 0, j)),
        out_shape=_sds((depth, rows, n), F32),
        compiler_params=_cparams("parallel", "parallel"),
        name="adaln",
    )(cond, w_ada, b_ada.reshape(depth, 1, n))


def _inproj_body(x_ref, g_ref, sc_ref, sh_ref, w_ref, o_ref):
    x = x_ref[...]
    h = x * lax.rsqrt(jnp.mean(x * x, axis=-1, keepdims=True) + EPS) * g_ref[...]
    h = h * (1.0 + sc_ref[...]) + sh_ref[...]
    o_ref[...] = _dot(h.astype(BF16), w_ref[...])


def _mod_row_map(tm, rows_per_mod, mod_row0):
    return lambda i: (mod_row0 + (i * tm) // rows_per_mod, 0, 0)


def _inproj(x, g, sc, sh, w, rows_per_mod, mod_row0, tm=512):
    t, d = x.shape
    n = w.shape[1]
    mod_map = _mod_row_map(tm, rows_per_mod, mod_row0)
    return pl.pallas_call(
        _inproj_body,
        grid=(t // tm,),
        in_specs=[pl.BlockSpec((tm, d), lambda i: (i, 0)),
                  _const_spec((1, d)),
                  pl.BlockSpec((None, 1, d), mod_map),
                  pl.BlockSpec((None, 1, d), mod_map),
                  _const_spec((d, n))],
        out_specs=pl.BlockSpec((tm, n), lambda i: (i, 0)),
        out_shape=_sds((t, n), F32),
        compiler_params=_cparams("parallel"),
        name="inproj",
    )(x, g, sc, sh, w)


def _prep_body(rope, emit_f32, *refs):
    (wq_ref, wkv_ref, dq_ref, dk_ref, dv_ref, qn_ref, kn_ref, dqn_ref, dkn_ref, g64_ref, g32_ref) = refs[:11]
    refs = refs[11:]
    if rope:
        c64_ref, s64_ref, c32_ref, s32_ref = refs[:4]
        refs = refs[4:]
    wq_o, wkv_o, dq_o, dk_o, dvt_o = refs[:5]
    g64 = g64_ref[...]
    g32 = g32_ref[...]

    wq = _group_rms(wq_ref[...], g64, HEAD_DIM, qn_ref[...])
    wkv = wkv_ref[...]
    wk = _group_rms(wkv[:, :LANES], g64[:LANES, :LANES], HEAD_DIM, kn_ref[...])
    dq = _group_rms(dq_ref[...], g32, DIFF_DIM, dqn_ref[...])
    dk = _group_rms(dk_ref[...], g32, DIFF_DIM, dkn_ref[...])
    if rope:
        c64, s64, c32, s32 = c64_ref[...], s64_ref[...], c32_ref[...], s32_ref[...]
        wq = _rope(wq, c64, s64, HEAD_DIM // 4)
        wk = _rope(wk, c64, s64, HEAD_DIM // 4)
        dq = _rope(dq, c32, s32, DIFF_DIM // 4)
        dk = _rope(dk, c32, s32, DIFF_DIM // 4)
    wq_o[...] = (wq * HEAD_DIM ** -0.5).astype(BF16)
    wkv_o[...] = jnp.concatenate([wk, wkv[:, LANES:]], axis=1).astype(BF16)
    dq_o[...] = (dq * (DIFF_DIM ** -0.5 * LOG2_E)).astype(BF16)
    dk_o[...] = dk.astype(BF16)
    tm = dv_ref.shape[0]
    dvt_o[:, :HEAD_DIM, :] = dv_ref[...].T.reshape(N_HEADS, HEAD_DIM, tm).astype(BF16)
    dvt_o[:, HEAD_DIM:, :] = jnp.ones((N_HEADS, DIFF_V_ROWS - HEAD_DIM, tm), BF16)
    if emit_f32:
        wk_f, dk_f = refs[5:7]
        wk_f[...] = wk
        dk_f[...] = dk


def _prep(proj, gains, consts, tables, emit_f32, tm=256):
    rope = tables is not None
    rows = proj.shape[0]

    def col(cb):
        return pl.BlockSpec((tm, GROUP), lambda i, cb=cb: (i, cb))

    in_specs = [col(CB_WQ), col(CB_WKV), col(CB_DQ), col(CB_DK), col(CB_DV),
                _const_spec((1, GROUP)), _const_spec((1, LANES)), _const_spec((1, GROUP)), _const_spec((1, GROUP)),
                _const_spec((GROUP, GROUP)), _const_spec((GROUP, GROUP))]
    args = [proj] * 5 + list(gains) + [consts["g64"], consts["g32"]]
    if rope:
        seq = tables[0].shape[0]
        nper = seq // tm
        in_specs += [pl.BlockSpec((tm, LANES), lambda i: (i % nper, 0))] * 4
        args += list(tables)
    out = pl.BlockSpec((tm, GROUP), lambda i: (i, 0))
    out_specs = [out] * 4 + [pl.BlockSpec((N_HEADS, DIFF_V_ROWS, tm), lambda i: (0, 0, i))]
    out_shape = [_sds((rows, GROUP), BF16)] * 4 + [_sds((N_HEADS, DIFF_V_ROWS, rows), BF16)]
    if emit_f32:
        out_specs += [pl.BlockSpec((tm, LANES), lambda i: (i, 0)), out]
        out_shape += [_sds((rows, LANES), F32), _sds((rows, GROUP), F32)]
    return pl.pallas_call(
        functools.partial(_prep_body, rope, emit_f32),
        grid=(rows // tm,),
        in_specs=in_specs, out_specs=out_specs, out_shape=out_shape,
        compiler_params=_cparams("parallel"),
        name="prep_rope" if rope else "prep",
    )(*args)


def _softmax_pv(parts, sink, den_extra=None):
    m = parts[0][0].max(axis=-1, keepdims=True)
    for s, _ in parts[1:]:
        m = jnp.maximum(m, s.max(axis=-1, keepdims=True))
    if sink is not None:
        m = jnp.maximum(m, sink)
    den = None
    acc = None
    for s, v in parts:
        p = jnp.exp(s - m)
        d = p.sum(axis=-1, keepdims=True)
        den = d if den is None else den + d
        o = _dot(p.astype(BF16), v)
        acc = o if acc is None else acc + o
    if sink is not None:
        den = den + jnp.exp(sink - m)
    return acc / den


def _win_sample_body(q_ref, kvp_ref, kvc_ref, kvn_ref, ck_ref, cv_ref, sink_ref, dupk_ref, dupv_ref, o_ref):
    n = pl.program_id(1)
    seq = pl.num_programs(1) * WIN_BLOCK
    dupk, dupv = dupk_ref[...], dupv_ref[...]
    kv = jnp.concatenate([kvp_ref[...], kvc_ref[...], kvn_ref[...]], axis=0)
    k_loc = _dot(kv, dupk).astype(BF16)
    v_loc = _dot(kv, dupv).astype(BF16)
    k_ctx = _dot(ck_ref[...], dupk[:LANES]).astype(BF16)
    v_ctx = _dot(cv_ref[...], dupk[:LANES]).astype(BF16)
    q4 = _stack_heads(q_ref[...])
    s_loc = _dot_nt(q4, k_loc)
    s_ctx = _dot_nt(q4, k_ctx)
    shape = s_loc.shape
    t = n * WIN_BLOCK + lax.broadcasted_iota(jnp.int32, shape, 0) % WIN_BLOCK
    kpos = (n - 1) * WIN_BLOCK + lax.broadcasted_iota(jnp.int32, shape, 1)
    valid = (kpos >= 0) & (kpos < seq) & (jnp.abs(t - kpos) <= WINDOW)
    s_loc = jnp.where(valid, s_loc, MASK_VALUE)
    o4 = _softmax_pv([(s_loc, v_loc), (s_ctx, v_ctx)], sink_ref[...])
    o_ref[...] = _unstack_heads(o4, WIN_BLOCK).astype(BF16)


def _win_sample(wq, wkv, ctx_k, ctx_v, sink_col, consts):
    b, p, _ = ctx_k.shape
    nq = wq.shape[0] // b // WIN_BLOCK
    blk = (WIN_BLOCK, GROUP)
    return pl.pallas_call(
        _win_sample_body,
        grid=(b, nq),
        in_specs=[pl.BlockSpec(blk, lambda i, n: (i * nq + n, 0)),
                  pl.BlockSpec(blk, lambda i, n: (i * nq + jnp.maximum(n - 1, 0), 0)),
                  pl.BlockSpec(blk, lambda i, n: (i * nq + n, 0)),
                  pl.BlockSpec(blk, lambda i, n: (i * nq + jnp.minimum(n + 1, nq - 1), 0)),
                  pl.BlockSpec((None, p, LANES), lambda i, n: (i, 0, 0)),
                  pl.BlockSpec((None, p, LANES), lambda i, n: (i, 0, 0)),
                  _const_spec((N_HEADS * WIN_BLOCK, 1)),
                  _const_spec((GROUP, GROUP)), _const_spec((GROUP, GROUP))],
        out_specs=pl.BlockSpec(blk, lambda i, n: (i * nq + n, 0)),
        out_shape=_sds(wq.shape, BF16),
        compiler_params=_cparams("parallel", "parallel"),
        name="win_sample",
    )(wq, wkv, wkv, wkv, ctx_k, ctx_v, sink_col, consts["dupk"], consts["dupv"])


def _win_prompt_body(q_ref, kv_ref, sink_ref, dupk_ref, dupv_ref, o_ref):
    kv = kv_ref[...]
    k = _dot(kv, dupk_ref[...]).astype(BF16)
    v = _dot(kv, dupv_ref[...]).astype(BF16)
    q4 = _stack_heads(q_ref[...])
    o4 = _softmax_pv([(_dot_nt(q4, k), v)], sink_ref[...])
    o_ref[...] = _unstack_heads(o4, q_ref.shape[0]).astype(BF16)


def _win_prompt(wq, wkv, seq, sink_col, consts):
    b = wq.shape[0] // seq
    blk = (seq, GROUP)
    return pl.pallas_call(
        _win_prompt_body,
        grid=(b,),
        in_specs=[pl.BlockSpec(blk, lambda i: (i, 0)), pl.BlockSpec(blk, lambda i: (i, 0)),
                  _const_spec((N_HEADS * seq, 1)),
                  _const_spec((GROUP, GROUP)), _const_spec((GROUP, GROUP))],
        out_specs=pl.BlockSpec(blk, lambda i: (i, 0)),
        out_shape=_sds(wq.shape, BF16),
        compiler_params=_cparams("parallel"),
        name="win_prompt",
    )(wq, wkv, sink_col, consts["dupk"], consts["dupv"])


def _diff_body(ck, has_ctx, *refs):
    if has_ctx:
        q_ref, k_ref, vt_ref, ck_ref, cvt_ref, lam_ref, gain_ref, o_ref = refs
        sources = [(k_ref, vt_ref), (ck_ref, cvt_ref)]
    else:
        q_ref, k_ref, vt_ref, lam_ref, gain_ref, o_ref = refs
        sources = [(k_ref, vt_ref)]
    h = pl.program_id(2)
    tq = q_ref.shape[0]
    q = q_ref[...]
    sub = lax.broadcasted_iota(jnp.int32, (1, GROUP), 1) // DIFF_DIM
    zero = jnp.zeros_like(q)
    q2 = jnp.concatenate([jnp.where(sub == 2 * h, q, zero), jnp.where(sub == 2 * h + 1, q, zero)], axis=0)
    chunks = []
    for kr, vr in sources:
        n_keys = kr.shape[0]
        step = min(ck, n_keys)
        chunks += [(kr, vr, c0, step) for c0 in range(0, n_keys, step)]

    def scores(i):
        kr, _, c0, step = chunks[i]
        return _dot_nt(kr[c0:c0 + step, :], q2)

    pending = [scores(i) for i in range(min(DIFF_LOOKAHEAD, len(chunks)))]
    m = acc = None
    for i, (_, vr, c0, step) in enumerate(chunks):
        s = pending.pop(0)
        if i + DIFF_LOOKAHEAD < len(chunks):
            pending.append(scores(i + DIFF_LOOKAHEAD))
        vt = vr[:, c0:c0 + step]
        mc = s.max(axis=0, keepdims=True)
        if m is None:
            m = mc
            acc = _dot(vt, jnp.exp2(s - m).astype(BF16))
        else:
            m_new = jnp.maximum(m, mc)
            acc = jnp.exp2(m - m_new) * acc + _dot(vt, jnp.exp2(s - m_new).astype(BF16))
            m = m_new
    o2 = acc[:HEAD_DIM] / acc[HEAD_DIM:HEAD_DIM + 1]
    od = o2[:, :tq] - lam_ref[0, 0] * o2[:, tq:]
    y = od * lax.rsqrt(jnp.mean(od * od, axis=0, keepdims=True) + EPS) * gain_ref[...] * lam_ref[0, 1]
    o_ref[...] = y.astype(BF16)


def _diff_attention(dq, dk, dvt, ctx, seq, lam2, gain_col, tq=512, ck=512):
    rows = dq.shape[0]
    b = rows // seq
    tq = min(tq, seq)
    nq = seq // tq
    in_specs = [pl.BlockSpec((tq, GROUP), lambda i, n, h: (i * nq + n, 0)),
                pl.BlockSpec((seq, GROUP), lambda i, n, h: (i, 0)),
                pl.BlockSpec((None, DIFF_V_ROWS, seq), lambda i, n, h: (h, 0, i))]
    args = [dq, dk, dvt]
    if ctx is not None:
        p = ctx[0].shape[1]
        in_specs += [pl.BlockSpec((None, p, GROUP), lambda i, n, h: (i, 0, 0)),
                     pl.BlockSpec((None, None, DIFF_V_ROWS, p), lambda i, n, h: (i, h, 0, 0))]
        args += list(ctx)
    in_specs += [pl.BlockSpec(memory_space=pltpu.SMEM), pl.BlockSpec((HEAD_DIM, 1), lambda i, n, h: (h, 0))]
    args += [lam2, gain_col]
    return pl.pallas_call(
        functools.partial(_diff_body, ck, ctx is not None),
        grid=(b, nq, N_HEADS),
        in_specs=in_specs,
        out_specs=pl.BlockSpec((HEAD_DIM, tq), lambda i, n, h: (h, i * nq + n)),
        out_shape=_sds((GROUP, rows), BF16),
        compiler_params=_cparams("parallel", "parallel", "parallel"),
        name="diff_ctx" if ctx is not None else "diff",
    )(*args)


def _ret_operands(rope, rq, rk, cos, sin):
    q = rq
    k = rk * HEAD_DIM ** -0.5
    if rope:
        q = _rope(q, cos, sin, HEAD_DIM // 4)
        k = _rope(k, cos, sin, HEAD_DIM // 4)
    return q, k


def _hgrn_gate(z, lb):
    f = lb + (1.0 - lb) * jax.nn.sigmoid(z)
    return jnp.log(jnp.maximum(f, 1e-30)), (1.0 - lb) * jax.nn.sigmoid(-z)


def _row_index(rows):
    return lax.broadcasted_iota(jnp.int32, (rows, 1), 0)


def _segment_state_step(st_ref, k, v, cum, total, bd):
    ku = k * jnp.exp(total - cum)
    st_ref[...] = st_ref[...] * jnp.exp(total) + _dot_tn(v.astype(BF16), ku.astype(BF16)) * bd


def _states_body(gated, *refs):
    if gated:
        (zf_ref, vf_ref, zb_ref, vb_ref, lb_ref, tril_ref, triu_ref, s0f_ref, s0b_ref, bd_ref,
         ef_ref, eb_ref, stf, stb) = refs
    else:
        (kf_ref, vf_ref, kb_ref, vb_ref, cf_ref, sf_ref, cb_ref, sb_ref, lg_ref, s0f_ref, s0b_ref, bd_ref,
         ef_ref, eb_ref, stf, stb) = refs
    s = pl.program_id(1)

    @pl.when(s == 0)
    def _():
        stf[...] = s0f_ref[...]
        stb[...] = s0b_ref[...]

    ef_ref[...] = stf[...]
    eb_ref[...] = stb[...]
    bd = bd_ref[...]
    rows = vf_ref.shape[0]
    if gated:
        lff, kf = _hgrn_gate(zf_ref[...], lb_ref[0:1, :])
        lfb, kb = _hgrn_gate(zb_ref[...], lb_ref[1:2, :])
        cumf = _sum_rows(tril_ref[...], lff)
        cumb = _sum_rows(triu_ref[...], lfb)
        totf, totb = cumf[rows - 1:rows, :], cumb[0:1, :]
    else:
        _, kf = _ret_operands(True, kf_ref[...], kf_ref[...], cf_ref[...], sf_ref[...])
        _, kb = _ret_operands(True, kb_ref[...], kb_ref[...], cb_ref[...], sb_ref[...])
        i = _row_index(rows).astype(F32)
        lgf, lgb = lg_ref[0:1, :], lg_ref[1:2, :]
        cumf, totf = (i + 1.0) * lgf, rows * lgf
        cumb, totb = (rows - i) * lgb, rows * lgb
    _segment_state_step(stf, kf, vf_ref[...], cumf, totf, bd)
    _segment_state_step(stb, kb, vb_ref[...], cumb, totb, bd)


def _segment_states(gated, proj, n_b, seq, s0f, s0b, lane_rows, consts, tables):
    ns = seq // SEG
    blk = (SEG, GROUP)

    def fwd(cb):
        return pl.BlockSpec(blk, lambda i, s, cb=cb: (i * ns + s, cb))

    def bwd(cb):
        return pl.BlockSpec(blk, lambda i, s, cb=cb: (i * ns + ns - 1 - s, cb))

    st_spec = pl.BlockSpec((None, GROUP, GROUP), lambda i, s: (i, 0, 0))
    if gated:
        in_specs = [fwd(CB_HZF), fwd(CB_HI), bwd(CB_HZB), bwd(CB_HI), _const_spec((2, GROUP)),
                    _const_spec((SEG, SEG)), _const_spec((SEG, SEG))]
        args = [proj] * 4 + [lane_rows, consts["tril_seg"], consts["triu_seg"]]
    else:
        tf = pl.BlockSpec((SEG, LANES), lambda i, s: (s, 0))
        tb = pl.BlockSpec((SEG, LANES), lambda i, s: (ns - 1 - s, 0))
        in_specs = [fwd(CB_RK), fwd(CB_RV), bwd(CB_RK), bwd(CB_RV), tf, tf, tb, tb, _const_spec((2, GROUP))]
        args = [proj] * 4 + [tables[0], tables[1], tables[0], tables[1], lane_rows]
    in_specs += [st_spec, st_spec, _const_spec((GROUP, GROUP))]
    args += [s0f, s0b, consts["bd"]]
    e_shape = _sds((n_b, ns, GROUP, GROUP), F32)
    return pl.pallas_call(
        functools.partial(_states_body, gated),
        grid=(n_b, ns),
        in_specs=in_specs,
        out_specs=[pl.BlockSpec((None, None, GROUP, GROUP), lambda i, s: (i, s, 0, 0)),
                   pl.BlockSpec((None, None, GROUP, GROUP), lambda i, s: (i, ns - 1 - s, 0, 0))],
        out_shape=[e_shape, e_shape],
        scratch_shapes=[pltpu.VMEM((GROUP, GROUP), F32), pltpu.VMEM((GROUP, GROUP), F32)],
        compiler_params=_cparams("parallel", "arbitrary"),
        name="hgrn_states" if gated else "ret_states",
    )(*args)


def _ret_out_body(rope, has_state, emit_state, *refs):
    rq_ref, rk_ref, rv_ref, rg_ref = refs[:4]
    refs = refs[4:]
    cos = sin = None
    if rope:
        cos, sin = refs[0][...], refs[1][...]
        refs = refs[2:]
    lgs_ref, lg_ref, gain_ref, g64_ref = refs[:4]
    refs = refs[4:]
    if has_state:
        ef_ref, eb_ref = refs[:2]
        refs = refs[2:]
    o_ref = refs[0]
    rows = rq_ref.shape[0]
    q, k = _ret_operands(rope, rq_ref[...], rk_ref[...], cos, sin)
    v = rv_ref[...]
    kb16, vb16 = k.astype(BF16), v.astype(BF16)
    s4 = _dot_nt(_stack_heads(q.astype(BF16)), kb16)
    d = (lax.broadcasted_iota(jnp.int32, (rows, rows), 0) - lax.broadcasted_iota(jnp.int32, (rows, rows), 1)).astype(F32)
    slabs = []
    for h in range(N_HEADS):
        m = (jnp.where(d >= 0, jnp.exp(jnp.maximum(d, 0.0) * lgs_ref[0, h]), 0.0)
             + jnp.where(d <= 0, jnp.exp(jnp.maximum(-d, 0.0) * lgs_ref[1, h]), 0.0))
        slabs.append((s4[h * rows:(h + 1) * rows] * m).astype(BF16))
    o = _unstack_heads(_dot(jnp.concatenate(slabs, axis=0), vb16), rows)
    i = _row_index(rows).astype(F32)
    lgf, lgb = lg_ref[0:1, :], lg_ref[1:2, :]
    if has_state:
        o = o + _dot_nt((q * jnp.exp((i + 1.0) * lgf)).astype(BF16), ef_ref[...].astype(BF16))
        o = o + _dot_nt((q * jnp.exp((rows - i) * lgb)).astype(BF16), eb_ref[...].astype(BF16))
    y = _group_rms(o, g64_ref[...], HEAD_DIM, gain_ref[...]) * _silu(rg_ref[...])
    o_ref[...] = y.astype(BF16)
    if emit_state:
        sf_ref, sb_ref = refs[1:3]
        sf_ref[...] = _dot_tn(vb16, (k * jnp.exp((rows - 1.0 - i) * lgf)).astype(BF16))
        sb_ref[...] = _dot_tn(vb16, (k * jnp.exp(i * lgb)).astype(BF16))


def _ret_out(proj, n_b, seq, lg_smem, lg_rows, gain, consts, tables, states, emit_state):
    ns = seq // SEG
    rope = tables is not None
    blk = (SEG, GROUP)

    def col(cb):
        return pl.BlockSpec(blk, lambda i, s, cb=cb: (i * ns + s, cb))

    in_specs = [col(CB_RQ), col(CB_RK), col(CB_RV), col(CB_RG)]
    args = [proj] * 4
    if rope:
        in_specs += [pl.BlockSpec((SEG, LANES), lambda i, s: (s, 0))] * 2
        args += list(tables)
    in_specs += [pl.BlockSpec(memory_space=pltpu.SMEM), _const_spec((2, GROUP)), _const_spec((1, GROUP)),
                 _const_spec((GROUP, GROUP))]
    args += [lg_smem, lg_rows, gain, consts["g64"]]
    if states is not None:
        in_specs += [pl.BlockSpec((None, None, GROUP, GROUP), lambda i, s: (i, s, 0, 0))] * 2
        args += list(states)
    out_specs = [pl.BlockSpec(blk, lambda i, s: (i * ns + s, 0))]
    out_shape = [_sds((n_b * seq, GROUP), BF16)]
    if emit_state:
        assert ns == 1
        out_specs += [pl.BlockSpec((None, GROUP, GROUP), lambda i, s: (i, 0, 0))] * 2
        out_shape += [_sds((n_b, GROUP, GROUP), F32)] * 2
    return pl.pallas_call(
        functools.partial(_ret_out_body, rope, states is not None, emit_state),
        grid=(n_b, ns),
        in_specs=in_specs, out_specs=out_specs, out_shape=out_shape,
        compiler_params=_cparams("parallel", "parallel"),
        name="ret_out_ctx" if states is not None else "ret_out",
    )(*args)


PAIR_LEVELS_MATMUL = (2, 4, 8)
PAIR_LEVELS_CUMSUM = (16, 32, 64, 128, 256)
PAIR_FOLD = 64


def _pair_level_matrix(n, forward):
    mats = []
    j = np.arange(n)[None, :]
    t = np.arange(n)[:, None]
    for g in PAIR_LEVELS_MATMUL:
        half = g // 2
        pos = t % g
        if forward:
            mid = t - pos + half - 1
            m = np.where(pos >= half, (j > mid) & (j <= t), (j > t) & (j <= mid))
        else:
            mid = t - pos + half
            m = np.where(pos < half, (j >= t) & (j < mid), (j >= mid) & (j < t))
        mats.append(m)
    return jnp.asarray(np.concatenate(mats, axis=0), BF16)


def _pair_decays(lf, cs, small_mat, forward):
    rows = lf.shape[0]
    hi = lf.astype(BF16)
    lo = (lf - hi.astype(F32)).astype(BF16)
    d = _dot(small_mat, hi) + _dot(small_mat, lo)
    out = [jnp.exp(d[i * rows:(i + 1) * rows]) for i in range(len(PAIR_LEVELS_MATMUL))]
    for g in PAIR_LEVELS_CUMSUM:
        half = g // 2
        pieces = []
        for grp in range(rows // g):
            mid = grp * g + (half - 1 if forward else half)
            pieces.append(jnp.broadcast_to(cs[mid:mid + 1, :], (g, cs.shape[1])))
        diff = cs - (pieces[0] if len(pieces) == 1 else jnp.concatenate(pieces, axis=0))
        out.append(jnp.exp(jnp.minimum(diff, -diff)))
    return out


def _hgrn_pair_weights(q, k, lf, cs, small_mat, forward):
    rows = q.shape[0]
    nfold = rows // PAIR_FOLD
    levels = PAIR_LEVELS_MATMUL + PAIR_LEVELS_CUMSUM
    decays = _pair_decays(lf, cs, small_mat, forward)
    pos = _row_index(rows)
    t_loc = lax.broadcasted_iota(jnp.int32, (PAIR_FOLD, GROUP), 0)
    s_loc = lax.broadcasted_iota(jnp.int32, (PAIR_FOLD, GROUP), 1) % PAIR_FOLD
    r_big = lax.broadcasted_iota(jnp.int32, (N_HEADS * rows, rows), 0) % rows
    c_big = lax.broadcasted_iota(jnp.int32, (N_HEADS * rows, rows), 1)
    folded = [None] * nfold
    stacked = None
    for g, e in reversed(list(zip(levels, decays))):
        right = (pos % g) >= (g // 2)
        is_query = right if forward else jnp.logical_not(right)
        qe = q * e
        ke = k * e
        qs = jnp.where(is_query, qe, 0.0).astype(BF16)
        ks = jnp.where(is_query, 0.0, ke).astype(BF16)
        if g > PAIR_FOLD:
            s = _dot_nt(_stack_heads(qs), ks)
            if stacked is None:
                stacked = s
            else:
                stacked = jnp.where((r_big // g) == (c_big // g), s, stacked)
        else:
            same = None if g == PAIR_FOLD else (t_loc // g) == (s_loc // g)
            for j in range(nfold):
                r0, r1 = j * PAIR_FOLD, (j + 1) * PAIR_FOLD
                s = _dot_nt(qs[r0:r1], _stack_heads(ks[r0:r1]))
                folded[j] = s if same is None else jnp.where(same, s, folded[j])
    return folded, stacked


def _hgrn_out_body(has_state, emit_state, *refs):
    (q_ref, zf_ref, zb_ref, v_ref, g_ref, lb_ref, gain_ref, g64_ref, bd_ref,
     tril_ref, triu_ref, smf_ref, smb_ref) = refs[:13]
    refs = refs[13:]
    g64 = g64_ref[...]
    if has_state:
        ef, eb = refs[0][...], refs[1][...]
        refs = refs[2:]
    o_ref = refs[0]
    q, v = q_ref[...], v_ref[...]
    rows = q.shape[0]
    lff, kf = _hgrn_gate(zf_ref[...], lb_ref[0:1, :])
    lfb, kb = _hgrn_gate(zb_ref[...], lb_ref[1:2, :])
    csf = _sum_rows(tril_ref[...], lff)
    csb = _sum_rows(triu_ref[...], lfb)
    fold_f, stack_f = _hgrn_pair_weights(q, kf, lff, csf, smf_ref[...], True)
    fold_b, stack_b = _hgrn_pair_weights(q, kb, lfb, csb, smb_ref[...], False)
    vb = v.astype(BF16)
    o = _dot((q * (kf + kb)).astype(BF16), g64) * v
    o = o + _unstack_heads(_dot((stack_f + stack_b).astype(BF16), vb), rows)
    tiles = []
    for j in range(rows // PAIR_FOLD):
        r0, r1 = j * PAIR_FOLD, (j + 1) * PAIR_FOLD
        tiles.append(_dot((fold_f[j] + fold_b[j]).astype(BF16), _stack_heads(vb[r0:r1])))
    o = o + jnp.concatenate(tiles, axis=0)
    if has_state:
        o = o + _dot_nt((q * jnp.exp(csf)).astype(BF16), ef.astype(BF16))
        o = o + _dot_nt((q * jnp.exp(csb)).astype(BF16), eb.astype(BF16))
    y = _group_rms(o, g64, HEAD_DIM, gain_ref[...]) * _silu(g_ref[...])
    o_ref[...] = y.astype(BF16)
    if emit_state:
        totf, totb = csf[rows - 1:rows, :], csb[0:1, :]
        stf = _dot_tn(vb, (kf * jnp.exp(totf - csf)).astype(BF16))
        stb = _dot_tn(vb, (kb * jnp.exp(totb - csb)).astype(BF16))
        if has_state:
            bd = bd_ref[...]
            stf = ef * jnp.exp(totf) + stf * bd
            stb = eb * jnp.exp(totb) + stb * bd
        refs[1][...] = stf
        refs[2][...] = stb


def _hgrn_out(proj, n_b, seq, lb_rows, gain, consts, states, emit_state):
    ns = seq // SEG
    blk = (SEG, GROUP)

    def col(cb):
        return pl.BlockSpec(blk, lambda i, s, cb=cb: (i * ns + s, cb))

    n_small = len(PAIR_LEVELS_MATMUL) * SEG
    in_specs = [col(CB_HQ), col(CB_HZF), col(CB_HZB), col(CB_HI), col(CB_HG),
                _const_spec((2, GROUP)), _const_spec((1, GROUP)), _const_spec((GROUP, GROUP)),
                _const_spec((GROUP, GROUP)), _const_spec((SEG, SEG)), _const_spec((SEG, SEG)),
                _const_spec((n_small, SEG)), _const_spec((n_small, SEG))]
    args = [proj] * 5 + [lb_rows, gain, consts["g64"], consts["bd"],
                         consts["tril_seg"], consts["triu_seg"], consts["pair_f"], consts["pair_b"]]
    if states is not None:
        in_specs += [pl.BlockSpec((None, None, GROUP, GROUP), lambda i, s: (i, s, 0, 0))] * 2
        args += list(states)
    out_specs = [pl.BlockSpec(blk, lambda i, s: (i * ns + s, 0))]
    out_shape = [_sds((n_b * seq, GROUP), BF16)]
    if emit_state:
        assert ns == 1
        out_specs += [pl.BlockSpec((None, GROUP, GROUP), lambda i, s: (i, 0, 0))] * 2
        out_shape += [_sds((n_b, GROUP, GROUP), F32)] * 2
    return pl.pallas_call(
        functools.partial(_hgrn_out_body, states is not None, emit_state),
        grid=(n_b, ns),
        in_specs=in_specs, out_specs=out_specs, out_shape=out_shape,
        compiler_params=_cparams("parallel", "parallel"),
        name="hgrn_out_ctx" if states is not None else "hgrn_out",
    )(*args)


def _ffn_body(x_ref, oa_ref, ob_ref, oc_ref, od_ref, wo_ref, g1_ref, n2_ref, sc_ref, sh_ref, g2_ref,
              wg_ref, wu_ref, wd_ref, y_ref, x1_ref, h_ref, acc_ref):
    j = pl.program_id(1)

    @pl.when(j == 0)
    def _():
        mix = _dot(oa_ref[...], wo_ref[0:GROUP, :])
        mix += _dot(ob_ref[...], wo_ref[GROUP:2 * GROUP, :])
        mix += _dot_tn(oc_ref[...], wo_ref[2 * GROUP:3 * GROUP, :])
        mix += _dot(od_ref[...], wo_ref[3 * GROUP:4 * GROUP, :])
        x1 = x_ref[...] + g1_ref[...] * mix
        x1_ref[...] = x1
        h = x1 * lax.rsqrt(jnp.mean(x1 * x1, axis=-1, keepdims=True) + EPS) * n2_ref[...]
        h_ref[...] = (h * (1.0 + sc_ref[...]) + sh_ref[...]).astype(BF16)
        acc_ref[...] = jnp.zeros_like(acc_ref)

    h = h_ref[...]
    a = _silu(_dot(h, wg_ref[...])) * _dot(h, wu_ref[...])
    acc_ref[...] += _dot(a.astype(BF16), wd_ref[...])

    @pl.when(j == pl.num_programs(1) - 1)
    def _():
        y_ref[...] = x1_ref[...] + g2_ref[...] * acc_ref[...]


def _outproj_ffn(x, mixes, w_out, g1, n2, sc2, sh2, g2, w_in, w_dn, rows_per_mod, mod_row0, tm=1024, th=256):
    t, d = x.shape
    hid = w_dn.shape[0]
    nh = hid // th
    mod_map = _mod_row_map(tm, rows_per_mod, mod_row0)
    mod2 = lambda i, j: mod_map(i)
    row = lambda i, j: (i, 0)
    mix_spec = pl.BlockSpec((tm, GROUP), row)
    mod_spec = pl.BlockSpec((None, 1, d), mod2)
    return pl.pallas_call(
        _ffn_body,
        grid=(t // tm, nh),
        in_specs=[pl.BlockSpec((tm, d), row), mix_spec, mix_spec, pl.BlockSpec((GROUP, tm), lambda i, j: (0, i)), mix_spec]
                 + [_const_spec((d, d)), mod_spec, _const_spec((1, d)), mod_spec, mod_spec, mod_spec,
                    pl.BlockSpec((d, th), lambda i, j: (0, j)),
                    pl.BlockSpec((d, th), lambda i, j: (0, nh + j)),
                    pl.BlockSpec((th, d), lambda i, j: (j, 0))],
        out_specs=pl.BlockSpec((tm, d), row),
        out_shape=_sds((t, d), F32),
        scratch_shapes=[pltpu.VMEM((tm, d), F32), pltpu.VMEM((tm, d), BF16), pltpu.VMEM((tm, d), F32)],
        compiler_params=_cparams("parallel", "arbitrary"),
        name="outproj_ffn",
    )(x, *mixes, w_out, g1, n2, sc2, sh2, g2, w_in, w_in, w_dn)


def _lane_rows(per_head):
    return jnp.repeat(per_head.astype(F32), HEAD_DIM, axis=1)


def _states_to_lanes(s):
    b = s.shape[0]
    eye = jnp.eye(N_HEADS, dtype=F32)
    st = jnp.swapaxes(s.astype(F32), -1, -2)[:, :, :, :, None, :] * eye[None, None, :, None, :, None]
    st = st.reshape(b, 2, GROUP, GROUP)
    return st[:, 0], st[:, 1]


def _lanes_to_states(sf, sb):
    def diag(x):
        b = x.shape[0]
        x = x.reshape(b, N_HEADS, HEAD_DIM, N_HEADS, HEAD_DIM)
        return jnp.stack([x[:, h, :, h, :] for h in range(N_HEADS)], axis=1).swapaxes(-1, -2)
    return jnp.stack([diag(sf), diag(sb)], axis=1)


def kernel(x_prompt, x_sample, state_ret, cache_win_k, cache_win_v, cache_diff_k, cache_diff_v, state_hgrn, c, c_ctx, norm1_g, norm2_g, w_ada, b_ada, w_in, ret_decay, ret_norm_g, win_q_norm, win_k_norm, win_sink, diff_q_norm, diff_k_norm, diff_lambda, diff_norm_g, hgrn_lb_logits, hgrn_norm_g, w_out, w_ffn_in, w_ffn_out):
    n_p, seq_p, d = x_prompt.shape
    n_s, seq_s, _ = x_sample.shape
    depth = w_in.shape[0]
    t_s, t_p = n_s * seq_s, n_p * seq_p
    assert seq_p == SEG and seq_s % SEG == 0 and d == N_HEADS * GROUP

    tril_seg, triu_seg = _cumsum_matrices(SEG)
    dupk, dupv = _kv_dup_matrices()
    consts = dict(g64=_group_matrix(GROUP, HEAD_DIM), g32=_group_matrix(GROUP, DIFF_DIM),
                  bd=_group_matrix(GROUP, HEAD_DIM).astype(F32), dupk=dupk, dupv=dupv,
                  tril_seg=tril_seg, triu_seg=triu_seg,
                  pair_f=_pair_level_matrix(SEG, True), pair_b=_pair_level_matrix(SEG, False))
    tab64 = _rope_tables(seq_s, HEAD_DIM)
    tab32 = _rope_tables(seq_s, DIFF_DIM)

    n_rows = -(-(n_s + 1) // 8) * 8
    cond = jnp.zeros((n_rows, d), F32).at[:n_s].set(c).at[n_s].set(c_ctx)
    mod = _adaln(cond, w_ada, b_ada).reshape(depth, n_rows, 6, 1, d)

    lb_p = jax.nn.softmax(hgrn_lb_logits.astype(F32), axis=0)
    lb_all = jnp.cumsum(lb_p, axis=0) - lb_p
    log_gamma = -jnp.exp(ret_decay.astype(F32))

    xs = x_sample.reshape(t_s, d)
    xp = x_prompt.reshape(t_p, d)
    new_ret, new_wk, new_wv, new_dk, new_dv, new_hg = [], [], [], [], [], []
    for l in range(depth):
        lam_init = 0.8 - 0.6 * math.exp(-0.3 * l)
        sh1, sc1, g1, sh2, sc2, g2 = [mod[l, :, i] for i in range(6)]
        w_in_l, w_out_l = w_in[l].astype(BF16), w_out[l].astype(BF16)
        w_up_l, w_dn_l = w_ffn_in[l].astype(BF16), w_ffn_out[l].astype(BF16)
        n1, n2 = norm1_g[l][None], norm2_g[l][None]
        gains = (jnp.tile(win_q_norm[l], N_HEADS)[None], jnp.tile(win_k_norm[l], KV_HEADS)[None],
                 jnp.tile(diff_q_norm[l], 2 * N_HEADS)[None], jnp.tile(diff_k_norm[l], 2 * N_HEADS)[None])
        sink = win_sink[l].astype(F32)
        lq1, lk1, lq2, lk2 = diff_lambda[l].astype(F32)
        lam = jnp.exp(jnp.sum(lq1 * lk1)) - jnp.exp(jnp.sum(lq2 * lk2)) + lam_init
        lam2 = jnp.stack([lam, jnp.asarray(1.0 - lam_init, F32)]).reshape(1, 2)
        dgain = diff_norm_g[l].astype(F32)[:, None]
        lg = log_gamma[l]
        lg_rows = _lane_rows(lg)
        rgain = ret_norm_g[l][None]
        lb_rows = lb_all[l]
        hgain = hgrn_norm_g[l][None]

        proj = _inproj(xs, n1, sc1, sh1, w_in_l, seq_s, 0)
        wq, wkv, dq, dk, dvt = _prep(proj, gains, consts, tab64 + tab32, False)
        ctx_wk = cache_win_k[:, l].transpose(0, 2, 1, 3).reshape(n_s, -1, LANES).astype(BF16)
        ctx_wv = cache_win_v[:, l].transpose(0, 2, 1, 3).reshape(n_s, -1, LANES).astype(BF16)
        o_win = _win_sample(wq, wkv, ctx_wk, ctx_wv, jnp.repeat(sink, WIN_BLOCK)[:, None], consts)
        ctx_dk = cache_diff_k[:, l].transpose(0, 3, 1, 2, 4).reshape(n_s, -1, GROUP).astype(BF16)
        ctx_dvt = cache_diff_v[:, l].swapaxes(-1, -2).astype(BF16)
        ctx_dvt = jnp.concatenate([ctx_dvt, jnp.ones(ctx_dvt.shape[:2] + (DIFF_V_ROWS - HEAD_DIM, ctx_dvt.shape[3]),
                                                     BF16)], axis=2)
        o_diff = _diff_attention(dq, dk, dvt, (ctx_dk, ctx_dvt), seq_s, lam2, dgain)
        s0f, s0b = _states_to_lanes(state_ret[:, l])
        ret_e = _segment_states(False, proj, n_s, seq_s, s0f, s0b, lg_rows, consts, tab64)
        (o_ret,) = _ret_out(proj, n_s, seq_s, lg, lg_rows, rgain, consts, tab64, ret_e, False)
        h0f, h0b = _states_to_lanes(state_hgrn[:, l])
        hg_e = _segment_states(True, proj, n_s, seq_s, h0f, h0b, lb_rows, consts, None)
        (o_h,) = _hgrn_out(proj, n_s, seq_s, lb_rows, hgain, consts, hg_e, False)
        xs = _outproj_ffn(xs, (o_ret, o_win, o_diff, o_h), w_out_l, g1, n2, sc2, sh2, g2, w_up_l, w_dn_l, seq_s, 0)

        proj = _inproj(xp, n1, sc1, sh1, w_in_l, t_p, n_s)
        wq, wkv, dq, dk, dvt, wk_f, dk_f = _prep(proj, gains, consts, None, True)
        o_win = _win_prompt(wq, wkv, seq_p, jnp.repeat(sink, seq_p)[:, None], consts)
        o_diff = _diff_attention(dq, dk, dvt, None, seq_p, lam2, dgain)
        o_ret, rsf, rsb = _ret_out(proj, n_p, seq_p, lg, lg_rows, rgain, consts, None, None, True)
        o_h, hsf, hsb = _hgrn_out(proj, n_p, seq_p, lb_rows, hgain, consts, None, True)
        xp = _outproj_ffn(xp, (o_ret, o_win, o_diff, o_h), w_out_l, g1, n2, sc2, sh2, g2, w_up_l, w_dn_l, t_p, n_s)

        new_ret.append(_lanes_to_states(rsf, rsb))
        new_hg.append(_lanes_to_states(hsf, hsb))
        new_wk.append(wk_f.reshape(n_p, seq_p, KV_HEADS, HEAD_DIM).transpose(0, 2, 1, 3))
        new_wv.append(proj[:, CB_WKV * GROUP + LANES:(CB_WKV + 1) * GROUP]
                      .reshape(n_p, seq_p, KV_HEADS, HEAD_DIM).transpose(0, 2, 1, 3))
        new_dk.append(dk_f.reshape(n_p, seq_p, N_HEADS, 2, DIFF_DIM).transpose(0, 2, 3, 1, 4))
        new_dv.append(proj[:, CB_DV * GROUP:(CB_DV + 1) * GROUP]
                      .reshape(n_p, seq_p, N_HEADS, HEAD_DIM).transpose(0, 2, 1, 3))

    return (xp.reshape(n_p, seq_p, d), xs.reshape(n_s, seq_s, d),
            jnp.stack(new_ret, axis=1), jnp.stack(new_wk, axis=1), jnp.stack(new_wv, axis=1),
            jnp.stack(new_dk, axis=1), jnp.stack(new_dv, axis=1), jnp.stack(new_hg, axis=1))
```

```python
import functools
import math

import numpy as np
import jax
import jax.numpy as jnp
from jax import lax
from jax.experimental import pallas as pl
from jax.experimental.pallas import tpu as pltpu

F32 = jnp.float32
BF16 = jnp.bfloat16

GROUP = 256
HEAD_DIM = 64
N_HEADS = 4
KV_HEADS = 2
DIFF_DIM = 32
WINDOW = 128
WIN_BLOCK = 128
GRID_W = 64
ROPE_BASE = 10000.0
EPS = 1e-6
MASK_VALUE = -1e30
SEG = 256
LANES = 128
BF16_SUBLANES = 16
LOG2_E = math.log2(math.e)
V_ROWS = HEAD_DIM + BF16_SUBLANES
DIFF_LOOKAHEAD = 2
V7X_VMEM_LIMIT_BYTES = 56 * 1024 * 1024

(CB_RQ, CB_RK, CB_RV, CB_RG, CB_WQ, CB_WKV, CB_DQ, CB_DK, CB_DV,
 CB_HQ, CB_HZF, CB_HZB, CB_HI, CB_HG) = range(14)

_NT = (((1,), (1,)), ((), ()))
_TN = (((0,), (0,)), ((), ()))


def _sds(shape, dtype):
    return jax.ShapeDtypeStruct(shape, dtype)


def _cparams(*sem):
    return pltpu.CompilerParams(dimension_semantics=sem, vmem_limit_bytes=V7X_VMEM_LIMIT_BYTES)


def _const_spec(shape):
    return pl.BlockSpec(shape, lambda *_: (0,) * len(shape))


def _dot(a, b):
    return jnp.dot(a, b, preferred_element_type=F32)


def _dot_nt(a, b):
    return lax.dot_general(a, b, _NT, preferred_element_type=F32)


def _dot_tn(a, b):
    return lax.dot_general(a, b, _TN, preferred_element_type=F32)


def _silu(x):
    return x * jax.nn.sigmoid(x)


def _group_matrix(width, gsize):
    i = np.arange(width)
    return jnp.asarray((i[:, None] // gsize) == (i[None, :] // gsize), BF16)


def _cumsum_matrices(n):
    r, c = np.arange(n)[:, None], np.arange(n)[None, :]
    return jnp.asarray(c <= r, BF16), jnp.asarray(c >= r, BF16)


def _kv_dup_matrix():
    d = np.zeros((KV_HEADS * HEAD_DIM, GROUP), np.float32)
    for h in range(N_HEADS):
        kv = h // (N_HEADS // KV_HEADS)
        for j in range(HEAD_DIM):
            d[kv * HEAD_DIM + j, h * HEAD_DIM + j] = 1.0
    return jnp.asarray(d, BF16)


def _rope_tables(n_tokens, head_dim):
    d = head_dim // 2
    half = d // 2
    inv = ROPE_BASE ** (-jnp.arange(half, dtype=F32) / half)
    t = jnp.arange(n_tokens)
    row = (t // GRID_W).astype(F32)
    col = (t % GRID_W).astype(F32)
    j = np.arange(LANES) % head_dim
    w = j % d
    use_row = jnp.asarray((j // d) == 0)
    pos = jnp.where(use_row[None, :], row[:, None], col[:, None])
    ang = pos * inv[w % half][None, :]
    sign = jnp.asarray(np.where(w >= half, 1.0, -1.0), F32)
    return jnp.cos(ang), jnp.sin(ang) * sign[None, :]


def _group_rms(x, g_mat, gsize, gain):
    ss = _dot((x * x).astype(BF16), g_mat)
    return x * lax.rsqrt(ss * (1.0 / gsize) + EPS) * gain


def _rope(y, cos, sin, half):
    lane = lax.broadcasted_iota(jnp.int32, (1, LANES), 1)
    second = (lane % (2 * half)) >= half
    outs = []
    for p in range(y.shape[1] // LANES):
        z = y[:, p * LANES:(p + 1) * LANES]
        partner = jnp.where(second, pltpu.roll(z, half, 1), pltpu.roll(z, LANES - half, 1))
        outs.append(z * cos + partner * sin)
    return outs[0] if len(outs) == 1 else jnp.concatenate(outs, axis=1)


def _head_of_lane(width=GROUP):
    return lax.broadcasted_iota(jnp.int32, (1, width), 1) // HEAD_DIM


def _stack_heads(q, n_heads=N_HEADS):
    head = _head_of_lane()
    zero = jnp.zeros_like(q)
    return jnp.concatenate([jnp.where(head == h, q, zero) for h in range(n_heads)], axis=0)


def _unstack_heads(o4, rows):
    head = _head_of_lane()
    out = jnp.where(head == 0, o4[:rows], 0.0)
    for h in range(1, N_HEADS):
        out = out + jnp.where(head == h, o4[h * rows:(h + 1) * rows], 0.0)
    return out


def _split3(x):
    hi = x.astype(BF16)
    r = x - hi.astype(F32)
    mid = r.astype(BF16)
    lo = (r - mid.astype(F32)).astype(BF16)
    return hi, mid, lo


def _sum_rows(m01, x):
    hi, mid, lo = _split3(x)
    return _dot(m01, hi) + _dot(m01, mid) + _dot(m01, lo)


def _adaln_body(c_ref, w_ref, b_ref, o_ref):
    c = c_ref[...]
    o_ref[...] = jnp.dot(_silu(c), w_ref[...], preferred_element_type=F32,
                         precision=lax.Precision.HIGHEST) + b_ref[...]


def _adaln(cond, w_ada, b_ada):
    depth, d, n = w_ada.shape
    rows = cond.shape[0]
    tn = n // 4
    return pl.pallas_call(
        _adaln_body,
        grid=(depth, n // tn),
        in_specs=[pl.BlockSpec((rows, d), lambda l, j: (0, 0)),
                  pl.BlockSpec((None, d, tn), lambda l, j: (l, 0, j)),
                  pl.BlockSpec((None, 1, tn), lambda l, j: (l, 0, j))],
        out_specs=pl.BlockSpec((None, rows, tn), lambda l, j: (l, 0, j)),
        out_shape=_sds((depth, rows, n), F32),
        compiler_params=_cparams("parallel", "parallel"),
        name="adaln",
    )(cond, w_ada, b_ada.reshape(depth, 1, n))


def _inproj_body(x_ref, g_ref, sc_ref, sh_ref, w_ref, o_ref):
    x = x_ref[...]
    h = x * lax.rsqrt(jnp.mean(x * x, axis=-1, keepdims=True) + EPS) * g_ref[...]
    h = h * (1.0 + sc_ref[...]) + sh_ref[...]
    o_ref[...] = _dot(h.astype(BF16), w_ref[...])


def _mod_row_map(tm, rows_per_mod, mod_row0):
    return lambda i: (mod_row0 + (i * tm) // rows_per_mod, 0, 0)


def _inproj(x, g, sc, sh, w, rows_per_mod, mod_row0, tm=512):
    t, d = x.shape
    n = w.shape[1]
    mod_map = _mod_row_map(tm, rows_per_mod, mod_row0)
    return pl.pallas_call(
        _inproj_body,
        grid=(t // tm,),
        in_specs=[pl.BlockSpec((tm, d), lambda i: (i, 0)),
                  _const_spec((1, d)),
                  pl.BlockSpec((None, 1, d), mod_map),
                  pl.BlockSpec((None, 1, d), mod_map),
                  _const_spec((d, n))],
        out_specs=pl.BlockSpec((tm, n), lambda i: (i, 0)),
        out_shape=_sds((t, n), F32),
        compiler_params=_cparams("parallel"),
        name="inproj",
    )(x, g, sc, sh, w)


def _prep_body(rope, emit_f32, *refs):
    (wq_ref, wkv_ref, dq_ref, dk_ref, dv_ref, qn_ref, kn_ref, dqn_ref, dkn_ref, g64_ref, g32_ref,
     dup_ref) = refs[:12]
    refs = refs[12:]
    if rope:
        c64_ref, s64_ref, c32_ref, s32_ref = refs[:4]
        refs = refs[4:]
    wq_o, wk_o, wvt_o, dq_o, dk_o, dvt_o = refs[:6]
    g64 = g64_ref[...]
    g32 = g32_ref[...]
    tm = dv_ref.shape[0]

    wq = _group_rms(wq_ref[...], g64, HEAD_DIM, qn_ref[...])
    wkv = wkv_ref[...]
    wk = _group_rms(wkv[:, :LANES], g64[:LANES, :LANES], HEAD_DIM, kn_ref[...])
    dq = _group_rms(dq_ref[...], g32, DIFF_DIM, dqn_ref[...])
    dk = _group_rms(dk_ref[...], g32, DIFF_DIM, dkn_ref[...])
    if rope:
        c64, s64, c32, s32 = c64_ref[...], s64_ref[...], c32_ref[...], s32_ref[...]
        wq = _rope(wq, c64, s64, HEAD_DIM // 4)
        wk = _rope(wk, c64, s64, HEAD_DIM // 4)
        dq = _rope(dq, c32, s32, DIFF_DIM // 4)
        dk = _rope(dk, c32, s32, DIFF_DIM // 4)
    wq_o[...] = (wq * (HEAD_DIM ** -0.5 * LOG2_E)).astype(BF16)
    wk_o[...] = _dot(wk.astype(BF16), dup_ref[...]).astype(BF16)
    wvt_o[:, :HEAD_DIM, :] = wkv[:, LANES:].T.reshape(KV_HEADS, HEAD_DIM, tm).astype(BF16)
    wvt_o[:, HEAD_DIM:, :] = jnp.ones((KV_HEADS, V_ROWS - HEAD_DIM, tm), BF16)
    dq_o[...] = (dq * (DIFF_DIM ** -0.5 * LOG2_E)).astype(BF16)
    dk_o[...] = dk.astype(BF16)
    dvt_o[:, :HEAD_DIM, :] = dv_ref[...].T.reshape(N_HEADS, HEAD_DIM, tm).astype(BF16)
    dvt_o[:, HEAD_DIM:, :] = jnp.ones((N_HEADS, V_ROWS - HEAD_DIM, tm), BF16)
    if emit_f32:
        wk_f, dk_f = refs[6:8]
        wk_f[...] = wk
        dk_f[...] = dk


def _prep(proj, gains, consts, tables, emit_f32, tm=256):
    rope = tables is not None
    rows = proj.shape[0]

    def col(cb):
        return pl.BlockSpec((tm, GROUP), lambda i, cb=cb: (i, cb))

    def vt(heads):
        return pl.BlockSpec((heads, V_ROWS, tm), lambda i: (0, 0, i))

    in_specs = [col(CB_WQ), col(CB_WKV), col(CB_DQ), col(CB_DK), col(CB_DV),
                _const_spec((1, GROUP)), _const_spec((1, LANES)), _const_spec((1, GROUP)), _const_spec((1, GROUP)),
                _const_spec((GROUP, GROUP)), _const_spec((GROUP, GROUP)), _const_spec((LANES, GROUP))]
    args = [proj] * 5 + list(gains) + [consts["g64"], consts["g32"], consts["dupk"]]
    if rope:
        seq = tables[0].shape[0]
        nper = seq // tm
        in_specs += [pl.BlockSpec((tm, LANES), lambda i: (i % nper, 0))] * 4
        args += list(tables)
    out = pl.BlockSpec((tm, GROUP), lambda i: (i, 0))
    tok = _sds((rows, GROUP), BF16)
    out_specs = [out, out, vt(KV_HEADS), out, out, vt(N_HEADS)]
    out_shape = [tok, tok, _sds((KV_HEADS, V_ROWS, rows), BF16), tok, tok, _sds((N_HEADS, V_ROWS, rows), BF16)]
    if emit_f32:
        out_specs += [pl.BlockSpec((tm, LANES), lambda i: (i, 0)), out]
        out_shape += [_sds((rows, LANES), F32), _sds((rows, GROUP), F32)]
    return pl.pallas_call(
        functools.partial(_prep_body, rope, emit_f32),
        grid=(rows // tm,),
        in_specs=in_specs, out_specs=out_specs, out_shape=out_shape,
        compiler_params=_cparams("parallel"),
        name="prep_rope" if rope else "prep",
    )(*args)


def _win_body(banded, *refs):
    if banded:
        (q_ref, kp_ref, kc_ref, kn_ref, vp_ref, vc_ref, vn_ref, ck_ref, cvt_ref, sink_ref, o_ref) = refs
        k_refs, v_refs = (kp_ref, kc_ref, kn_ref), (vp_ref, vc_ref, vn_ref)
    else:
        q_ref, kc_ref, vc_ref, sink_ref, o_ref = refs
        k_refs, v_refs = (kc_ref,), (vc_ref,)
    tq = q_ref.shape[0]
    q4 = _stack_heads(q_ref[...])
    k_loc = k_refs[0][...] if len(k_refs) == 1 else jnp.concatenate([r[...] for r in k_refs], axis=0)
    s_loc = _dot_nt(k_loc, q4)
    sink = sink_ref[...]
    if banded:
        n = pl.program_id(1)
        seq = pl.num_programs(1) * WIN_BLOCK
        t = n * WIN_BLOCK + lax.broadcasted_iota(jnp.int32, s_loc.shape, 1) % WIN_BLOCK
        kpos = (n - 1) * WIN_BLOCK + lax.broadcasted_iota(jnp.int32, s_loc.shape, 0)
        valid = (kpos >= 0) & (kpos < seq) & (jnp.abs(t - kpos) <= WINDOW)
        s_loc = jnp.where(valid, s_loc, MASK_VALUE)
        s_ctx = _dot_nt(ck_ref[...], q4)
        m = jnp.maximum(jnp.maximum(s_loc.max(axis=0, keepdims=True), s_ctx.max(axis=0, keepdims=True)), sink)
        p_ctx = jnp.exp2(s_ctx - m).astype(BF16)
    else:
        m = jnp.maximum(s_loc.max(axis=0, keepdims=True), sink)
    p_loc = jnp.exp2(s_loc - m).astype(BF16)
    p_sink = jnp.exp2(sink - m)
    outs = []
    for h in range(N_HEADS):
        kv = h // (N_HEADS // KV_HEADS)
        c0, c1 = h * tq, (h + 1) * tq
        vt = v_refs[0][kv] if len(v_refs) == 1 else jnp.concatenate([r[kv] for r in v_refs], axis=1)
        acc = _dot(vt, p_loc[:, c0:c1])
        if banded:
            acc = acc + _dot(cvt_ref[kv], p_ctx[:, c0:c1])
        outs.append(acc[:HEAD_DIM] / (acc[HEAD_DIM:HEAD_DIM + 1] + p_sink[:, c0:c1]))
    o_ref[...] = jnp.concatenate(outs, axis=0).astype(BF16)


def _win_sample(wq, wk, wvt, ctx_k, ctx_vt, sink_row):
    b, p, _ = ctx_k.shape
    rows = wq.shape[0]
    nq = rows // b // WIN_BLOCK
    blk = (WIN_BLOCK, GROUP)
    vblk = (KV_HEADS, V_ROWS, WIN_BLOCK)
    prev = lambda i, n: i * nq + jnp.maximum(n - 1, 0)
    cur = lambda i, n: i * nq + n
    nxt = lambda i, n: i * nq + jnp.minimum(n + 1, nq - 1)
    return pl.pallas_call(
        functools.partial(_win_body, True),
        grid=(b, nq),
        in_specs=[pl.BlockSpec(blk, lambda i, n: (cur(i, n), 0)),
                  pl.BlockSpec(blk, lambda i, n: (prev(i, n), 0)),
                  pl.BlockSpec(blk, lambda i, n: (cur(i, n), 0)),
                  pl.BlockSpec(blk, lambda i, n: (nxt(i, n), 0)),
                  pl.BlockSpec(vblk, lambda i, n: (0, 0, prev(i, n))),
                  pl.BlockSpec(vblk, lambda i, n: (0, 0, cur(i, n))),
                  pl.BlockSpec(vblk, lambda i, n: (0, 0, nxt(i, n))),
                  pl.BlockSpec((None, p, GROUP), lambda i, n: (i, 0, 0)),
                  pl.BlockSpec((None, KV_HEADS, V_ROWS, p), lambda i, n: (i, 0, 0, 0)),
                  _const_spec((1, N_HEADS * WIN_BLOCK))],
        out_specs=pl.BlockSpec((GROUP, WIN_BLOCK), lambda i, n: (0, cur(i, n))),
        out_shape=_sds((GROUP, rows), BF16),
        compiler_params=_cparams("parallel", "parallel"),
        name="win_sample",
    )(wq, wk, wk, wk, wvt, wvt, wvt, ctx_k, ctx_vt, sink_row)


def _win_prompt(wq, wk, wvt, seq, sink_row):
    rows = wq.shape[0]
    blk = (seq, GROUP)
    return pl.pallas_call(
        functools.partial(_win_body, False),
        grid=(rows // seq,),
        in_specs=[pl.BlockSpec(blk, lambda i: (i, 0)), pl.BlockSpec(blk, lambda i: (i, 0)),
                  pl.BlockSpec((KV_HEADS, V_ROWS, seq), lambda i: (0, 0, i)),
                  _const_spec((1, N_HEADS * seq))],
        out_specs=pl.BlockSpec((GROUP, seq), lambda i: (0, i)),
        out_shape=_sds((GROUP, rows), BF16),
        compiler_params=_cparams("parallel"),
        name="win_prompt",
    )(wq, wk, wvt, sink_row)


def _diff_body(ck, has_ctx, heads, *refs):
    if has_ctx:
        q_ref, k_ref, vt_ref, ck_ref, cvt_ref, lam_ref, gain_ref, o_ref = refs
    else:
        q_ref, k_ref, vt_ref, lam_ref, gain_ref, o_ref = refs
    for j in range(heads):
        sources = [(k_ref, vt_ref.at[j])] + ([(ck_ref, cvt_ref.at[j])] if has_ctx else [])
        rows = slice(j * HEAD_DIM, (j + 1) * HEAD_DIM)
        o_ref[rows, :] = _diff_head(ck, pl.program_id(2) * heads + j, q_ref[...], sources,
                                    lam_ref, gain_ref[rows, :])


def _diff_head(ck, h, q, sources, lam_ref, gain):
    tq = q.shape[0]
    sub = lax.broadcasted_iota(jnp.int32, (1, GROUP), 1) // DIFF_DIM
    zero = jnp.zeros_like(q)
    q2 = jnp.concatenate([jnp.where(sub == 2 * h, q, zero), jnp.where(sub == 2 * h + 1, q, zero)], axis=0)
    chunks = []
    for kr, vr in sources:
        n_keys = kr.shape[0]
        step = min(ck, n_keys)
        chunks += [(kr, vr, c0, step) for c0 in range(0, n_keys, step)]

    def scores(i):
        kr, _, c0, step = chunks[i]
        return _dot_nt(kr[c0:c0 + step, :], q2)

    pending = [scores(i) for i in range(min(DIFF_LOOKAHEAD, len(chunks)))]
    m = acc = None
    for i, (_, vr, c0, step) in enumerate(chunks):
        s = pending.pop(0)
        if i + DIFF_LOOKAHEAD < len(chunks):
            pending.append(scores(i + DIFF_LOOKAHEAD))
        vt = vr[:, c0:c0 + step]
        mc = s.max(axis=0, keepdims=True)
        if m is None:
            m = mc
            acc = _dot(vt, jnp.exp2(s - m).astype(BF16))
        else:
            m_new = jnp.maximum(m, mc)
            acc = jnp.exp2(m - m_new) * acc + _dot(vt, jnp.exp2(s - m_new).astype(BF16))
            m = m_new
    o2 = acc[:HEAD_DIM] / acc[HEAD_DIM:HEAD_DIM + 1]
    od = o2[:, :tq] - lam_ref[0, 0] * o2[:, tq:]
    y = od * lax.rsqrt(jnp.mean(od * od, axis=0, keepdims=True) + EPS) * gain * lam_ref[0, 1]
    return y.astype(BF16)


def _diff_attention(dq, dk, dvt, ctx, seq, lam2, gain_col, heads, tq=512, ck=512):
    rows = dq.shape[0]
    b = rows // seq
    tq = min(tq, seq)
    nq = seq // tq
    in_specs = [pl.BlockSpec((tq, GROUP), lambda i, n, h: (i * nq + n, 0)),
                pl.BlockSpec((seq, GROUP), lambda i, n, h: (i, 0)),
                pl.BlockSpec((heads, V_ROWS, seq), lambda i, n, h: (h, 0, i))]
    args = [dq, dk, dvt]
    if ctx is not None:
        p = ctx[0].shape[1]
        in_specs += [pl.BlockSpec((None, p, GROUP), lambda i, n, h: (i, 0, 0)),
                     pl.BlockSpec((None, heads, V_ROWS, p), lambda i, n, h: (i, h, 0, 0))]
        args += list(ctx)
    in_specs += [pl.BlockSpec(memory_space=pltpu.SMEM),
                 pl.BlockSpec((heads * HEAD_DIM, 1), lambda i, n, h: (h, 0))]
    args += [lam2, gain_col]
    return pl.pallas_call(
        functools.partial(_diff_body, ck, ctx is not None, heads),
        grid=(b, nq, N_HEADS // heads),
        in_specs=in_specs,
        out_specs=pl.BlockSpec((heads * HEAD_DIM, tq), lambda i, n, h: (h, i * nq + n)),
        out_shape=_sds((GROUP, rows), BF16),
        compiler_params=_cparams("parallel", "parallel", "parallel"),
        name="diff_ctx" if ctx is not None else "diff",
    )(*args)


def _ret_operands(rope, rq, rk, cos, sin):
    q = rq
    k = rk * HEAD_DIM ** -0.5
    if rope:
        q = _rope(q, cos, sin, HEAD_DIM // 4)
        k = _rope(k, cos, sin, HEAD_DIM // 4)
    return q, k


def _hgrn_gate(z, lb):
    f = lb + (1.0 - lb) * jax.nn.sigmoid(z)
    return jnp.log(jnp.maximum(f, 1e-30)), (1.0 - lb) * jax.nn.sigmoid(-z)


def _row_index(rows):
    return lax.broadcasted_iota(jnp.int32, (rows, 1), 0)


def _segment_state_step(st_ref, k, v, cum, total, bd):
    ku = k * jnp.exp(total - cum)
    st_ref[...] = st_ref[...] * jnp.exp(total) + _dot_tn(v.astype(BF16), ku.astype(BF16)) * bd


def _states_body(gated, *refs):
    if gated:
        (zf_ref, vf_ref, zb_ref, vb_ref, lb_ref, tril_ref, triu_ref, s0f_ref, s0b_ref, bd_ref,
         ef_ref, eb_ref, stf, stb) = refs
    else:
        (kf_ref, vf_ref, kb_ref, vb_ref, cf_ref, sf_ref, cb_ref, sb_ref, lg_ref, s0f_ref, s0b_ref, bd_ref,
         ef_ref, eb_ref, stf, stb) = refs
    s = pl.program_id(1)

    @pl.when(s == 0)
    def _():
        stf[...] = s0f_ref[...]
        stb[...] = s0b_ref[...]

    ef_ref[...] = stf[...]
    eb_ref[...] = stb[...]
    bd = bd_ref[...]
    rows = vf_ref.shape[0]
    if gated:
        lff, kf = _hgrn_gate(zf_ref[...], lb_ref[0:1, :])
        lfb, kb = _hgrn_gate(zb_ref[...], lb_ref[1:2, :])
        cumf = _sum_rows(tril_ref[...], lff)
        cumb = _sum_rows(triu_ref[...], lfb)
        totf, totb = cumf[rows - 1:rows, :], cumb[0:1, :]
    else:
        _, kf = _ret_operands(True, kf_ref[...], kf_ref[...], cf_ref[...], sf_ref[...])
        _, kb = _ret_operands(True, kb_ref[...], kb_ref[...], cb_ref[...], sb_ref[...])
        i = _row_index(rows).astype(F32)
        lgf, lgb = lg_ref[0:1, :], lg_ref[1:2, :]
        cumf, totf = (i + 1.0) * lgf, rows * lgf
        cumb, totb = (rows - i) * lgb, rows * lgb
    _segment_state_step(stf, kf, vf_ref[...], cumf, totf, bd)
    _segment_state_step(stb, kb, vb_ref[...], cumb, totb, bd)


def _segment_states(gated, proj, n_b, seq, s0f, s0b, lane_rows, consts, tables):
    ns = seq // SEG
    blk = (SEG, GROUP)

    def fwd(cb):
        return pl.BlockSpec(blk, lambda i, s, cb=cb: (i * ns + s, cb))

    def bwd(cb):
        return pl.BlockSpec(blk, lambda i, s, cb=cb: (i * ns + ns - 1 - s, cb))

    st_spec = pl.BlockSpec((None, GROUP, GROUP), lambda i, s: (i, 0, 0))
    if gated:
        in_specs = [fwd(CB_HZF), fwd(CB_HI), bwd(CB_HZB), bwd(CB_HI), _const_spec((2, GROUP)),
                    _const_spec((SEG, SEG)), _const_spec((SEG, SEG))]
        args = [proj] * 4 + [lane_rows, consts["tril_seg"], consts["triu_seg"]]
    else:
        tf = pl.BlockSpec((SEG, LANES), lambda i, s: (s, 0))
        tb = pl.BlockSpec((SEG, LANES), lambda i, s: (ns - 1 - s, 0))
        in_specs = [fwd(CB_RK), fwd(CB_RV), bwd(CB_RK), bwd(CB_RV), tf, tf, tb, tb, _const_spec((2, GROUP))]
        args = [proj] * 4 + [tables[0], tables[1], tables[0], tables[1], lane_rows]
    in_specs += [st_spec, st_spec, _const_spec((GROUP, GROUP))]
    args += [s0f, s0b, consts["bd"]]
    e_shape = _sds((n_b, ns, GROUP, GROUP), F32)
    return pl.pallas_call(
        functools.partial(_states_body, gated),
        grid=(n_b, ns),
        in_specs=in_specs,
        out_specs=[pl.BlockSpec((None, None, GROUP, GROUP), lambda i, s: (i, s, 0, 0)),
                   pl.BlockSpec((None, None, GROUP, GROUP), lambda i, s: (i, ns - 1 - s, 0, 0))],
        out_shape=[e_shape, e_shape],
        scratch_shapes=[pltpu.VMEM((GROUP, GROUP), F32), pltpu.VMEM((GROUP, GROUP), F32)],
        compiler_params=_cparams("parallel", "arbitrary"),
        name="hgrn_states" if gated else "ret_states",
    )(*args)


def _ret_out_body(rope, has_state, emit_state, *refs):
    rq_ref, rk_ref, rv_ref, rg_ref = refs[:4]
    refs = refs[4:]
    cos = sin = None
    if rope:
        cos, sin = refs[0][...], refs[1][...]
        refs = refs[2:]
    lgs_ref, lg_ref, gain_ref, g64_ref = refs[:4]
    refs = refs[4:]
    if has_state:
        ef_ref, eb_ref = refs[:2]
        refs = refs[2:]
    o_ref = refs[0]
    rows = rq_ref.shape[0]
    q, k = _ret_operands(rope, rq_ref[...], rk_ref[...], cos, sin)
    v = rv_ref[...]
    kb16, vb16 = k.astype(BF16), v.astype(BF16)
    s4 = _dot_nt(_stack_heads(q.astype(BF16)), kb16)
    d = (lax.broadcasted_iota(jnp.int32, (rows, rows), 0) - lax.broadcasted_iota(jnp.int32, (rows, rows), 1)).astype(F32)
    slabs = []
    for h in range(N_HEADS):
        m = (jnp.where(d >= 0, jnp.exp(jnp.maximum(d, 0.0) * lgs_ref[0, h]), 0.0)
             + jnp.where(d <= 0, jnp.exp(jnp.maximum(-d, 0.0) * lgs_ref[1, h]), 0.0))
        slabs.append((s4[h * rows:(h + 1) * rows] * m).astype(BF16))
    o = _unstack_heads(_dot(jnp.concatenate(slabs, axis=0), vb16), rows)
    i = _row_index(rows).astype(F32)
    lgf, lgb = lg_ref[0:1, :], lg_ref[1:2, :]
    if has_state:
        o = o + _dot_nt((q * jnp.exp((i + 1.0) * lgf)).astype(BF16), ef_ref[...].astype(BF16))
        o = o + _dot_nt((q * jnp.exp((rows - i) * lgb)).astype(BF16), eb_ref[...].astype(BF16))
    y = _group_rms(o, g64_ref[...], HEAD_DIM, gain_ref[...]) * _silu(rg_ref[...])
    o_ref[...] = y.astype(BF16)
    if emit_state:
        sf_ref, sb_ref = refs[1:3]
        sf_ref[...] = _dot_tn(vb16, (k * jnp.exp((rows - 1.0 - i) * lgf)).astype(BF16))
        sb_ref[...] = _dot_tn(vb16, (k * jnp.exp(i * lgb)).astype(BF16))


def _ret_out(proj, n_b, seq, lg_smem, lg_rows, gain, consts, tables, states, emit_state):
    ns = seq // SEG
    rope = tables is not None
    blk = (SEG, GROUP)

    def col(cb):
        return pl.BlockSpec(blk, lambda i, s, cb=cb: (i * ns + s, cb))

    in_specs = [col(CB_RQ), col(CB_RK), col(CB_RV), col(CB_RG)]
    args = [proj] * 4
    if rope:
        in_specs += [pl.BlockSpec((SEG, LANES), lambda i, s: (s, 0))] * 2
        args += list(tables)
    in_specs += [pl.BlockSpec(memory_space=pltpu.SMEM), _const_spec((2, GROUP)), _const_spec((1, GROUP)),
                 _const_spec((GROUP, GROUP))]
    args += [lg_smem, lg_rows, gain, consts["g64"]]
    if states is not None:
        in_specs += [pl.BlockSpec((None, None, GROUP, GROUP), lambda i, s: (i, s, 0, 0))] * 2
        args += list(states)
    out_specs = [pl.BlockSpec(blk, lambda i, s: (i * ns + s, 0))]
    out_shape = [_sds((n_b * seq, GROUP), BF16)]
    if emit_state:
        assert ns == 1
        out_specs += [pl.BlockSpec((None, GROUP, GROUP), lambda i, s: (i, 0, 0))] * 2
        out_shape += [_sds((n_b, GROUP, GROUP), F32)] * 2
    return pl.pallas_call(
        functools.partial(_ret_out_body, rope, states is not None, emit_state),
        grid=(n_b, ns),
        in_specs=in_specs, out_specs=out_specs, out_shape=out_shape,
        compiler_params=_cparams("parallel", "parallel"),
        name="ret_out_ctx" if states is not None else "ret_out",
    )(*args)


PAIR_LEVELS_MATMUL = (2, 4, 8)
PAIR_LEVELS_CUMSUM = (16, 32, 64, 128, 256)
PAIR_FOLD = 64


def _pair_level_matrix(n, forward):
    mats = []
    j = np.arange(n)[None, :]
    t = np.arange(n)[:, None]
    for g in PAIR_LEVELS_MATMUL:
        half = g // 2
        pos = t % g
        if forward:
            mid = t - pos + half - 1
            m = np.where(pos >= half, (j > mid) & (j <= t), (j > t) & (j <= mid))
        else:
            mid = t - pos + half
            m = np.where(pos < half, (j >= t) & (j < mid), (j >= mid) & (j < t))
        mats.append(m)
    return jnp.asarray(np.concatenate(mats, axis=0), BF16)


def _pair_decays(lf, cs, small_mat, forward):
    rows = lf.shape[0]
    hi = lf.astype(BF16)
    lo = (lf - hi.astype(F32)).astype(BF16)
    d = _dot(small_mat, hi) + _dot(small_mat, lo)
    out = [jnp.exp(d[i * rows:(i + 1) * rows]) for i in range(len(PAIR_LEVELS_MATMUL))]
    for g in PAIR_LEVELS_CUMSUM:
        half = g // 2
        pieces = []
        for grp in range(rows // g):
            mid = grp * g + (half - 1 if forward else half)
            pieces.append(jnp.broadcast_to(cs[mid:mid + 1, :], (g, cs.shape[1])))
        diff = cs - (pieces[0] if len(pieces) == 1 else jnp.concatenate(pieces, axis=0))
        out.append(jnp.exp(jnp.minimum(diff, -diff)))
    return out


def _hgrn_pair_weights(q, k, lf, cs, small_mat, forward):
    rows = q.shape[0]
    nfold = rows // PAIR_FOLD
    levels = PAIR_LEVELS_MATMUL + PAIR_LEVELS_CUMSUM
    decays = _pair_decays(lf, cs, small_mat, forward)
    pos = _row_index(rows)
    t_loc = lax.broadcasted_iota(jnp.int32, (PAIR_FOLD, GROUP), 0)
    s_loc = lax.broadcasted_iota(jnp.int32, (PAIR_FOLD, GROUP), 1) % PAIR_FOLD
    r_big = lax.broadcasted_iota(jnp.int32, (N_HEADS * rows, rows), 0) % rows
    c_big = lax.broadcasted_iota(jnp.int32, (N_HEADS * rows, rows), 1)
    folded = [None] * nfold
    stacked = None
    for g, e in reversed(list(zip(levels, decays))):
        right = (pos % g) >= (g // 2)
        is_query = right if forward else jnp.logical_not(right)
        qe = q * e
        ke = k * e
        qs = jnp.where(is_query, qe, 0.0).astype(BF16)
        ks = jnp.where(is_query, 0.0, ke).astype(BF16)
        if g > PAIR_FOLD:
            s = _dot_nt(_stack_heads(qs), ks)
            if stacked is None:
                stacked = s
            else:
                stacked = jnp.where((r_big // g) == (c_big // g), s, stacked)
        else:
            same = None if g == PAIR_FOLD else (t_loc // g) == (s_loc // g)
            for j in range(nfold):
                r0, r1 = j * PAIR_FOLD, (j + 1) * PAIR_FOLD
                s = _dot_nt(qs[r0:r1], _stack_heads(ks[r0:r1]))
                folded[j] = s if same is None else jnp.where(same, s, folded[j])
    return folded, stacked


def _hgrn_out_body(has_state, emit_state, *refs):
    (q_ref, zf_ref, zb_ref, v_ref, g_ref, lb_ref, gain_ref, g64_ref, bd_ref,
     tril_ref, triu_ref, smf_ref, smb_ref) = refs[:13]
    refs = refs[13:]
    g64 = g64_ref[...]
    if has_state:
        ef, eb = refs[0][...], refs[1][...]
        refs = refs[2:]
    o_ref = refs[0]
    q, v = q_ref[...], v_ref[...]
    rows = q.shape[0]
    lff, kf = _hgrn_gate(zf_ref[...], lb_ref[0:1, :])
    lfb, kb = _hgrn_gate(zb_ref[...], lb_ref[1:2, :])
    csf = _sum_rows(tril_ref[...], lff)
    csb = _sum_rows(triu_ref[...], lfb)
    fold_f, stack_f = _hgrn_pair_weights(q, kf, lff, csf, smf_ref[...], True)
    fold_b, stack_b = _hgrn_pair_weights(q, kb, lfb, csb, smb_ref[...], False)
    vb = v.astype(BF16)
    o = _dot((q * (kf + kb)).astype(BF16), g64) * v
    o = o + _unstack_heads(_dot((stack_f + stack_b).astype(BF16), vb), rows)
    tiles = []
    for j in range(rows // PAIR_FOLD):
        r0, r1 = j * PAIR_FOLD, (j + 1) * PAIR_FOLD
        tiles.append(_dot((fold_f[j] + fold_b[j]).astype(BF16), _stack_heads(vb[r0:r1])))
    o = o + jnp.concatenate(tiles, axis=0)
    if has_state:
        o = o + _dot_nt((q * jnp.exp(csf)).astype(BF16), ef.astype(BF16))
        o = o + _dot_nt((q * jnp.exp(csb)).astype(BF16), eb.astype(BF16))
    y = _group_rms(o, g64, HEAD_DIM, gain_ref[...]) * _silu(g_ref[...])
    o_ref[...] = y.astype(BF16)
    if emit_state:
        totf, totb = csf[rows - 1:rows, :], csb[0:1, :]
        stf = _dot_tn(vb, (kf * jnp.exp(totf - csf)).astype(BF16))
        stb = _dot_tn(vb, (kb * jnp.exp(totb - csb)).astype(BF16))
        if has_state:
            bd = bd_ref[...]
            stf = ef * jnp.exp(totf) + stf * bd
            stb = eb * jnp.exp(totb) + stb * bd
        refs[1][...] = stf
        refs[2][...] = stb


def _hgrn_out(proj, n_b, seq, lb_rows, gain, consts, states, emit_state):
    ns = seq // SEG
    blk = (SEG, GROUP)

    def col(cb):
        return pl.BlockSpec(blk, lambda i, s, cb=cb: (i * ns + s, cb))

    n_small = len(PAIR_LEVELS_MATMUL) * SEG
    in_specs = [col(CB_HQ), col(CB_HZF), col(CB_HZB), col(CB_HI), col(CB_HG),
                _const_spec((2, GROUP)), _const_spec((1, GROUP)), _const_spec((GROUP, GROUP)),
                _const_spec((GROUP, GROUP)), _const_spec((SEG, SEG)), _const_spec((SEG, SEG)),
                _const_spec((n_small, SEG)), _const_spec((n_small, SEG))]
    args = [proj] * 5 + [lb_rows, gain, consts["g64"], consts["bd"],
                         consts["tril_seg"], consts["triu_seg"], consts["pair_f"], consts["pair_b"]]
    if states is not None:
        in_specs += [pl.BlockSpec((None, None, GROUP, GROUP), lambda i, s: (i, s, 0, 0))] * 2
        args += list(states)
    out_specs = [pl.BlockSpec(blk, lambda i, s: (i * ns + s, 0))]
    out_shape = [_sds((n_b * seq, GROUP), BF16)]
    if emit_state:
        assert ns == 1
        out_specs += [pl.BlockSpec((None, GROUP, GROUP), lambda i, s: (i, 0, 0))] * 2
        out_shape += [_sds((n_b, GROUP, GROUP), F32)] * 2
    return pl.pallas_call(
        functools.partial(_hgrn_out_body, states is not None, emit_state),
        grid=(n_b, ns),
        in_specs=in_specs, out_specs=out_specs, out_shape=out_shape,
        compiler_params=_cparams("parallel", "parallel"),
        name="hgrn_out_ctx" if states is not None else "hgrn_out",
    )(*args)


def _ffn_body(x_ref, oa_ref, ob_ref, oc_ref, od_ref, wo_ref, g1_ref, n2_ref, sc_ref, sh_ref, g2_ref,
              wg_ref, wu_ref, wd_ref, y_ref, x1_ref, h_ref, acc_ref):
    j = pl.program_id(1)

    @pl.when(j == 0)
    def _():
        mix = _dot(oa_ref[...], wo_ref[0:GROUP, :])
        mix += _dot_tn(ob_ref[...], wo_ref[GROUP:2 * GROUP, :])
        mix += _dot_tn(oc_ref[...], wo_ref[2 * GROUP:3 * GROUP, :])
        mix += _dot(od_ref[...], wo_ref[3 * GROUP:4 * GROUP, :])
        x1 = x_ref[...] + g1_ref[...] * mix
        x1_ref[...] = x1
        h = x1 * lax.rsqrt(jnp.mean(x1 * x1, axis=-1, keepdims=True) + EPS) * n2_ref[...]
        h_ref[...] = (h * (1.0 + sc_ref[...]) + sh_ref[...]).astype(BF16)
        acc_ref[...] = jnp.zeros_like(acc_ref)

    h = h_ref[...]
    a = _silu(_dot(h, wg_ref[...])) * _dot(h, wu_ref[...])
    acc_ref[...] += _dot(a.astype(BF16), wd_ref[...])

    @pl.when(j == pl.num_programs(1) - 1)
    def _():
        y_ref[...] = x1_ref[...] + g2_ref[...] * acc_ref[...]


def _outproj_ffn(x, mixes, w_out, g1, n2, sc2, sh2, g2, w_in, w_dn, rows_per_mod, mod_row0, tm=1024, th=256):
    t, d = x.shape
    hid = w_dn.shape[0]
    nh = hid // th
    mod_map = _mod_row_map(tm, rows_per_mod, mod_row0)
    mod2 = lambda i, j: mod_map(i)
    row = lambda i, j: (i, 0)
    mix_spec = pl.BlockSpec((tm, GROUP), row)
    mix_t_spec = pl.BlockSpec((GROUP, tm), lambda i, j: (0, i))
    mod_spec = pl.BlockSpec((None, 1, d), mod2)
    return pl.pallas_call(
        _ffn_body,
        grid=(t // tm, nh),
        in_specs=[pl.BlockSpec((tm, d), row), mix_spec, mix_t_spec, mix_t_spec, mix_spec]
                 + [_const_spec((d, d)), mod_spec, _const_spec((1, d)), mod_spec, mod_spec, mod_spec,
                    pl.BlockSpec((d, th), lambda i, j: (0, j)),
                    pl.BlockSpec((d, th), lambda i, j: (0, nh + j)),
                    pl.BlockSpec((th, d), lambda i, j: (j, 0))],
        out_specs=pl.BlockSpec((tm, d), row),
        out_shape=_sds((t, d), F32),
        scratch_shapes=[pltpu.VMEM((tm, d), F32), pltpu.VMEM((tm, d), BF16), pltpu.VMEM((tm, d), F32)],
        compiler_params=_cparams("parallel", "arbitrary"),
        name="outproj_ffn",
    )(x, *mixes, w_out, g1, n2, sc2, sh2, g2, w_in, w_in, w_dn)


def _lane_rows(per_head):
    return jnp.repeat(per_head.astype(F32), HEAD_DIM, axis=1)


def _states_to_lanes(s):
    b = s.shape[0]
    eye = jnp.eye(N_HEADS, dtype=F32)
    st = jnp.swapaxes(s.astype(F32), -1, -2)[:, :, :, :, None, :] * eye[None, None, :, None, :, None]
    st = st.reshape(b, 2, GROUP, GROUP)
    return st[:, 0], st[:, 1]


def _with_ones_rows(vt):
    ones = jnp.ones(vt.shape[:-2] + (V_ROWS - HEAD_DIM, vt.shape[-1]), vt.dtype)
    return jnp.concatenate([vt, ones], axis=-2)


def _lanes_to_states(sf, sb):
    def diag(x):
        b = x.shape[0]
        x = x.reshape(b, N_HEADS, HEAD_DIM, N_HEADS, HEAD_DIM)
        return jnp.stack([x[:, h, :, h, :] for h in range(N_HEADS)], axis=1).swapaxes(-1, -2)
    return jnp.stack([diag(sf), diag(sb)], axis=1)


def kernel(x_prompt, x_sample, state_ret, cache_win_k, cache_win_v, cache_diff_k, cache_diff_v, state_hgrn, c, c_ctx, norm1_g, norm2_g, w_ada, b_ada, w_in, ret_decay, ret_norm_g, win_q_norm, win_k_norm, win_sink, diff_q_norm, diff_k_norm, diff_lambda, diff_norm_g, hgrn_lb_logits, hgrn_norm_g, w_out, w_ffn_in, w_ffn_out):
    n_p, seq_p, d = x_prompt.shape
    n_s, seq_s, _ = x_sample.shape
    depth = w_in.shape[0]
    t_s, t_p = n_s * seq_s, n_p * seq_p
    assert seq_p == SEG and seq_s % SEG == 0 and d == N_HEADS * GROUP

    tril_seg, triu_seg = _cumsum_matrices(SEG)
    consts = dict(g64=_group_matrix(GROUP, HEAD_DIM), g32=_group_matrix(GROUP, DIFF_DIM),
                  bd=_group_matrix(GROUP, HEAD_DIM).astype(F32), dupk=_kv_dup_matrix(),
                  tril_seg=tril_seg, triu_seg=triu_seg,
                  pair_f=_pair_level_matrix(SEG, True), pair_b=_pair_level_matrix(SEG, False))
    tab64 = _rope_tables(seq_s, HEAD_DIM)
    tab32 = _rope_tables(seq_s, DIFF_DIM)

    n_rows = -(-(n_s + 1) // 8) * 8
    cond = jnp.zeros((n_rows, d), F32).at[:n_s].set(c).at[n_s].set(c_ctx)
    mod = _adaln(cond, w_ada, b_ada).reshape(depth, n_rows, 6, 1, d)

    lb_p = jax.nn.softmax(hgrn_lb_logits.astype(F32), axis=0)
    lb_all = jnp.cumsum(lb_p, axis=0) - lb_p
    log_gamma = -jnp.exp(ret_decay.astype(F32))

    xs = x_sample.reshape(t_s, d)
    xp = x_prompt.reshape(t_p, d)
    new_ret, new_wk, new_wv, new_dk, new_dv, new_hg = [], [], [], [], [], []
    for l in range(depth):
        lam_init = 0.8 - 0.6 * math.exp(-0.3 * l)
        sh1, sc1, g1, sh2, sc2, g2 = [mod[l, :, i] for i in range(6)]
        w_in_l, w_out_l = w_in[l].astype(BF16), w_out[l].astype(BF16)
        w_up_l, w_dn_l = w_ffn_in[l].astype(BF16), w_ffn_out[l].astype(BF16)
        n1, n2 = norm1_g[l][None], norm2_g[l][None]
        gains = (jnp.tile(win_q_norm[l], N_HEADS)[None], jnp.tile(win_k_norm[l], KV_HEADS)[None],
                 jnp.tile(diff_q_norm[l], 2 * N_HEADS)[None], jnp.tile(diff_k_norm[l], 2 * N_HEADS)[None])
        sink = win_sink[l].astype(F32)
        lq1, lk1, lq2, lk2 = diff_lambda[l].astype(F32)
        lam = jnp.exp(jnp.sum(lq1 * lk1)) - jnp.exp(jnp.sum(lq2 * lk2)) + lam_init
        lam2 = jnp.stack([lam, jnp.asarray(1.0 - lam_init, F32)]).reshape(1, 2)
        dgain = diff_norm_g[l].astype(F32)[:, None]
        lg = log_gamma[l]
        lg_rows = _lane_rows(lg)
        rgain = ret_norm_g[l][None]
        lb_rows = lb_all[l]
        hgain = hgrn_norm_g[l][None]

        proj = _inproj(xs, n1, sc1, sh1, w_in_l, seq_s, 0)
        wq, wk, wvt, dq, dk, dvt = _prep(proj, gains, consts, tab64 + tab32, False)
        ctx_wk = jnp.repeat(cache_win_k[:, l].transpose(0, 2, 1, 3), N_HEADS // KV_HEADS, axis=2)
        ctx_wk = ctx_wk.reshape(n_s, -1, GROUP).astype(BF16)
        ctx_wvt = _with_ones_rows(cache_win_v[:, l].swapaxes(-1, -2).astype(BF16))
        o_win = _win_sample(wq, wk, wvt, ctx_wk, ctx_wvt, jnp.repeat(sink * LOG2_E, WIN_BLOCK)[None])
        ctx_dk = cache_diff_k[:, l].transpose(0, 3, 1, 2, 4).reshape(n_s, -1, GROUP).astype(BF16)
        ctx_dvt = _with_ones_rows(cache_diff_v[:, l].swapaxes(-1, -2).astype(BF16))
        o_diff = _diff_attention(dq, dk, dvt, (ctx_dk, ctx_dvt), seq_s, lam2, dgain, 1)
        s0f, s0b = _states_to_lanes(state_ret[:, l])
        ret_e = _segment_states(False, proj, n_s, seq_s, s0f, s0b, lg_rows, consts, tab64)
        (o_ret,) = _ret_out(proj, n_s, seq_s, lg, lg_rows, rgain, consts, tab64, ret_e, False)
        h0f, h0b = _states_to_lanes(state_hgrn[:, l])
        hg_e = _segment_states(True, proj, n_s, seq_s, h0f, h0b, lb_rows, consts, None)
        (o_h,) = _hgrn_out(proj, n_s, seq_s, lb_rows, hgain, consts, hg_e, False)
        xs = _outproj_ffn(xs, (o_ret, o_win, o_diff, o_h), w_out_l, g1, n2, sc2, sh2, g2, w_up_l, w_dn_l, seq_s, 0)

        proj = _inproj(xp, n1, sc1, sh1, w_in_l, t_p, n_s)
        wq, wk, wvt, dq, dk, dvt, wk_f, dk_f = _prep(proj, gains, consts, None, True)
        o_win = _win_prompt(wq, wk, wvt, seq_p, jnp.repeat(sink * LOG2_E, seq_p)[None])
        o_diff = _diff_attention(dq, dk, dvt, None, seq_p, lam2, dgain, N_HEADS)
        o_ret, rsf, rsb = _ret_out(proj, n_p, seq_p, lg, lg_rows, rgain, consts, None, None, True)
        o_h, hsf, hsb = _hgrn_out(proj, n_p, seq_p, lb_rows, hgain, consts, None, True)
        xp = _outproj_ffn(xp, (o_ret, o_win, o_diff, o_h), w_out_l, g1, n2, sc2, sh2, g2, w_up_l, w_dn_l, t_p, n_s)

        new_ret.append(_lanes_to_states(rsf, rsb))
        new_hg.append(_lanes_to_states(hsf, hsb))
        new_wk.append(wk_f.reshape(n_p, seq_p, KV_HEADS, HEAD_DIM).transpose(0, 2, 1, 3))
        new_wv.append(proj[:, CB_WKV * GROUP + LANES:(CB_WKV + 1) * GROUP]
                      .reshape(n_p, seq_p, KV_HEADS, HEAD_DIM).transpose(0, 2, 1, 3))
        new_dk.append(dk_f.reshape(n_p, seq_p, N_HEADS, 2, DIFF_DIM).transpose(0, 2, 3, 1, 4))
        new_dv.append(proj[:, CB_DV * GROUP:(CB_DV + 1) * GROUP]
                      .reshape(n_p, seq_p, N_HEADS, HEAD_DIM).transpose(0, 2, 1, 3))

    return (xp.reshape(n_p, seq_p, d), xs.reshape(n_s, seq_s, d),
            jnp.stack(new_ret, axis=1), jnp.stack(new_wk, axis=1), jnp.stack(new_wv, axis=1),
            jnp.stack(new_dk, axis=1), jnp.stack(new_dv, axis=1), jnp.stack(new_hg, axis=1))
```

```python
import functools
import math

import numpy as np
import jax
import jax.numpy as jnp
from jax import lax
from jax.experimental import pallas as pl
from jax.experimental.pallas import tpu as pltpu

F32 = jnp.float32
BF16 = jnp.bfloat16

GROUP = 256
HEAD_DIM = 64
N_HEADS = 4
KV_HEADS = 2
DIFF_DIM = 32
WINDOW = 128
WIN_BLOCK = 128
GRID_W = 64
ROPE_BASE = 10000.0
EPS = 1e-6
MASK_VALUE = -1e30
SEG = 256
LANES = 128
BF16_SUBLANES = 16
LOG2_E = math.log2(math.e)
V_ROWS = HEAD_DIM + BF16_SUBLANES
DIFF_LOOKAHEAD = 2
V7X_VMEM_LIMIT_BYTES = 56 * 1024 * 1024

(CB_RQ, CB_RK, CB_RV, CB_RG, CB_WQ, CB_WKV, CB_DQ, CB_DK, CB_DV,
 CB_HQ, CB_HZF, CB_HZB, CB_HI, CB_HG) = range(14)

_NT = (((1,), (1,)), ((), ()))
_TN = (((0,), (0,)), ((), ()))


def _sds(shape, dtype):
    return jax.ShapeDtypeStruct(shape, dtype)


def _cparams(*sem):
    return pltpu.CompilerParams(dimension_semantics=sem, vmem_limit_bytes=V7X_VMEM_LIMIT_BYTES)


def _const_spec(shape):
    return pl.BlockSpec(shape, lambda *_: (0,) * len(shape))


def _dot(a, b):
    return jnp.dot(a, b, preferred_element_type=F32)


def _dot_nt(a, b):
    return lax.dot_general(a, b, _NT, preferred_element_type=F32)


def _dot_tn(a, b):
    return lax.dot_general(a, b, _TN, preferred_element_type=F32)


def _silu(x):
    return x * jax.nn.sigmoid(x)


def _group_matrix(width, gsize):
    i = np.arange(width)
    return jnp.asarray((i[:, None] // gsize) == (i[None, :] // gsize), BF16)


def _cumsum_matrices(n):
    r, c = np.arange(n)[:, None], np.arange(n)[None, :]
    return jnp.asarray(c <= r, BF16), jnp.asarray(c >= r, BF16)


def _kv_dup_matrix():
    d = np.zeros((KV_HEADS * HEAD_DIM, GROUP), np.float32)
    for h in range(N_HEADS):
        kv = h // (N_HEADS // KV_HEADS)
        for j in range(HEAD_DIM):
            d[kv * HEAD_DIM + j, h * HEAD_DIM + j] = 1.0
    return jnp.asarray(d, BF16)


def _rope_tables(n_tokens, head_dim):
    d = head_dim // 2
    half = d // 2
    inv = ROPE_BASE ** (-jnp.arange(half, dtype=F32) / half)
    t = jnp.arange(n_tokens)
    row = (t // GRID_W).astype(F32)
    col = (t % GRID_W).astype(F32)
    j = np.arange(LANES) % head_dim
    w = j % d
    use_row = jnp.asarray((j // d) == 0)
    pos = jnp.where(use_row[None, :], row[:, None], col[:, None])
    ang = pos * inv[w % half][None, :]
    sign = jnp.asarray(np.where(w >= half, 1.0, -1.0), F32)
    return jnp.cos(ang), jnp.sin(ang) * sign[None, :]


def _group_rms(x, g_mat, gsize, gain):
    ss = _dot((x * x).astype(BF16), g_mat)
    return x * lax.rsqrt(ss * (1.0 / gsize) + EPS) * gain


def _rope(y, cos, sin, half):
    lane = lax.broadcasted_iota(jnp.int32, (1, LANES), 1)
    second = (lane % (2 * half)) >= half
    outs = []
    for p in range(y.shape[1] // LANES):
        z = y[:, p * LANES:(p + 1) * LANES]
        partner = jnp.where(second, pltpu.roll(z, half, 1), pltpu.roll(z, LANES - half, 1))
        outs.append(z * cos + partner * sin)
    return outs[0] if len(outs) == 1 else jnp.concatenate(outs, axis=1)


def _head_of_lane(width=GROUP):
    return lax.broadcasted_iota(jnp.int32, (1, width), 1) // HEAD_DIM


def _stack_heads(q, n_heads=N_HEADS):
    head = _head_of_lane()
    zero = jnp.zeros_like(q)
    return jnp.concatenate([jnp.where(head == h, q, zero) for h in range(n_heads)], axis=0)


def _unstack_heads(o4, rows):
    head = _head_of_lane()
    out = jnp.where(head == 0, o4[:rows], 0.0)
    for h in range(1, N_HEADS):
        out = out + jnp.where(head == h, o4[h * rows:(h + 1) * rows], 0.0)
    return out


def _split3(x):
    hi = x.astype(BF16)
    r = x - hi.astype(F32)
    mid = r.astype(BF16)
    lo = (r - mid.astype(F32)).astype(BF16)
    return hi, mid, lo


def _sum_rows(m01, x):
    hi, mid, lo = _split3(x)
    return _dot(m01, hi) + _dot(m01, mid) + _dot(m01, lo)


def _adaln_body(c_ref, w_ref, b_ref, o_ref):
    c = c_ref[...]
    o_ref[...] = jnp.dot(_silu(c), w_ref[...], preferred_element_type=F32,
                         precision=lax.Precision.HIGHEST) + b_ref[...]


def _adaln(cond, w_ada, b_ada):
    depth, d, n = w_ada.shape
    rows = cond.shape[0]
    tn = n // 4
    return pl.pallas_call(
        _adaln_body,
        grid=(depth, n // tn),
        in_specs=[pl.BlockSpec((rows, d), lambda l, j: (0, 0)),
                  pl.BlockSpec((None, d, tn), lambda l, j: (l, 0, j)),
                  pl.BlockSpec((None, 1, tn), lambda l, j: (l, 0, j))],
        out_specs=pl.BlockSpec((None, rows, tn), lambda l, j: (l, 0, j)),
        out_shape=_sds((depth, rows, n), F32),
        compiler_params=_cparams("parallel", "parallel"),
        name="adaln",
    )(cond, w_ada, b_ada.reshape(depth, 1, n))


def _inproj_body(x_ref, g_ref, sc_ref, sh_ref, w_ref, o_ref):
    x = x_ref[...]
    h = x * lax.rsqrt(jnp.mean(x * x, axis=-1, keepdims=True) + EPS) * g_ref[...]
    h = h * (1.0 + sc_ref[...]) + sh_ref[...]
    o_ref[...] = _dot(h.astype(BF16), w_ref[...])


def _mod_row_map(tm, rows_per_mod, mod_row0):
    return lambda i: (mod_row0 + (i * tm) // rows_per_mod, 0, 0)


def _inproj(x, g, sc, sh, w, rows_per_mod, mod_row0, tm=512):
    t, d = x.shape
    n = w.shape[1]
    mod_map = _mod_row_map(tm, rows_per_mod, mod_row0)
    return pl.pallas_call(
        _inproj_body,
        grid=(t // tm,),
        in_specs=[pl.BlockSpec((tm, d), lambda i: (i, 0)),
                  _const_spec((1, d)),
                  pl.BlockSpec((None, 1, d), mod_map),
                  pl.BlockSpec((None, 1, d), mod_map),
                  _const_spec((d, n))],
        out_specs=pl.BlockSpec((tm, n), lambda i: (i, 0)),
        out_shape=_sds((t, n), F32),
        compiler_params=_cparams("parallel"),
        name="inproj",
    )(x, g, sc, sh, w)


def _prep_body(rope, emit_f32, *refs):
    (wq_ref, wkv_ref, dq_ref, dk_ref, dv_ref, qn_ref, kn_ref, dqn_ref, dkn_ref, g64_ref, g32_ref,
     dup_ref) = refs[:12]
    refs = refs[12:]
    if rope:
        c64_ref, s64_ref, c32_ref, s32_ref = refs[:4]
        refs = refs[4:]
    wq_o, wk_o, wvt_o, dq_o, dk_o, dvt_o = refs[:6]
    g64 = g64_ref[...]
    g32 = g32_ref[...]
    tm = dv_ref.shape[0]

    wq = _group_rms(wq_ref[...], g64, HEAD_DIM, qn_ref[...])
    wkv = wkv_ref[...]
    wk = _group_rms(wkv[:, :LANES], g64[:LANES, :LANES], HEAD_DIM, kn_ref[...])
    dq = _group_rms(dq_ref[...], g32, DIFF_DIM, dqn_ref[...])
    dk = _group_rms(dk_ref[...], g32, DIFF_DIM, dkn_ref[...])
    if rope:
        c64, s64, c32, s32 = c64_ref[...], s64_ref[...], c32_ref[...], s32_ref[...]
        wq = _rope(wq, c64, s64, HEAD_DIM // 4)
        wk = _rope(wk, c64, s64, HEAD_DIM // 4)
        dq = _rope(dq, c32, s32, DIFF_DIM // 4)
        dk = _rope(dk, c32, s32, DIFF_DIM // 4)
    wq_o[...] = (wq * (HEAD_DIM ** -0.5 * LOG2_E)).astype(BF16)
    wk_o[...] = _dot(wk.astype(BF16), dup_ref[...]).astype(BF16)
    wvt_o[:, :HEAD_DIM, :] = wkv[:, LANES:].T.reshape(KV_HEADS, HEAD_DIM, tm).astype(BF16)
    wvt_o[:, HEAD_DIM:, :] = jnp.ones((KV_HEADS, V_ROWS - HEAD_DIM, tm), BF16)
    dq_o[...] = (dq * (DIFF_DIM ** -0.5 * LOG2_E)).astype(BF16)
    dk_o[...] = dk.astype(BF16)
    dvt_o[:, :HEAD_DIM, :] = dv_ref[...].T.reshape(N_HEADS, HEAD_DIM, tm).astype(BF16)
    dvt_o[:, HEAD_DIM:, :] = jnp.ones((N_HEADS, V_ROWS - HEAD_DIM, tm), BF16)
    if emit_f32:
        wk_c, wv_c, dk_c, dv_c = refs[6:10]
        dv = dv_ref[...]
        for kv in range(KV_HEADS):
            wk_c[kv] = wk[:, kv * HEAD_DIM:(kv + 1) * HEAD_DIM]
            wv_c[kv] = wkv[:, LANES + kv * HEAD_DIM:LANES + (kv + 1) * HEAD_DIM]
        for h in range(N_HEADS):
            dv_c[h] = dv[:, h * HEAD_DIM:(h + 1) * HEAD_DIM]
            for c in range(2):
                lane0 = h * HEAD_DIM + c * DIFF_DIM
                dk_c[h, c] = dk[:, lane0:lane0 + DIFF_DIM]


def _prep(proj, gains, consts, tables, emit_f32, tm=256):
    rope = tables is not None
    rows = proj.shape[0]

    def col(cb):
        return pl.BlockSpec((tm, GROUP), lambda i, cb=cb: (i, cb))

    def vt(heads):
        return pl.BlockSpec((heads, V_ROWS, tm), lambda i: (0, 0, i))

    in_specs = [col(CB_WQ), col(CB_WKV), col(CB_DQ), col(CB_DK), col(CB_DV),
                _const_spec((1, GROUP)), _const_spec((1, LANES)), _const_spec((1, GROUP)), _const_spec((1, GROUP)),
                _const_spec((GROUP, GROUP)), _const_spec((GROUP, GROUP)), _const_spec((LANES, GROUP))]
    args = [proj] * 5 + list(gains) + [consts["g64"], consts["g32"], consts["dupk"]]
    if rope:
        seq = tables[0].shape[0]
        nper = seq // tm
        in_specs += [pl.BlockSpec((tm, LANES), lambda i: (i % nper, 0))] * 4
        args += list(tables)
    out = pl.BlockSpec((tm, GROUP), lambda i: (i, 0))
    tok = _sds((rows, GROUP), BF16)
    out_specs = [out, out, vt(KV_HEADS), out, out, vt(N_HEADS)]
    out_shape = [tok, tok, _sds((KV_HEADS, V_ROWS, rows), BF16), tok, tok, _sds((N_HEADS, V_ROWS, rows), BF16)]
    if emit_f32:
        nb = rows // tm
        out_specs += [pl.BlockSpec((None, KV_HEADS, tm, HEAD_DIM), lambda i: (i, 0, 0, 0)),
                      pl.BlockSpec((None, KV_HEADS, tm, HEAD_DIM), lambda i: (i, 0, 0, 0)),
                      pl.BlockSpec((None, N_HEADS, 2, tm, DIFF_DIM), lambda i: (i, 0, 0, 0, 0)),
                      pl.BlockSpec((None, N_HEADS, tm, HEAD_DIM), lambda i: (i, 0, 0, 0))]
        out_shape += [_sds((nb, KV_HEADS, tm, HEAD_DIM), F32), _sds((nb, KV_HEADS, tm, HEAD_DIM), F32),
                      _sds((nb, N_HEADS, 2, tm, DIFF_DIM), F32), _sds((nb, N_HEADS, tm, HEAD_DIM), F32)]
    return pl.pallas_call(
        functools.partial(_prep_body, rope, emit_f32),
        grid=(rows // tm,),
        in_specs=in_specs, out_specs=out_specs, out_shape=out_shape,
        compiler_params=_cparams("parallel"),
        name="prep_rope" if rope else "prep",
    )(*args)


def _win_body(banded, *refs):
    if banded:
        (q_ref, kp_ref, kc_ref, kn_ref, vp_ref, vc_ref, vn_ref, ck_ref, cvt_ref, sink_ref, o_ref) = refs
        k_refs, v_refs = (kp_ref, kc_ref, kn_ref), (vp_ref, vc_ref, vn_ref)
    else:
        q_ref, kc_ref, vc_ref, sink_ref, o_ref = refs
        k_refs, v_refs = (kc_ref,), (vc_ref,)
    tq = q_ref.shape[0]
    q4 = _stack_heads(q_ref[...])
    k_loc = k_refs[0][...] if len(k_refs) == 1 else jnp.concatenate([r[...] for r in k_refs], axis=0)
    s_loc = _dot_nt(k_loc, q4)
    sink = sink_ref[...]
    if banded:
        n = pl.program_id(1)
        seq = pl.num_programs(1) * WIN_BLOCK
        t = n * WIN_BLOCK + lax.broadcasted_iota(jnp.int32, s_loc.shape, 1) % WIN_BLOCK
        kpos = (n - 1) * WIN_BLOCK + lax.broadcasted_iota(jnp.int32, s_loc.shape, 0)
        valid = (kpos >= 0) & (kpos < seq) & (jnp.abs(t - kpos) <= WINDOW)
        s_loc = jnp.where(valid, s_loc, MASK_VALUE)
        s_ctx = _dot_nt(ck_ref[...], q4)
        m = jnp.maximum(jnp.maximum(s_loc.max(axis=0, keepdims=True), s_ctx.max(axis=0, keepdims=True)), sink)
        p_ctx = jnp.exp2(s_ctx - m).astype(BF16)
    else:
        m = jnp.maximum(s_loc.max(axis=0, keepdims=True), sink)
    p_loc = jnp.exp2(s_loc - m).astype(BF16)
    p_sink = jnp.exp2(sink - m)
    outs = []
    for h in range(N_HEADS):
        kv = h // (N_HEADS // KV_HEADS)
        c0, c1 = h * tq, (h + 1) * tq
        vt = v_refs[0][kv] if len(v_refs) == 1 else jnp.concatenate([r[kv] for r in v_refs], axis=1)
        acc = _dot(vt, p_loc[:, c0:c1])
        if banded:
            acc = acc + _dot(cvt_ref[kv], p_ctx[:, c0:c1])
        outs.append(acc[:HEAD_DIM] / (acc[HEAD_DIM:HEAD_DIM + 1] + p_sink[:, c0:c1]))
    o_ref[...] = jnp.concatenate(outs, axis=0).astype(BF16)


def _win_sample(wq, wk, wvt, ctx_k, ctx_vt, sink_row):
    b, p, _ = ctx_k.shape
    rows = wq.shape[0]
    nq = rows // b // WIN_BLOCK
    blk = (WIN_BLOCK, GROUP)
    vblk = (KV_HEADS, V_ROWS, WIN_BLOCK)
    prev = lambda i, n: i * nq + jnp.maximum(n - 1, 0)
    cur = lambda i, n: i * nq + n
    nxt = lambda i, n: i * nq + jnp.minimum(n + 1, nq - 1)
    return pl.pallas_call(
        functools.partial(_win_body, True),
        grid=(b, nq),
        in_specs=[pl.BlockSpec(blk, lambda i, n: (cur(i, n), 0)),
                  pl.BlockSpec(blk, lambda i, n: (prev(i, n), 0)),
                  pl.BlockSpec(blk, lambda i, n: (cur(i, n), 0)),
                  pl.BlockSpec(blk, lambda i, n: (nxt(i, n), 0)),
                  pl.BlockSpec(vblk, lambda i, n: (0, 0, prev(i, n))),
                  pl.BlockSpec(vblk, lambda i, n: (0, 0, cur(i, n))),
                  pl.BlockSpec(vblk, lambda i, n: (0, 0, nxt(i, n))),
                  pl.BlockSpec((None, p, GROUP), lambda i, n: (i, 0, 0)),
                  pl.BlockSpec((None, KV_HEADS, V_ROWS, p), lambda i, n: (i, 0, 0, 0)),
                  _const_spec((1, N_HEADS * WIN_BLOCK))],
        out_specs=pl.BlockSpec((GROUP, WIN_BLOCK), lambda i, n: (0, cur(i, n))),
        out_shape=_sds((GROUP, rows), BF16),
        compiler_params=_cparams("parallel", "parallel"),
        name="win_sample",
    )(wq, wk, wk, wk, wvt, wvt, wvt, ctx_k, ctx_vt, sink_row)


def _win_prompt(wq, wk, wvt, seq, sink_row):
    rows = wq.shape[0]
    blk = (seq, GROUP)
    return pl.pallas_call(
        functools.partial(_win_body, False),
        grid=(rows // seq,),
        in_specs=[pl.BlockSpec(blk, lambda i: (i, 0)), pl.BlockSpec(blk, lambda i: (i, 0)),
                  pl.BlockSpec((KV_HEADS, V_ROWS, seq), lambda i: (0, 0, i)),
                  _const_spec((1, N_HEADS * seq))],
        out_specs=pl.BlockSpec((GROUP, seq), lambda i: (0, i)),
        out_shape=_sds((GROUP, rows), BF16),
        compiler_params=_cparams("parallel"),
        name="win_prompt",
    )(wq, wk, wvt, sink_row)


def _diff_body(ck, has_ctx, heads, *refs):
    if has_ctx:
        q_ref, k_ref, vt_ref, ck_ref, cvt_ref, lam_ref, gain_ref, o_ref = refs
    else:
        q_ref, k_ref, vt_ref, lam_ref, gain_ref, o_ref = refs
    for j in range(heads):
        sources = [(k_ref, vt_ref.at[j])] + ([(ck_ref, cvt_ref.at[j])] if has_ctx else [])
        rows = slice(j * HEAD_DIM, (j + 1) * HEAD_DIM)
        o_ref[rows, :] = _diff_head(ck, pl.program_id(2) * heads + j, q_ref[...], sources,
                                    lam_ref, gain_ref[rows, :])


def _diff_head(ck, h, q, sources, lam_ref, gain):
    tq = q.shape[0]
    sub = lax.broadcasted_iota(jnp.int32, (1, GROUP), 1) // DIFF_DIM
    zero = jnp.zeros_like(q)
    q2 = jnp.concatenate([jnp.where(sub == 2 * h, q, zero), jnp.where(sub == 2 * h + 1, q, zero)], axis=0)
    chunks = []
    for kr, vr in sources:
        n_keys = kr.shape[0]
        step = min(ck, n_keys)
        chunks += [(kr, vr, c0, step) for c0 in range(0, n_keys, step)]

    def scores(i):
        kr, _, c0, step = chunks[i]
        return _dot_nt(kr[c0:c0 + step, :], q2)

    pending = [scores(i) for i in range(min(DIFF_LOOKAHEAD, len(chunks)))]
    m = acc = None
    for i, (_, vr, c0, step) in enumerate(chunks):
        s = pending.pop(0)
        if i + DIFF_LOOKAHEAD < len(chunks):
            pending.append(scores(i + DIFF_LOOKAHEAD))
        vt = vr[:, c0:c0 + step]
        mc = s.max(axis=0, keepdims=True)
        if m is None:
            m = mc
            acc = _dot(vt, jnp.exp2(s - m).astype(BF16))
        else:
            m_new = jnp.maximum(m, mc)
            acc = jnp.exp2(m - m_new) * acc + _dot(vt, jnp.exp2(s - m_new).astype(BF16))
            m = m_new
    o2 = acc[:HEAD_DIM] / acc[HEAD_DIM:HEAD_DIM + 1]
    od = o2[:, :tq] - lam_ref[0, 0] * o2[:, tq:]
    y = od * lax.rsqrt(jnp.mean(od * od, axis=0, keepdims=True) + EPS) * gain * lam_ref[0, 1]
    return y.astype(BF16)


def _diff_attention(dq, dk, dvt, ctx, seq, lam2, gain_col, heads, tq=512, ck=512):
    rows = dq.shape[0]
    b = rows // seq
    tq = min(tq, seq)
    nq = seq // tq
    in_specs = [pl.BlockSpec((tq, GROUP), lambda i, n, h: (i * nq + n, 0)),
                pl.BlockSpec((seq, GROUP), lambda i, n, h: (i, 0)),
                pl.BlockSpec((heads, V_ROWS, seq), lambda i, n, h: (h, 0, i))]
    args = [dq, dk, dvt]
    if ctx is not None:
        p = ctx[0].shape[1]
        in_specs += [pl.BlockSpec((None, p, GROUP), lambda i, n, h: (i, 0, 0)),
                     pl.BlockSpec((None, heads, V_ROWS, p), lambda i, n, h: (i, h, 0, 0))]
        args += list(ctx)
    in_specs += [pl.BlockSpec(memory_space=pltpu.SMEM),
                 pl.BlockSpec((heads * HEAD_DIM, 1), lambda i, n, h: (h, 0))]
    args += [lam2, gain_col]
    return pl.pallas_call(
        functools.partial(_diff_body, ck, ctx is not None, heads),
        grid=(b, nq, N_HEADS // heads),
        in_specs=in_specs,
        out_specs=pl.BlockSpec((heads * HEAD_DIM, tq), lambda i, n, h: (h, i * nq + n)),
        out_shape=_sds((GROUP, rows), BF16),
        compiler_params=_cparams("parallel", "parallel", "parallel"),
        name="diff_ctx" if ctx is not None else "diff",
    )(*args)


def _ret_operands(rope, rq, rk, cos, sin):
    q = rq
    k = rk * HEAD_DIM ** -0.5
    if rope:
        q = _rope(q, cos, sin, HEAD_DIM // 4)
        k = _rope(k, cos, sin, HEAD_DIM // 4)
    return q, k


def _hgrn_gate(z, lb):
    f = lb + (1.0 - lb) * jax.nn.sigmoid(z)
    return jnp.log(jnp.maximum(f, 1e-30)), (1.0 - lb) * jax.nn.sigmoid(-z)


def _row_index(rows):
    return lax.broadcasted_iota(jnp.int32, (rows, 1), 0)


def _segment_state_step(st_ref, k, v, cum, total, bd):
    ku = k * jnp.exp(total - cum)
    st_ref[...] = st_ref[...] * jnp.exp(total) + _dot_tn(v.astype(BF16), ku.astype(BF16)) * bd


def _states_body(rkf_ref, rvf_ref, rkb_ref, rvb_ref, cf_ref, sf_ref, cb_ref, sb_ref, lg_ref,
                 zf_ref, hvf_ref, zb_ref, hvb_ref, lb_ref, tril_ref, triu_ref,
                 r0f_ref, r0b_ref, h0f_ref, h0b_ref, bd_ref,
                 ref_ref, reb_ref, hef_ref, heb_ref, rstf, rstb, hstf, hstb):
    s = pl.program_id(1)

    @pl.when(s == 0)
    def _():
        rstf[...] = r0f_ref[...]
        rstb[...] = r0b_ref[...]
        hstf[...] = h0f_ref[...]
        hstb[...] = h0b_ref[...]

    ref_ref[...] = rstf[...]
    reb_ref[...] = rstb[...]
    hef_ref[...] = hstf[...]
    heb_ref[...] = hstb[...]
    bd = bd_ref[...]
    rows = rvf_ref.shape[0]
    lff, kf = _hgrn_gate(zf_ref[...], lb_ref[0:1, :])
    lfb, kb = _hgrn_gate(zb_ref[...], lb_ref[1:2, :])
    cumf = _sum_rows(tril_ref[...], lff)
    cumb = _sum_rows(triu_ref[...], lfb)
    _segment_state_step(hstf, kf, hvf_ref[...], cumf, cumf[rows - 1:rows, :], bd)
    _segment_state_step(hstb, kb, hvb_ref[...], cumb, cumb[0:1, :], bd)
    _, kf = _ret_operands(True, rkf_ref[...], rkf_ref[...], cf_ref[...], sf_ref[...])
    _, kb = _ret_operands(True, rkb_ref[...], rkb_ref[...], cb_ref[...], sb_ref[...])
    i = _row_index(rows).astype(F32)
    lgf, lgb = lg_ref[0:1, :], lg_ref[1:2, :]
    _segment_state_step(rstf, kf, rvf_ref[...], (i + 1.0) * lgf, rows * lgf, bd)
    _segment_state_step(rstb, kb, rvb_ref[...], (rows - i) * lgb, rows * lgb, bd)


def _segment_states(proj, n_b, seq, ret_s0, hgrn_s0, lg_rows, lb_rows, consts, tables):
    ns = seq // SEG
    blk = (SEG, GROUP)

    def fwd(cb):
        return pl.BlockSpec(blk, lambda i, s, cb=cb: (i * ns + s, cb))

    def bwd(cb):
        return pl.BlockSpec(blk, lambda i, s, cb=cb: (i * ns + ns - 1 - s, cb))

    tf = pl.BlockSpec((SEG, LANES), lambda i, s: (s, 0))
    tb = pl.BlockSpec((SEG, LANES), lambda i, s: (ns - 1 - s, 0))
    st_spec = pl.BlockSpec((None, GROUP, GROUP), lambda i, s: (i, 0, 0))
    ef_spec = pl.BlockSpec((None, None, GROUP, GROUP), lambda i, s: (i, s, 0, 0))
    eb_spec = pl.BlockSpec((None, None, GROUP, GROUP), lambda i, s: (i, ns - 1 - s, 0, 0))
    e_shape = _sds((n_b, ns, GROUP, GROUP), F32)
    return pl.pallas_call(
        _states_body,
        grid=(n_b, ns),
        in_specs=[fwd(CB_RK), fwd(CB_RV), bwd(CB_RK), bwd(CB_RV), tf, tf, tb, tb, _const_spec((2, GROUP)),
                  fwd(CB_HZF), fwd(CB_HI), bwd(CB_HZB), bwd(CB_HI), _const_spec((2, GROUP)),
                  _const_spec((SEG, SEG)), _const_spec((SEG, SEG)),
                  st_spec, st_spec, st_spec, st_spec, _const_spec((GROUP, GROUP))],
        out_specs=[ef_spec, eb_spec, ef_spec, eb_spec],
        out_shape=[e_shape] * 4,
        scratch_shapes=[pltpu.VMEM((GROUP, GROUP), F32)] * 4,
        compiler_params=_cparams("parallel", "arbitrary"),
        name="seg_states",
    )(proj, proj, proj, proj, tables[0], tables[1], tables[0], tables[1], lg_rows,
      proj, proj, proj, proj, lb_rows, consts["tril_seg"], consts["triu_seg"],
      *ret_s0, *hgrn_s0, consts["bd"])


def _ret_out_body(rope, has_state, emit_state, *refs):
    rq_ref, rk_ref, rv_ref, rg_ref = refs[:4]
    refs = refs[4:]
    cos = sin = None
    if rope:
        cos, sin = refs[0][...], refs[1][...]
        refs = refs[2:]
    lgs_ref, lg_ref, gain_ref, g64_ref = refs[:4]
    refs = refs[4:]
    if has_state:
        ef_ref, eb_ref = refs[:2]
        refs = refs[2:]
    o_ref = refs[0]
    rows = rq_ref.shape[0]
    q, k = _ret_operands(rope, rq_ref[...], rk_ref[...], cos, sin)
    v = rv_ref[...]
    kb16, vb16 = k.astype(BF16), v.astype(BF16)
    s4 = _dot_nt(_stack_heads(q.astype(BF16)), kb16)
    d = (lax.broadcasted_iota(jnp.int32, (rows, rows), 0) - lax.broadcasted_iota(jnp.int32, (rows, rows), 1)).astype(F32)
    slabs = []
    for h in range(N_HEADS):
        m = (jnp.where(d >= 0, jnp.exp(jnp.maximum(d, 0.0) * lgs_ref[0, h]), 0.0)
             + jnp.where(d <= 0, jnp.exp(jnp.maximum(-d, 0.0) * lgs_ref[1, h]), 0.0))
        slabs.append((s4[h * rows:(h + 1) * rows] * m).astype(BF16))
    o = _unstack_heads(_dot(jnp.concatenate(slabs, axis=0), vb16), rows)
    i = _row_index(rows).astype(F32)
    lgf, lgb = lg_ref[0:1, :], lg_ref[1:2, :]
    if has_state:
        o = o + _dot_nt((q * jnp.exp((i + 1.0) * lgf)).astype(BF16), ef_ref[...].astype(BF16))
        o = o + _dot_nt((q * jnp.exp((rows - i) * lgb)).astype(BF16), eb_ref[...].astype(BF16))
    y = _group_rms(o, g64_ref[...], HEAD_DIM, gain_ref[...]) * _silu(rg_ref[...])
    o_ref[...] = y.astype(BF16)
    if emit_state:
        sf_ref, sb_ref = refs[1:3]
        sf_ref[...] = _dot_tn(vb16, (k * jnp.exp((rows - 1.0 - i) * lgf)).astype(BF16))
        sb_ref[...] = _dot_tn(vb16, (k * jnp.exp(i * lgb)).astype(BF16))


def _ret_out(proj, n_b, seq, lg_smem, lg_rows, gain, consts, tables, states, emit_state):
    ns = seq // SEG
    rope = tables is not None
    blk = (SEG, GROUP)

    def col(cb):
        return pl.BlockSpec(blk, lambda i, s, cb=cb: (i * ns + s, cb))

    in_specs = [col(CB_RQ), col(CB_RK), col(CB_RV), col(CB_RG)]
    args = [proj] * 4
    if rope:
        in_specs += [pl.BlockSpec((SEG, LANES), lambda i, s: (s, 0))] * 2
        args += list(tables)
    in_specs += [pl.BlockSpec(memory_space=pltpu.SMEM), _const_spec((2, GROUP)), _const_spec((1, GROUP)),
                 _const_spec((GROUP, GROUP))]
    args += [lg_smem, lg_rows, gain, consts["g64"]]
    if states is not None:
        in_specs += [pl.BlockSpec((None, None, GROUP, GROUP), lambda i, s: (i, s, 0, 0))] * 2
        args += list(states)
    out_specs = [pl.BlockSpec(blk, lambda i, s: (i * ns + s, 0))]
    out_shape = [_sds((n_b * seq, GROUP), BF16)]
    if emit_state:
        assert ns == 1
        out_specs += [pl.BlockSpec((None, GROUP, GROUP), lambda i, s: (i, 0, 0))] * 2
        out_shape += [_sds((n_b, GROUP, GROUP), F32)] * 2
    return pl.pallas_call(
        functools.partial(_ret_out_body, rope, states is not None, emit_state),
        grid=(n_b, ns),
        in_specs=in_specs, out_specs=out_specs, out_shape=out_shape,
        compiler_params=_cparams("parallel", "parallel"),
        name="ret_out_ctx" if states is not None else "ret_out",
    )(*args)


PAIR_LEVELS_MATMUL = (2, 4, 8)
PAIR_LEVELS_CUMSUM = (16, 32, 64, 128, 256)
PAIR_FOLD = 64


def _pair_level_matrix(n, forward):
    mats = []
    j = np.arange(n)[None, :]
    t = np.arange(n)[:, None]
    for g in PAIR_LEVELS_MATMUL:
        half = g // 2
        pos = t % g
        if forward:
            mid = t - pos + half - 1
            m = np.where(pos >= half, (j > mid) & (j <= t), (j > t) & (j <= mid))
        else:
            mid = t - pos + half
            m = np.where(pos < half, (j >= t) & (j < mid), (j >= mid) & (j < t))
        mats.append(m)
    return jnp.asarray(np.concatenate(mats, axis=0), BF16)


def _pair_decays(lf, cs, small_mat, forward):
    rows = lf.shape[0]
    hi = lf.astype(BF16)
    lo = (lf - hi.astype(F32)).astype(BF16)
    d = _dot(small_mat, hi) + _dot(small_mat, lo)
    out = [jnp.exp(d[i * rows:(i + 1) * rows]) for i in range(len(PAIR_LEVELS_MATMUL))]
    for g in PAIR_LEVELS_CUMSUM:
        half = g // 2
        pieces = []
        for grp in range(rows // g):
            mid = grp * g + (half - 1 if forward else half)
            pieces.append(jnp.broadcast_to(cs[mid:mid + 1, :], (g, cs.shape[1])))
        diff = cs - (pieces[0] if len(pieces) == 1 else jnp.concatenate(pieces, axis=0))
        out.append(jnp.exp(jnp.minimum(diff, -diff)))
    return out


def _hgrn_pair_weights(q, kf, kb, decays_f, decays_b):
    rows = q.shape[0]
    nfold = rows // PAIR_FOLD
    levels = PAIR_LEVELS_MATMUL + PAIR_LEVELS_CUMSUM
    pos = _row_index(rows)
    t_loc = lax.broadcasted_iota(jnp.int32, (PAIR_FOLD, GROUP), 0)
    s_loc = lax.broadcasted_iota(jnp.int32, (PAIR_FOLD, GROUP), 1) % PAIR_FOLD
    r_big = lax.broadcasted_iota(jnp.int32, (N_HEADS * rows, rows), 0) % rows
    c_big = lax.broadcasted_iota(jnp.int32, (N_HEADS * rows, rows), 1)
    folded = [None] * nfold
    stacked = None
    for g, ef, eb in reversed(list(zip(levels, decays_f, decays_b))):
        right = (pos % g) >= (g // 2)
        qs = (q * jnp.where(right, ef, eb)).astype(BF16)
        ks = (jnp.where(right, kb, kf) * jnp.where(right, eb, ef)).astype(BF16)
        if g > PAIR_FOLD:
            s = _dot_nt(_stack_heads(qs), ks)
            stacked = s if stacked is None else jnp.where((r_big // g) == (c_big // g), s, stacked)
        else:
            same = None if g == PAIR_FOLD else (t_loc // g) == (s_loc // g)
            for j in range(nfold):
                r0, r1 = j * PAIR_FOLD, (j + 1) * PAIR_FOLD
                s = _dot_nt(qs[r0:r1], _stack_heads(ks[r0:r1]))
                folded[j] = s if same is None else jnp.where(same, s, folded[j])
    stacked = jnp.where((r_big // PAIR_FOLD) == (c_big // PAIR_FOLD), 0.0, stacked)
    folded = [jnp.where(t_loc == s_loc, 0.0, f) for f in folded]
    return folded, stacked


def _hgrn_out_body(has_state, emit_state, *refs):
    (q_ref, zf_ref, zb_ref, v_ref, g_ref, lb_ref, gain_ref, g64_ref, bd_ref,
     tril_ref, triu_ref, smf_ref, smb_ref) = refs[:13]
    refs = refs[13:]
    g64 = g64_ref[...]
    if has_state:
        ef, eb = refs[0][...], refs[1][...]
        refs = refs[2:]
    o_ref = refs[0]
    q, v = q_ref[...], v_ref[...]
    rows = q.shape[0]
    lff, kf = _hgrn_gate(zf_ref[...], lb_ref[0:1, :])
    lfb, kb = _hgrn_gate(zb_ref[...], lb_ref[1:2, :])
    csf = _sum_rows(tril_ref[...], lff)
    csb = _sum_rows(triu_ref[...], lfb)
    folded, stacked = _hgrn_pair_weights(q, kf, kb, _pair_decays(lff, csf, smf_ref[...], True),
                                         _pair_decays(lfb, csb, smb_ref[...], False))
    vb = v.astype(BF16)
    o = _dot((q * (kf + kb)).astype(BF16), g64) * v
    o = o + _unstack_heads(_dot(stacked.astype(BF16), vb), rows)
    tiles = []
    for j in range(rows // PAIR_FOLD):
        r0, r1 = j * PAIR_FOLD, (j + 1) * PAIR_FOLD
        tiles.append(_dot(folded[j].astype(BF16), _stack_heads(vb[r0:r1])))
    o = o + jnp.concatenate(tiles, axis=0)
    if has_state:
        o = o + _dot_nt((q * jnp.exp(csf)).astype(BF16), ef.astype(BF16))
        o = o + _dot_nt((q * jnp.exp(csb)).astype(BF16), eb.astype(BF16))
    y = _group_rms(o, g64, HEAD_DIM, gain_ref[...]) * _silu(g_ref[...])
    o_ref[...] = y.astype(BF16)
    if emit_state:
        totf, totb = csf[rows - 1:rows, :], csb[0:1, :]
        stf = _dot_tn(vb, (kf * jnp.exp(totf - csf)).astype(BF16))
        stb = _dot_tn(vb, (kb * jnp.exp(totb - csb)).astype(BF16))
        if has_state:
            bd = bd_ref[...]
            stf = ef * jnp.exp(totf) + stf * bd
            stb = eb * jnp.exp(totb) + stb * bd
        refs[1][...] = stf
        refs[2][...] = stb


def _hgrn_out(proj, n_b, seq, lb_rows, gain, consts, states, emit_state):
    ns = seq // SEG
    blk = (SEG, GROUP)

    def col(cb):
        return pl.BlockSpec(blk, lambda i, s, cb=cb: (i * ns + s, cb))

    n_small = len(PAIR_LEVELS_MATMUL) * SEG
    in_specs = [col(CB_HQ), col(CB_HZF), col(CB_HZB), col(CB_HI), col(CB_HG),
                _const_spec((2, GROUP)), _const_spec((1, GROUP)), _const_spec((GROUP, GROUP)),
                _const_spec((GROUP, GROUP)), _const_spec((SEG, SEG)), _const_spec((SEG, SEG)),
                _const_spec((n_small, SEG)), _const_spec((n_small, SEG))]
    args = [proj] * 5 + [lb_rows, gain, consts["g64"], consts["bd"],
                         consts["tril_seg"], consts["triu_seg"], consts["pair_f"], consts["pair_b"]]
    if states is not None:
        in_specs += [pl.BlockSpec((None, None, GROUP, GROUP), lambda i, s: (i, s, 0, 0))] * 2
        args += list(states)
    out_specs = [pl.BlockSpec(blk, lambda i, s: (i * ns + s, 0))]
    out_shape = [_sds((n_b * seq, GROUP), BF16)]
    if emit_state:
        assert ns == 1
        out_specs += [pl.BlockSpec((None, GROUP, GROUP), lambda i, s: (i, 0, 0))] * 2
        out_shape += [_sds((n_b, GROUP, GROUP), F32)] * 2
    return pl.pallas_call(
        functools.partial(_hgrn_out_body, states is not None, emit_state),
        grid=(n_b, ns),
        in_specs=in_specs, out_specs=out_specs, out_shape=out_shape,
        compiler_params=_cparams("parallel", "parallel"),
        name="hgrn_out_ctx" if states is not None else "hgrn_out",
    )(*args)


def _ffn_body(x_ref, oa_ref, ob_ref, oc_ref, od_ref, wo_ref, g1_ref, n2_ref, sc_ref, sh_ref, g2_ref,
              wg_ref, wu_ref, wd_ref, y_ref, x1_ref, h_ref, acc_ref):
    j = pl.program_id(1)

    @pl.when(j == 0)
    def _():
        mix = _dot(oa_ref[...], wo_ref[0:GROUP, :])
        mix += _dot_tn(ob_ref[...], wo_ref[GROUP:2 * GROUP, :])
        mix += _dot_tn(oc_ref[...], wo_ref[2 * GROUP:3 * GROUP, :])
        mix += _dot(od_ref[...], wo_ref[3 * GROUP:4 * GROUP, :])
        x1 = x_ref[...] + g1_ref[...] * mix
        x1_ref[...] = x1
        h = x1 * lax.rsqrt(jnp.mean(x1 * x1, axis=-1, keepdims=True) + EPS) * n2_ref[...]
        h_ref[...] = (h * (1.0 + sc_ref[...]) + sh_ref[...]).astype(BF16)
        acc_ref[...] = jnp.zeros_like(acc_ref)

    h = h_ref[...]
    a = _silu(_dot(h, wg_ref[...])) * _dot(h, wu_ref[...])
    acc_ref[...] += _dot(a.astype(BF16), wd_ref[...])

    @pl.when(j == pl.num_programs(1) - 1)
    def _():
        y_ref[...] = x1_ref[...] + g2_ref[...] * acc_ref[...]


def _outproj_ffn(x, mixes, w_out, g1, n2, sc2, sh2, g2, w_in, w_dn, rows_per_mod, mod_row0, tm=1024, th=256):
    t, d = x.shape
    hid = w_dn.shape[0]
    nh = hid // th
    mod_map = _mod_row_map(tm, rows_per_mod, mod_row0)
    mod2 = lambda i, j: mod_map(i)
    row = lambda i, j: (i, 0)
    mix_spec = pl.BlockSpec((tm, GROUP), row)
    mix_t_spec = pl.BlockSpec((GROUP, tm), lambda i, j: (0, i))
    mod_spec = pl.BlockSpec((None, 1, d), mod2)
    return pl.pallas_call(
        _ffn_body,
        grid=(t // tm, nh),
        in_specs=[pl.BlockSpec((tm, d), row), mix_spec, mix_t_spec, mix_t_spec, mix_spec]
                 + [_const_spec((d, d)), mod_spec, _const_spec((1, d)), mod_spec, mod_spec, mod_spec,
                    pl.BlockSpec((d, th), lambda i, j: (0, j)),
                    pl.BlockSpec((d, th), lambda i, j: (0, nh + j)),
                    pl.BlockSpec((th, d), lambda i, j: (j, 0))],
        out_specs=pl.BlockSpec((tm, d), row),
        out_shape=_sds((t, d), F32),
        scratch_shapes=[pltpu.VMEM((tm, d), F32), pltpu.VMEM((tm, d), BF16), pltpu.VMEM((tm, d), F32)],
        compiler_params=_cparams("parallel", "arbitrary"),
        name="outproj_ffn",
    )(x, *mixes, w_out, g1, n2, sc2, sh2, g2, w_in, w_in, w_dn)


def _lane_rows(per_head):
    return jnp.repeat(per_head.astype(F32), HEAD_DIM, axis=1)


def _states_to_lanes(s):
    b = s.shape[0]
    eye = jnp.eye(N_HEADS, dtype=F32)
    st = jnp.swapaxes(s.astype(F32), -1, -2)[:, :, :, :, None, :] * eye[None, None, :, None, :, None]
    st = st.reshape(b, 2, GROUP, GROUP)
    return st[:, 0], st[:, 1]


def _with_ones_rows(vt):
    ones = jnp.ones(vt.shape[:-2] + (V_ROWS - HEAD_DIM, vt.shape[-1]), vt.dtype)
    return jnp.concatenate([vt, ones], axis=-2)


def _lanes_to_states(sf, sb):
    def diag(x):
        b = x.shape[0]
        x = x.reshape(b, N_HEADS, HEAD_DIM, N_HEADS, HEAD_DIM)
        return jnp.stack([x[:, h, :, h, :] for h in range(N_HEADS)], axis=1).swapaxes(-1, -2)
    return jnp.stack([diag(sf), diag(sb)], axis=1)


def kernel(x_prompt, x_sample, state_ret, cache_win_k, cache_win_v, cache_diff_k, cache_diff_v, state_hgrn, c, c_ctx, norm1_g, norm2_g, w_ada, b_ada, w_in, ret_decay, ret_norm_g, win_q_norm, win_k_norm, win_sink, diff_q_norm, diff_k_norm, diff_lambda, diff_norm_g, hgrn_lb_logits, hgrn_norm_g, w_out, w_ffn_in, w_ffn_out):
    n_p, seq_p, d = x_prompt.shape
    n_s, seq_s, _ = x_sample.shape
    depth = w_in.shape[0]
    t_s, t_p = n_s * seq_s, n_p * seq_p
    assert seq_p == SEG and seq_s % SEG == 0 and d == N_HEADS * GROUP

    tril_seg, triu_seg = _cumsum_matrices(SEG)
    consts = dict(g64=_group_matrix(GROUP, HEAD_DIM), g32=_group_matrix(GROUP, DIFF_DIM),
                  bd=_group_matrix(GROUP, HEAD_DIM).astype(F32), dupk=_kv_dup_matrix(),
                  tril_seg=tril_seg, triu_seg=triu_seg,
                  pair_f=_pair_level_matrix(SEG, True), pair_b=_pair_level_matrix(SEG, False))
    tab64 = _rope_tables(seq_s, HEAD_DIM)
    tab32 = _rope_tables(seq_s, DIFF_DIM)

    n_rows = -(-(n_s + 1) // 8) * 8
    cond = jnp.zeros((n_rows, d), F32).at[:n_s].set(c).at[n_s].set(c_ctx)
    mod = _adaln(cond, w_ada, b_ada).reshape(depth, n_rows, 6, 1, d)

    lb_p = jax.nn.softmax(hgrn_lb_logits.astype(F32), axis=0)
    lb_all = jnp.cumsum(lb_p, axis=0) - lb_p
    log_gamma = -jnp.exp(ret_decay.astype(F32))

    xs = x_sample.reshape(t_s, d)
    xp = x_prompt.reshape(t_p, d)
    new_ret, new_wk, new_wv, new_dk, new_dv, new_hg = [], [], [], [], [], []
    for l in range(depth):
        lam_init = 0.8 - 0.6 * math.exp(-0.3 * l)
        sh1, sc1, g1, sh2, sc2, g2 = [mod[l, :, i] for i in range(6)]
        w_in_l, w_out_l = w_in[l].astype(BF16), w_out[l].astype(BF16)
        w_up_l, w_dn_l = w_ffn_in[l].astype(BF16), w_ffn_out[l].astype(BF16)
        n1, n2 = norm1_g[l][None], norm2_g[l][None]
        gains = (jnp.tile(win_q_norm[l], N_HEADS)[None], jnp.tile(win_k_norm[l], KV_HEADS)[None],
                 jnp.tile(diff_q_norm[l], 2 * N_HEADS)[None], jnp.tile(diff_k_norm[l], 2 * N_HEADS)[None])
        sink = win_sink[l].astype(F32)
        lq1, lk1, lq2, lk2 = diff_lambda[l].astype(F32)
        lam = jnp.exp(jnp.sum(lq1 * lk1)) - jnp.exp(jnp.sum(lq2 * lk2)) + lam_init
        lam2 = jnp.stack([lam, jnp.asarray(1.0 - lam_init, F32)]).reshape(1, 2)
        dgain = diff_norm_g[l].astype(F32)[:, None]
        lg = log_gamma[l]
        lg_rows = _lane_rows(lg)
        rgain = ret_norm_g[l][None]
        lb_rows = lb_all[l]
        hgain = hgrn_norm_g[l][None]

        proj = _inproj(xs, n1, sc1, sh1, w_in_l, seq_s, 0)
        wq, wk, wvt, dq, dk, dvt = _prep(proj, gains, consts, tab64 + tab32, False)
        ctx_wk = jnp.repeat(cache_win_k[:, l].transpose(0, 2, 1, 3), N_HEADS // KV_HEADS, axis=2)
        ctx_wk = ctx_wk.reshape(n_s, -1, GROUP).astype(BF16)
        ctx_wvt = _with_ones_rows(cache_win_v[:, l].swapaxes(-1, -2).astype(BF16))
        o_win = _win_sample(wq, wk, wvt, ctx_wk, ctx_wvt, jnp.repeat(sink * LOG2_E, WIN_BLOCK)[None])
        ctx_dk = cache_diff_k[:, l].transpose(0, 3, 1, 2, 4).reshape(n_s, -1, GROUP).astype(BF16)
        ctx_dvt = _with_ones_rows(cache_diff_v[:, l].swapaxes(-1, -2).astype(BF16))
        o_diff = _diff_attention(dq, dk, dvt, (ctx_dk, ctx_dvt), seq_s, lam2, dgain, 1)
        entry = _segment_states(proj, n_s, seq_s, _states_to_lanes(state_ret[:, l]),
                                _states_to_lanes(state_hgrn[:, l]), lg_rows, lb_rows, consts, tab64)
        (o_ret,) = _ret_out(proj, n_s, seq_s, lg, lg_rows, rgain, consts, tab64, entry[:2], False)
        (o_h,) = _hgrn_out(proj, n_s, seq_s, lb_rows, hgain, consts, entry[2:], False)
        xs = _outproj_ffn(xs, (o_ret, o_win, o_diff, o_h), w_out_l, g1, n2, sc2, sh2, g2, w_up_l, w_dn_l, seq_s, 0)

        proj = _inproj(xp, n1, sc1, sh1, w_in_l, t_p, n_s)
        wq, wk, wvt, dq, dk, dvt, wk_c, wv_c, dk_c, dv_c = _prep(proj, gains, consts, None, True, tm=seq_p)
        o_win = _win_prompt(wq, wk, wvt, seq_p, jnp.repeat(sink * LOG2_E, seq_p)[None])
        o_diff = _diff_attention(dq, dk, dvt, None, seq_p, lam2, dgain, N_HEADS)
        o_ret, rsf, rsb = _ret_out(proj, n_p, seq_p, lg, lg_rows, rgain, consts, None, None, True)
        o_h, hsf, hsb = _hgrn_out(proj, n_p, seq_p, lb_rows, hgain, consts, None, True)
        xp = _outproj_ffn(xp, (o_ret, o_win, o_diff, o_h), w_out_l, g1, n2, sc2, sh2, g2, w_up_l, w_dn_l, t_p, n_s)

        new_ret.append(_lanes_to_states(rsf, rsb))
        new_hg.append(_lanes_to_states(hsf, hsb))
        new_wk.append(wk_c)
        new_wv.append(wv_c)
        new_dk.append(dk_c)
        new_dv.append(dv_c)

    return (xp.reshape(n_p, seq_p, d), xs.reshape(n_s, seq_s, d),
            jnp.stack(new_ret, axis=1), jnp.stack(new_wk, axis=1), jnp.stack(new_wv, axis=1),
            jnp.stack(new_dk, axis=1), jnp.stack(new_dv, axis=1), jnp.stack(new_hg, axis=1))
```

```python
import functools
import math

import numpy as np
import jax
import jax.numpy as jnp
from jax import lax
from jax.experimental import pallas as pl
from jax.experimental.pallas import tpu as pltpu

F32 = jnp.float32
BF16 = jnp.bfloat16

GROUP = 256
HEAD_DIM = 64
N_HEADS = 4
KV_HEADS = 2
DIFF_DIM = 32
WINDOW = 128
WIN_BLOCK = 128
GRID_W = 64
ROPE_BASE = 10000.0
EPS = 1e-6
MASK_VALUE = -1e30
SEG = 256
LANES = 128
BF16_SUBLANES = 16
LOG2_E = math.log2(math.e)
V_ROWS = HEAD_DIM + BF16_SUBLANES
DIFF_LOOKAHEAD = 2
V7X_VMEM_LIMIT_BYTES = 56 * 1024 * 1024

(CB_RQ, CB_RK, CB_RV, CB_RG, CB_WQ, CB_WKV, CB_DQ, CB_DK, CB_DV,
 CB_HQ, CB_HZF, CB_HZB, CB_HI, CB_HG) = range(14)

_NT = (((1,), (1,)), ((), ()))
_TN = (((0,), (0,)), ((), ()))


def _sds(shape, dtype):
    return jax.ShapeDtypeStruct(shape, dtype)


def _cparams(*sem):
    return pltpu.CompilerParams(dimension_semantics=sem, vmem_limit_bytes=V7X_VMEM_LIMIT_BYTES)


def _const_spec(shape):
    return pl.BlockSpec(shape, lambda *_: (0,) * len(shape))


def _dot(a, b):
    return jnp.dot(a, b, preferred_element_type=F32)


def _dot_nt(a, b):
    return lax.dot_general(a, b, _NT, preferred_element_type=F32)


def _dot_tn(a, b):
    return lax.dot_general(a, b, _TN, preferred_element_type=F32)


def _silu(x):
    return x * jax.nn.sigmoid(x)


def _group_matrix(width, gsize):
    i = np.arange(width)
    return jnp.asarray((i[:, None] // gsize) == (i[None, :] // gsize), BF16)


def _cumsum_matrices(n):
    r, c = np.arange(n)[:, None], np.arange(n)[None, :]
    return jnp.asarray(c <= r, BF16), jnp.asarray(c >= r, BF16)


def _kv_dup_matrix():
    d = np.zeros((KV_HEADS * HEAD_DIM, GROUP), np.float32)
    for h in range(N_HEADS):
        kv = h // (N_HEADS // KV_HEADS)
        for j in range(HEAD_DIM):
            d[kv * HEAD_DIM + j, h * HEAD_DIM + j] = 1.0
    return jnp.asarray(d, BF16)


def _rope_tables(n_tokens, head_dim):
    d = head_dim // 2
    half = d // 2
    inv = ROPE_BASE ** (-jnp.arange(half, dtype=F32) / half)
    t = jnp.arange(n_tokens)
    row = (t // GRID_W).astype(F32)
    col = (t % GRID_W).astype(F32)
    j = np.arange(LANES) % head_dim
    w = j % d
    use_row = jnp.asarray((j // d) == 0)
    pos = jnp.where(use_row[None, :], row[:, None], col[:, None])
    ang = pos * inv[w % half][None, :]
    sign = jnp.asarray(np.where(w >= half, 1.0, -1.0), F32)
    return jnp.cos(ang), jnp.sin(ang) * sign[None, :]


def _group_rms(x, g_mat, gsize, gain):
    ss = _dot((x * x).astype(BF16), g_mat)
    return x * lax.rsqrt(ss * (1.0 / gsize) + EPS) * gain


def _rope(y, cos, sin, half):
    lane = lax.broadcasted_iota(jnp.int32, (1, LANES), 1)
    second = (lane % (2 * half)) >= half
    outs = []
    for p in range(y.shape[1] // LANES):
        z = y[:, p * LANES:(p + 1) * LANES]
        partner = jnp.where(second, pltpu.roll(z, half, 1), pltpu.roll(z, LANES - half, 1))
        outs.append(z * cos + partner * sin)
    return outs[0] if len(outs) == 1 else jnp.concatenate(outs, axis=1)


def _head_of_lane(width=GROUP):
    return lax.broadcasted_iota(jnp.int32, (1, width), 1) // HEAD_DIM


def _stack_heads(q, n_heads=N_HEADS):
    head = _head_of_lane()
    zero = jnp.zeros_like(q)
    return jnp.concatenate([jnp.where(head == h, q, zero) for h in range(n_heads)], axis=0)


def _unstack_heads(o4, rows):
    head = _head_of_lane()
    out = jnp.where(head == 0, o4[:rows], 0.0)
    for h in range(1, N_HEADS):
        out = out + jnp.where(head == h, o4[h * rows:(h + 1) * rows], 0.0)
    return out


def _head_blocks(s):
    head = _head_of_lane()
    out = jnp.where(head == 0, s[:HEAD_DIM], 0.0)
    for h in range(1, N_HEADS):
        out = out + jnp.where(head == h, s[h * HEAD_DIM:(h + 1) * HEAD_DIM], 0.0)
    return out


def _split3(x):
    hi = x.astype(BF16)
    r = x - hi.astype(F32)
    mid = r.astype(BF16)
    lo = (r - mid.astype(F32)).astype(BF16)
    return hi, mid, lo


def _sum_rows(m01, x):
    hi, mid, lo = _split3(x)
    return _dot(m01, hi) + _dot(m01, mid) + _dot(m01, lo)


def _adaln_body(c_ref, w_ref, b_ref, o_ref):
    c = c_ref[...]
    o_ref[...] = jnp.dot(_silu(c), w_ref[...], preferred_element_type=F32,
                         precision=lax.Precision.HIGHEST) + b_ref[...]


def _adaln(cond, w_ada, b_ada):
    depth, d, n = w_ada.shape
    rows = cond.shape[0]
    tn = n // 4
    return pl.pallas_call(
        _adaln_body,
        grid=(depth, n // tn),
        in_specs=[pl.BlockSpec((rows, d), lambda l, j: (0, 0)),
                  pl.BlockSpec((None, d, tn), lambda l, j: (l, 0, j)),
                  pl.BlockSpec((None, 1, tn), lambda l, j: (l, 0, j))],
        out_specs=pl.BlockSpec((None, rows, tn), lambda l, j: (l, 0, j)),
        out_shape=_sds((depth, rows, n), F32),
        compiler_params=_cparams("parallel", "parallel"),
        name="adaln",
    )(cond, w_ada, b_ada.reshape(depth, 1, n))


def _inproj_body(x_ref, g_ref, sc_ref, sh_ref, w_ref, o_ref):
    x = x_ref[...]
    h = x * lax.rsqrt(jnp.mean(x * x, axis=-1, keepdims=True) + EPS) * g_ref[...]
    h = h * (1.0 + sc_ref[...]) + sh_ref[...]
    o_ref[...] = _dot(h.astype(BF16), w_ref[...])


def _mod_row_map(tm, rows_per_mod, mod_row0):
    return lambda i: (mod_row0 + (i * tm) // rows_per_mod, 0, 0)


def _inproj(x, g, sc, sh, w, rows_per_mod, mod_row0, tm=512):
    t, d = x.shape
    n = w.shape[1]
    mod_map = _mod_row_map(tm, rows_per_mod, mod_row0)
    return pl.pallas_call(
        _inproj_body,
        grid=(t // tm,),
        in_specs=[pl.BlockSpec((tm, d), lambda i: (i, 0)),
                  _const_spec((1, d)),
                  pl.BlockSpec((None, 1, d), mod_map),
                  pl.BlockSpec((None, 1, d), mod_map),
                  _const_spec((d, n))],
        out_specs=pl.BlockSpec((tm, n), lambda i: (i, 0)),
        out_shape=_sds((t, n), F32),
        compiler_params=_cparams("parallel"),
        name="inproj",
    )(x, g, sc, sh, w)


def _prep_body(rope, emit_f32, *refs):
    (wq_ref, wkv_ref, dq_ref, dk_ref, dv_ref, qn_ref, kn_ref, dqn_ref, dkn_ref, g64_ref, g32_ref,
     dup_ref) = refs[:12]
    refs = refs[12:]
    if rope:
        c64_ref, s64_ref, c32_ref, s32_ref = refs[:4]
        refs = refs[4:]
    wq_o, wk_o, wvt_o, dq_o, dk_o, dvt_o = refs[:6]
    g64 = g64_ref[...]
    g32 = g32_ref[...]
    tm = dv_ref.shape[0]

    wq = _group_rms(wq_ref[...], g64, HEAD_DIM, qn_ref[...])
    wkv = wkv_ref[...]
    wk = _group_rms(wkv[:, :LANES], g64[:LANES, :LANES], HEAD_DIM, kn_ref[...])
    dq = _group_rms(dq_ref[...], g32, DIFF_DIM, dqn_ref[...])
    dk = _group_rms(dk_ref[...], g32, DIFF_DIM, dkn_ref[...])
    if rope:
        c64, s64, c32, s32 = c64_ref[...], s64_ref[...], c32_ref[...], s32_ref[...]
        wq = _rope(wq, c64, s64, HEAD_DIM // 4)
        wk = _rope(wk, c64, s64, HEAD_DIM // 4)
        dq = _rope(dq, c32, s32, DIFF_DIM // 4)
        dk = _rope(dk, c32, s32, DIFF_DIM // 4)
    wq_o[...] = (wq * (HEAD_DIM ** -0.5 * LOG2_E)).astype(BF16)
    wk_o[...] = _dot(wk.astype(BF16), dup_ref[...]).astype(BF16)
    wvt_o[:, :HEAD_DIM, :] = wkv[:, LANES:].T.reshape(KV_HEADS, HEAD_DIM, tm).astype(BF16)
    wvt_o[:, HEAD_DIM:, :] = jnp.ones((KV_HEADS, V_ROWS - HEAD_DIM, tm), BF16)
    dq_o[...] = (dq * (DIFF_DIM ** -0.5 * LOG2_E)).astype(BF16)
    dk_o[...] = dk.astype(BF16)
    dvt_o[:, :HEAD_DIM, :] = dv_ref[...].T.reshape(N_HEADS, HEAD_DIM, tm).astype(BF16)
    dvt_o[:, HEAD_DIM:, :] = jnp.ones((N_HEADS, V_ROWS - HEAD_DIM, tm), BF16)
    if emit_f32:
        wk_c, wv_c, dk_c, dv_c = refs[6:10]
        dv = dv_ref[...]
        for kv in range(KV_HEADS):
            wk_c[kv] = wk[:, kv * HEAD_DIM:(kv + 1) * HEAD_DIM]
            wv_c[kv] = wkv[:, LANES + kv * HEAD_DIM:LANES + (kv + 1) * HEAD_DIM]
        for h in range(N_HEADS):
            dv_c[h] = dv[:, h * HEAD_DIM:(h + 1) * HEAD_DIM]
            for c in range(2):
                lane0 = h * HEAD_DIM + c * DIFF_DIM
                dk_c[h, c] = dk[:, lane0:lane0 + DIFF_DIM]


def _prep(proj, gains, consts, tables, emit_f32, tm=256):
    rope = tables is not None
    rows = proj.shape[0]

    def col(cb):
        return pl.BlockSpec((tm, GROUP), lambda i, cb=cb: (i, cb))

    def vt(heads):
        return pl.BlockSpec((heads, V_ROWS, tm), lambda i: (0, 0, i))

    in_specs = [col(CB_WQ), col(CB_WKV), col(CB_DQ), col(CB_DK), col(CB_DV),
                _const_spec((1, GROUP)), _const_spec((1, LANES)), _const_spec((1, GROUP)), _const_spec((1, GROUP)),
                _const_spec((GROUP, GROUP)), _const_spec((GROUP, GROUP)), _const_spec((LANES, GROUP))]
    args = [proj] * 5 + list(gains) + [consts["g64"], consts["g32"], consts["dupk"]]
    if rope:
        seq = tables[0].shape[0]
        nper = seq // tm
        in_specs += [pl.BlockSpec((tm, LANES), lambda i: (i % nper, 0))] * 4
        args += list(tables)
    out = pl.BlockSpec((tm, GROUP), lambda i: (i, 0))
    tok = _sds((rows, GROUP), BF16)
    out_specs = [out, out, vt(KV_HEADS), out, out, vt(N_HEADS)]
    out_shape = [tok, tok, _sds((KV_HEADS, V_ROWS, rows), BF16), tok, tok, _sds((N_HEADS, V_ROWS, rows), BF16)]
    if emit_f32:
        nb = rows // tm
        out_specs += [pl.BlockSpec((None, KV_HEADS, tm, HEAD_DIM), lambda i: (i, 0, 0, 0)),
                      pl.BlockSpec((None, KV_HEADS, tm, HEAD_DIM), lambda i: (i, 0, 0, 0)),
                      pl.BlockSpec((None, N_HEADS, 2, tm, DIFF_DIM), lambda i: (i, 0, 0, 0, 0)),
                      pl.BlockSpec((None, N_HEADS, tm, HEAD_DIM), lambda i: (i, 0, 0, 0))]
        out_shape += [_sds((nb, KV_HEADS, tm, HEAD_DIM), F32), _sds((nb, KV_HEADS, tm, HEAD_DIM), F32),
                      _sds((nb, N_HEADS, 2, tm, DIFF_DIM), F32), _sds((nb, N_HEADS, tm, HEAD_DIM), F32)]
    return pl.pallas_call(
        functools.partial(_prep_body, rope, emit_f32),
        grid=(rows // tm,),
        in_specs=in_specs, out_specs=out_specs, out_shape=out_shape,
        compiler_params=_cparams("parallel"),
        name="prep_rope" if rope else "prep",
    )(*args)


def _win_body(banded, *refs):
    if banded:
        (q_ref, kp_ref, kc_ref, kn_ref, vp_ref, vc_ref, vn_ref, ck_ref, cvt_ref, sink_ref, o_ref) = refs
        k_refs, v_refs = (kp_ref, kc_ref, kn_ref), (vp_ref, vc_ref, vn_ref)
    else:
        q_ref, kc_ref, vc_ref, sink_ref, o_ref = refs
        k_refs, v_refs = (kc_ref,), (vc_ref,)
    tq = q_ref.shape[0]
    q4 = _stack_heads(q_ref[...])
    k_loc = k_refs[0][...] if len(k_refs) == 1 else jnp.concatenate([r[...] for r in k_refs], axis=0)
    s_loc = _dot_nt(k_loc, q4)
    sink = sink_ref[...]
    if banded:
        n = pl.program_id(1)
        seq = pl.num_programs(1) * WIN_BLOCK
        t = n * WIN_BLOCK + lax.broadcasted_iota(jnp.int32, s_loc.shape, 1) % WIN_BLOCK
        kpos = (n - 1) * WIN_BLOCK + lax.broadcasted_iota(jnp.int32, s_loc.shape, 0)
        valid = (kpos >= 0) & (kpos < seq) & (jnp.abs(t - kpos) <= WINDOW)
        s_loc = jnp.where(valid, s_loc, MASK_VALUE)
        s_ctx = _dot_nt(ck_ref[...], q4)
        m = jnp.maximum(jnp.maximum(s_loc.max(axis=0, keepdims=True), s_ctx.max(axis=0, keepdims=True)), sink)
        p_ctx = jnp.exp2(s_ctx - m).astype(BF16)
    else:
        m = jnp.maximum(s_loc.max(axis=0, keepdims=True), sink)
    p_loc = jnp.exp2(s_loc - m).astype(BF16)
    p_sink = jnp.exp2(sink - m)
    outs = []
    for h in range(N_HEADS):
        kv = h // (N_HEADS // KV_HEADS)
        c0, c1 = h * tq, (h + 1) * tq
        vt = v_refs[0][kv] if len(v_refs) == 1 else jnp.concatenate([r[kv] for r in v_refs], axis=1)
        acc = _dot(vt, p_loc[:, c0:c1])
        if banded:
            acc = acc + _dot(cvt_ref[kv], p_ctx[:, c0:c1])
        outs.append(acc[:HEAD_DIM] / (acc[HEAD_DIM:HEAD_DIM + 1] + p_sink[:, c0:c1]))
    o_ref[...] = jnp.concatenate(outs, axis=0).astype(BF16)


def _win_sample(wq, wk, wvt, ctx_k, ctx_vt, sink_row):
    b, p, _ = ctx_k.shape
    rows = wq.shape[0]
    nq = rows // b // WIN_BLOCK
    blk = (WIN_BLOCK, GROUP)
    vblk = (KV_HEADS, V_ROWS, WIN_BLOCK)
    prev = lambda i, n: i * nq + jnp.maximum(n - 1, 0)
    cur = lambda i, n: i * nq + n
    nxt = lambda i, n: i * nq + jnp.minimum(n + 1, nq - 1)
    return pl.pallas_call(
        functools.partial(_win_body, True),
        grid=(b, nq),
        in_specs=[pl.BlockSpec(blk, lambda i, n: (cur(i, n), 0)),
                  pl.BlockSpec(blk, lambda i, n: (prev(i, n), 0)),
                  pl.BlockSpec(blk, lambda i, n: (cur(i, n), 0)),
                  pl.BlockSpec(blk, lambda i, n: (nxt(i, n), 0)),
                  pl.BlockSpec(vblk, lambda i, n: (0, 0, prev(i, n))),
                  pl.BlockSpec(vblk, lambda i, n: (0, 0, cur(i, n))),
                  pl.BlockSpec(vblk, lambda i, n: (0, 0, nxt(i, n))),
                  pl.BlockSpec((None, p, GROUP), lambda i, n: (i, 0, 0)),
                  pl.BlockSpec((None, KV_HEADS, V_ROWS, p), lambda i, n: (i, 0, 0, 0)),
                  _const_spec((1, N_HEADS * WIN_BLOCK))],
        out_specs=pl.BlockSpec((GROUP, WIN_BLOCK), lambda i, n: (0, cur(i, n))),
        out_shape=_sds((GROUP, rows), BF16),
        compiler_params=_cparams("parallel", "parallel"),
        name="win_sample",
    )(wq, wk, wk, wk, wvt, wvt, wvt, ctx_k, ctx_vt, sink_row)


def _win_prompt(wq, wk, wvt, seq, sink_row):
    rows = wq.shape[0]
    blk = (seq, GROUP)
    return pl.pallas_call(
        functools.partial(_win_body, False),
        grid=(rows // seq,),
        in_specs=[pl.BlockSpec(blk, lambda i: (i, 0)), pl.BlockSpec(blk, lambda i: (i, 0)),
                  pl.BlockSpec((KV_HEADS, V_ROWS, seq), lambda i: (0, 0, i)),
                  _const_spec((1, N_HEADS * seq))],
        out_specs=pl.BlockSpec((GROUP, seq), lambda i: (0, i)),
        out_shape=_sds((GROUP, rows), BF16),
        compiler_params=_cparams("parallel"),
        name="win_prompt",
    )(wq, wk, wvt, sink_row)


def _diff_body(ck, has_ctx, heads, *refs):
    if has_ctx:
        q_ref, k_ref, vt_ref, ck_ref, cvt_ref, lam_ref, gain_ref, o_ref = refs
    else:
        q_ref, k_ref, vt_ref, lam_ref, gain_ref, o_ref = refs
    for j in range(heads):
        sources = [(k_ref, vt_ref.at[j])] + ([(ck_ref, cvt_ref.at[j])] if has_ctx else [])
        rows = slice(j * HEAD_DIM, (j + 1) * HEAD_DIM)
        o_ref[rows, :] = _diff_head(ck, pl.program_id(2) * heads + j, q_ref[...], sources,
                                    lam_ref, gain_ref[rows, :])


def _diff_head(ck, h, q, sources, lam_ref, gain):
    tq = q.shape[0]
    sub = lax.broadcasted_iota(jnp.int32, (1, GROUP), 1) // DIFF_DIM
    zero = jnp.zeros_like(q)
    q2 = jnp.concatenate([jnp.where(sub == 2 * h, q, zero), jnp.where(sub == 2 * h + 1, q, zero)], axis=0)
    chunks = []
    for kr, vr in sources:
        n_keys = kr.shape[0]
        step = min(ck, n_keys)
        chunks += [(kr, vr, c0, step) for c0 in range(0, n_keys, step)]

    def scores(i):
        kr, _, c0, step = chunks[i]
        return _dot_nt(kr[c0:c0 + step, :], q2)

    pending = [scores(i) for i in range(min(DIFF_LOOKAHEAD, len(chunks)))]
    m = acc = None
    for i, (_, vr, c0, step) in enumerate(chunks):
        s = pending.pop(0)
        if i + DIFF_LOOKAHEAD < len(chunks):
            pending.append(scores(i + DIFF_LOOKAHEAD))
        vt = vr[:, c0:c0 + step]
        mc = s.max(axis=0, keepdims=True)
        if m is None:
            m = mc
            acc = _dot(vt, jnp.exp2(s - m).astype(BF16))
        else:
            m_new = jnp.maximum(m, mc)
            acc = jnp.exp2(m - m_new) * acc + _dot(vt, jnp.exp2(s - m_new).astype(BF16))
            m = m_new
    o2 = acc[:HEAD_DIM] / acc[HEAD_DIM:HEAD_DIM + 1]
    od = o2[:, :tq] - lam_ref[0, 0] * o2[:, tq:]
    y = od * lax.rsqrt(jnp.mean(od * od, axis=0, keepdims=True) + EPS) * gain * lam_ref[0, 1]
    return y.astype(BF16)


def _diff_attention(dq, dk, dvt, ctx, seq, lam2, gain_col, heads, tq=512, ck=512):
    rows = dq.shape[0]
    b = rows // seq
    tq = min(tq, seq)
    nq = seq // tq
    in_specs = [pl.BlockSpec((tq, GROUP), lambda i, n, h: (i * nq + n, 0)),
                pl.BlockSpec((seq, GROUP), lambda i, n, h: (i, 0)),
                pl.BlockSpec((heads, V_ROWS, seq), lambda i, n, h: (h, 0, i))]
    args = [dq, dk, dvt]
    if ctx is not None:
        p = ctx[0].shape[1]
        in_specs += [pl.BlockSpec((None, p, GROUP), lambda i, n, h: (i, 0, 0)),
                     pl.BlockSpec((None, heads, V_ROWS, p), lambda i, n, h: (i, h, 0, 0))]
        args += list(ctx)
    in_specs += [pl.BlockSpec(memory_space=pltpu.SMEM),
                 pl.BlockSpec((heads * HEAD_DIM, 1), lambda i, n, h: (h, 0))]
    args += [lam2, gain_col]
    return pl.pallas_call(
        functools.partial(_diff_body, ck, ctx is not None, heads),
        grid=(b, nq, N_HEADS // heads),
        in_specs=in_specs,
        out_specs=pl.BlockSpec((heads * HEAD_DIM, tq), lambda i, n, h: (h, i * nq + n)),
        out_shape=_sds((GROUP, rows), BF16),
        compiler_params=_cparams("parallel", "parallel", "parallel"),
        name="diff_ctx" if ctx is not None else "diff",
    )(*args)


def _ret_operands(rope, rq, rk, cos, sin):
    q = rq
    k = rk * HEAD_DIM ** -0.5
    if rope:
        q = _rope(q, cos, sin, HEAD_DIM // 4)
        k = _rope(k, cos, sin, HEAD_DIM // 4)
    return q, k


def _hgrn_gate(z, lb):
    f = lb + (1.0 - lb) * jax.nn.sigmoid(z)
    return jnp.log(jnp.maximum(f, 1e-30)), (1.0 - lb) * jax.nn.sigmoid(-z)


def _row_index(rows):
    return lax.broadcasted_iota(jnp.int32, (rows, 1), 0)


def _segment_state_step(st_ref, k, v, cum, total, bd):
    ku = k * jnp.exp(total - cum)
    st_ref[...] = st_ref[...] * jnp.exp(total) + _dot_tn(v.astype(BF16), ku.astype(BF16)) * bd


def _states_body(rkf_ref, rvf_ref, rkb_ref, rvb_ref, cf_ref, sf_ref, cb_ref, sb_ref, lg_ref,
                 zf_ref, hvf_ref, zb_ref, hvb_ref, lb_ref, tril_ref, triu_ref,
                 r0f_ref, r0b_ref, h0f_ref, h0b_ref, bd_ref,
                 ref_ref, reb_ref, hef_ref, heb_ref, rstf, rstb, hstf, hstb):
    s = pl.program_id(1)

    @pl.when(s == 0)
    def _():
        rstf[...] = r0f_ref[...]
        rstb[...] = r0b_ref[...]
        hstf[...] = h0f_ref[...]
        hstb[...] = h0b_ref[...]

    ref_ref[...] = rstf[...]
    reb_ref[...] = rstb[...]
    hef_ref[...] = hstf[...]
    heb_ref[...] = hstb[...]
    bd = bd_ref[...]
    rows = rvf_ref.shape[0]
    lff, kf = _hgrn_gate(zf_ref[...], lb_ref[0:1, :])
    lfb, kb = _hgrn_gate(zb_ref[...], lb_ref[1:2, :])
    cumf = _sum_rows(tril_ref[...], lff)
    cumb = _sum_rows(triu_ref[...], lfb)
    _segment_state_step(hstf, kf, hvf_ref[...], cumf, cumf[rows - 1:rows, :], bd)
    _segment_state_step(hstb, kb, hvb_ref[...], cumb, cumb[0:1, :], bd)
    _, kf = _ret_operands(True, rkf_ref[...], rkf_ref[...], cf_ref[...], sf_ref[...])
    _, kb = _ret_operands(True, rkb_ref[...], rkb_ref[...], cb_ref[...], sb_ref[...])
    i = _row_index(rows).astype(F32)
    lgf, lgb = lg_ref[0:1, :], lg_ref[1:2, :]
    _segment_state_step(rstf, kf, rvf_ref[...], (i + 1.0) * lgf, rows * lgf, bd)
    _segment_state_step(rstb, kb, rvb_ref[...], (rows - i) * lgb, rows * lgb, bd)


def _segment_states(proj, n_b, seq, ret_s0, hgrn_s0, lg_rows, lb_rows, consts, tables):
    ns = seq // SEG
    blk = (SEG, GROUP)

    def fwd(cb):
        return pl.BlockSpec(blk, lambda i, s, cb=cb: (i * ns + s, cb))

    def bwd(cb):
        return pl.BlockSpec(blk, lambda i, s, cb=cb: (i * ns + ns - 1 - s, cb))

    tf = pl.BlockSpec((SEG, LANES), lambda i, s: (s, 0))
    tb = pl.BlockSpec((SEG, LANES), lambda i, s: (ns - 1 - s, 0))
    st_spec = pl.BlockSpec((None, GROUP, GROUP), lambda i, s: (i, 0, 0))
    ef_spec = pl.BlockSpec((None, None, GROUP, GROUP), lambda i, s: (i, s, 0, 0))
    eb_spec = pl.BlockSpec((None, None, GROUP, GROUP), lambda i, s: (i, ns - 1 - s, 0, 0))
    e_shape = _sds((n_b, ns, GROUP, GROUP), F32)
    return pl.pallas_call(
        _states_body,
        grid=(n_b, ns),
        in_specs=[fwd(CB_RK), fwd(CB_RV), bwd(CB_RK), bwd(CB_RV), tf, tf, tb, tb, _const_spec((2, GROUP)),
                  fwd(CB_HZF), fwd(CB_HI), bwd(CB_HZB), bwd(CB_HI), _const_spec((2, GROUP)),
                  _const_spec((SEG, SEG)), _const_spec((SEG, SEG)),
                  st_spec, st_spec, st_spec, st_spec, _const_spec((GROUP, GROUP))],
        out_specs=[ef_spec, eb_spec, ef_spec, eb_spec],
        out_shape=[e_shape] * 4,
        scratch_shapes=[pltpu.VMEM((GROUP, GROUP), F32)] * 4,
        compiler_params=_cparams("parallel", "arbitrary"),
        name="seg_states",
    )(proj, proj, proj, proj, tables[0], tables[1], tables[0], tables[1], lg_rows,
      proj, proj, proj, proj, lb_rows, consts["tril_seg"], consts["triu_seg"],
      *ret_s0, *hgrn_s0, consts["bd"])


def _ret_out_body(rope, has_state, emit_state, *refs):
    rq_ref, rk_ref, rv_ref, rg_ref = refs[:4]
    refs = refs[4:]
    cos = sin = None
    if rope:
        cos, sin = refs[0][...], refs[1][...]
        refs = refs[2:]
    lgs_ref, lg_ref, gain_ref, g64_ref = refs[:4]
    refs = refs[4:]
    if has_state:
        ef_ref, eb_ref = refs[:2]
        refs = refs[2:]
    o_ref = refs[0]
    mask_ref = refs[-1]
    rows = rq_ref.shape[0]

    @pl.when((pl.program_id(0) == 0) & (pl.program_id(1) == 0))
    def _():
        d = (lax.broadcasted_iota(jnp.int32, (rows, rows), 0)
             - lax.broadcasted_iota(jnp.int32, (rows, rows), 1)).astype(F32)
        for h in range(N_HEADS):
            mask_ref[h * rows:(h + 1) * rows, :] = (
                jnp.where(d >= 0, jnp.exp(jnp.maximum(d, 0.0) * lgs_ref[0, h]), 0.0)
                + jnp.where(d <= 0, jnp.exp(jnp.maximum(-d, 0.0) * lgs_ref[1, h]), 0.0))

    q, k = _ret_operands(rope, rq_ref[...], rk_ref[...], cos, sin)
    v = rv_ref[...]
    kb16, vb16 = k.astype(BF16), v.astype(BF16)
    s4 = _dot_nt(_stack_heads(q.astype(BF16)), kb16)
    o = _unstack_heads(_dot((s4 * mask_ref[...]).astype(BF16), vb16), rows)
    i = _row_index(rows).astype(F32)
    lgf, lgb = lg_ref[0:1, :], lg_ref[1:2, :]
    if has_state:
        o = o + _dot_nt((q * jnp.exp((i + 1.0) * lgf)).astype(BF16), ef_ref[...].astype(BF16))
        o = o + _dot_nt((q * jnp.exp((rows - i) * lgb)).astype(BF16), eb_ref[...].astype(BF16))
    y = _group_rms(o, g64_ref[...], HEAD_DIM, gain_ref[...]) * _silu(rg_ref[...])
    o_ref[...] = y.astype(BF16)
    if emit_state:
        sf_ref, sb_ref = refs[1:3]
        sf_ref[...] = _head_blocks(_dot_tn((k * jnp.exp((rows - 1.0 - i) * lgf)).astype(BF16), vb16))
        sb_ref[...] = _head_blocks(_dot_tn((k * jnp.exp(i * lgb)).astype(BF16), vb16))


def _ret_out(proj, n_b, seq, lg_smem, lg_rows, gain, consts, tables, states, emit_state):
    ns = seq // SEG
    rope = tables is not None
    blk = (SEG, GROUP)

    def col(cb):
        return pl.BlockSpec(blk, lambda i, s, cb=cb: (i * ns + s, cb))

    in_specs = [col(CB_RQ), col(CB_RK), col(CB_RV), col(CB_RG)]
    args = [proj] * 4
    if rope:
        in_specs += [pl.BlockSpec((SEG, LANES), lambda i, s: (s, 0))] * 2
        args += list(tables)
    in_specs += [pl.BlockSpec(memory_space=pltpu.SMEM), _const_spec((2, GROUP)), _const_spec((1, GROUP)),
                 _const_spec((GROUP, GROUP))]
    args += [lg_smem, lg_rows, gain, consts["g64"]]
    if states is not None:
        in_specs += [pl.BlockSpec((None, None, GROUP, GROUP), lambda i, s: (i, s, 0, 0))] * 2
        args += list(states)
    out_specs = [pl.BlockSpec(blk, lambda i, s: (i * ns + s, 0))]
    out_shape = [_sds((n_b * seq, GROUP), BF16)]
    if emit_state:
        assert ns == 1 and states is None
        out_specs += [pl.BlockSpec((None, HEAD_DIM, GROUP), lambda i, s: (i, 0, 0))] * 2
        out_shape += [_sds((n_b, HEAD_DIM, GROUP), F32)] * 2
    return pl.pallas_call(
        functools.partial(_ret_out_body, rope, states is not None, emit_state),
        grid=(n_b, ns),
        in_specs=in_specs, out_specs=out_specs, out_shape=out_shape,
        scratch_shapes=[pltpu.VMEM((N_HEADS * SEG, SEG), F32)],
        compiler_params=_cparams("arbitrary", "arbitrary"),
        name="ret_out_ctx" if states is not None else "ret_out",
    )(*args)


PAIR_LEVELS_MATMUL = (2, 4, 8)
PAIR_LEVELS_CUMSUM = (16, 32, 64, 128, 256)
PAIR_FOLD = 64


def _pair_level_matrix(n, forward):
    mats = []
    j = np.arange(n)[None, :]
    t = np.arange(n)[:, None]
    for g in PAIR_LEVELS_MATMUL:
        half = g // 2
        pos = t % g
        if forward:
            mid = t - pos + half - 1
            m = np.where(pos >= half, (j > mid) & (j <= t), (j > t) & (j <= mid))
        else:
            mid = t - pos + half
            m = np.where(pos < half, (j >= t) & (j < mid), (j >= mid) & (j < t))
        mats.append(m)
    return jnp.asarray(np.concatenate(mats, axis=0), BF16)


def _pair_decays(lf, cs, small_mat, forward):
    rows = lf.shape[0]
    hi = lf.astype(BF16)
    lo = (lf - hi.astype(F32)).astype(BF16)
    d = _dot(small_mat, hi) + _dot(small_mat, lo)
    out = [jnp.exp(d[i * rows:(i + 1) * rows]) for i in range(len(PAIR_LEVELS_MATMUL))]
    for g in PAIR_LEVELS_CUMSUM:
        half = g // 2
        pieces = []
        for grp in range(rows // g):
            mid = grp * g + (half - 1 if forward else half)
            pieces.append(jnp.broadcast_to(cs[mid:mid + 1, :], (g, cs.shape[1])))
        diff = cs - (pieces[0] if len(pieces) == 1 else jnp.concatenate(pieces, axis=0))
        out.append(jnp.exp(jnp.minimum(diff, -diff)))
    return out


def _hgrn_pair_weights(q, kf, kb, decays_f, decays_b):
    rows = q.shape[0]
    nfold = rows // PAIR_FOLD
    levels = PAIR_LEVELS_MATMUL + PAIR_LEVELS_CUMSUM
    pos = _row_index(rows)
    t_loc = lax.broadcasted_iota(jnp.int32, (PAIR_FOLD, GROUP), 0)
    s_loc = lax.broadcasted_iota(jnp.int32, (PAIR_FOLD, GROUP), 1) % PAIR_FOLD
    r_big = lax.broadcasted_iota(jnp.int32, (N_HEADS * rows, rows), 0) % rows
    c_big = lax.broadcasted_iota(jnp.int32, (N_HEADS * rows, rows), 1)
    folded = [None] * nfold
    stacked = None
    for g, ef, eb in reversed(list(zip(levels, decays_f, decays_b))):
        right = (pos % g) >= (g // 2)
        qs = (q * jnp.where(right, ef, eb)).astype(BF16)
        ks = (jnp.where(right, kb, kf) * jnp.where(right, eb, ef)).astype(BF16)
        if g > PAIR_FOLD:
            s = _dot_nt(_stack_heads(qs), ks)
            stacked = s if stacked is None else jnp.where((r_big // g) == (c_big // g), s, stacked)
        else:
            same = None if g == PAIR_FOLD else (t_loc // g) == (s_loc // g)
            for j in range(nfold):
                r0, r1 = j * PAIR_FOLD, (j + 1) * PAIR_FOLD
                s = _dot_nt(qs[r0:r1], _stack_heads(ks[r0:r1]))
                folded[j] = s if same is None else jnp.where(same, s, folded[j])
    stacked = jnp.where((r_big // PAIR_FOLD) == (c_big // PAIR_FOLD), 0.0, stacked)
    folded = [jnp.where(t_loc == s_loc, 0.0, f) for f in folded]
    return folded, stacked


def _hgrn_out_body(has_state, emit_state, *refs):
    (q_ref, zf_ref, zb_ref, v_ref, g_ref, lb_ref, gain_ref, g64_ref,
     tril_ref, triu_ref, smf_ref, smb_ref) = refs[:12]
    refs = refs[12:]
    g64 = g64_ref[...]
    if has_state:
        ef, eb = refs[0][...], refs[1][...]
        refs = refs[2:]
    o_ref = refs[0]
    q, v = q_ref[...], v_ref[...]
    rows = q.shape[0]
    lff, kf = _hgrn_gate(zf_ref[...], lb_ref[0:1, :])
    lfb, kb = _hgrn_gate(zb_ref[...], lb_ref[1:2, :])
    csf = _sum_rows(tril_ref[...], lff)
    csb = _sum_rows(triu_ref[...], lfb)
    folded, stacked = _hgrn_pair_weights(q, kf, kb, _pair_decays(lff, csf, smf_ref[...], True),
                                         _pair_decays(lfb, csb, smb_ref[...], False))
    vb = v.astype(BF16)
    o = _dot((q * (kf + kb)).astype(BF16), g64) * v
    o = o + _unstack_heads(_dot(stacked.astype(BF16), vb), rows)
    tiles = []
    for j in range(rows // PAIR_FOLD):
        r0, r1 = j * PAIR_FOLD, (j + 1) * PAIR_FOLD
        tiles.append(_dot(folded[j].astype(BF16), _stack_heads(vb[r0:r1])))
    o = o + jnp.concatenate(tiles, axis=0)
    if has_state:
        o = o + _dot_nt((q * jnp.exp(csf)).astype(BF16), ef.astype(BF16))
        o = o + _dot_nt((q * jnp.exp(csb)).astype(BF16), eb.astype(BF16))
    y = _group_rms(o, g64, HEAD_DIM, gain_ref[...]) * _silu(g_ref[...])
    o_ref[...] = y.astype(BF16)
    if emit_state:
        totf, totb = csf[rows - 1:rows, :], csb[0:1, :]
        refs[1][...] = _head_blocks(_dot_tn((kf * jnp.exp(totf - csf)).astype(BF16), vb))
        refs[2][...] = _head_blocks(_dot_tn((kb * jnp.exp(totb - csb)).astype(BF16), vb))


def _hgrn_out(proj, n_b, seq, lb_rows, gain, consts, states, emit_state):
    ns = seq // SEG
    blk = (SEG, GROUP)

    def col(cb):
        return pl.BlockSpec(blk, lambda i, s, cb=cb: (i * ns + s, cb))

    n_small = len(PAIR_LEVELS_MATMUL) * SEG
    in_specs = [col(CB_HQ), col(CB_HZF), col(CB_HZB), col(CB_HI), col(CB_HG),
                _const_spec((2, GROUP)), _const_spec((1, GROUP)), _const_spec((GROUP, GROUP)),
                _const_spec((SEG, SEG)), _const_spec((SEG, SEG)),
                _const_spec((n_small, SEG)), _const_spec((n_small, SEG))]
    args = [proj] * 5 + [lb_rows, gain, consts["g64"],
                         consts["tril_seg"], consts["triu_seg"], consts["pair_f"], consts["pair_b"]]
    if states is not None:
        in_specs += [pl.BlockSpec((None, None, GROUP, GROUP), lambda i, s: (i, s, 0, 0))] * 2
        args += list(states)
    out_specs = [pl.BlockSpec(blk, lambda i, s: (i * ns + s, 0))]
    out_shape = [_sds((n_b * seq, GROUP), BF16)]
    if emit_state:
        assert ns == 1 and states is None
        out_specs += [pl.BlockSpec((None, HEAD_DIM, GROUP), lambda i, s: (i, 0, 0))] * 2
        out_shape += [_sds((n_b, HEAD_DIM, GROUP), F32)] * 2
    return pl.pallas_call(
        functools.partial(_hgrn_out_body, states is not None, emit_state),
        grid=(n_b, ns),
        in_specs=in_specs, out_specs=out_specs, out_shape=out_shape,
        compiler_params=_cparams("parallel", "parallel"),
        name="hgrn_out_ctx" if states is not None else "hgrn_out",
    )(*args)


def _ffn_body(x_ref, oa_ref, ob_ref, oc_ref, od_ref, wo_ref, g1_ref, n2_ref, sc_ref, sh_ref, g2_ref,
              wg_ref, wu_ref, wd_ref, y_ref, x1_ref, h_ref, acc_ref):
    j = pl.program_id(1)

    @pl.when(j == 0)
    def _():
        mix = _dot(oa_ref[...], wo_ref[0:GROUP, :])
        mix += _dot_tn(ob_ref[...], wo_ref[GROUP:2 * GROUP, :])
        mix += _dot_tn(oc_ref[...], wo_ref[2 * GROUP:3 * GROUP, :])
        mix += _dot(od_ref[...], wo_ref[3 * GROUP:4 * GROUP, :])
        x1 = x_ref[...] + g1_ref[...] * mix
        x1_ref[...] = x1
        h = x1 * lax.rsqrt(jnp.mean(x1 * x1, axis=-1, keepdims=True) + EPS) * n2_ref[...]
        h_ref[...] = (h * (1.0 + sc_ref[...]) + sh_ref[...]).astype(BF16)
        acc_ref[...] = jnp.zeros_like(acc_ref)

    h = h_ref[...]
    a = _silu(_dot(h, wg_ref[...])) * _dot(h, wu_ref[...])
    acc_ref[...] += _dot(a.astype(BF16), wd_ref[...])

    @pl.when(j == pl.num_programs(1) - 1)
    def _():
        y_ref[...] = x1_ref[...] + g2_ref[...] * acc_ref[...]


def _outproj_ffn(x, mixes, w_out, g1, n2, sc2, sh2, g2, w_in, w_dn, rows_per_mod, mod_row0, tm=1024, th=256):
    t, d = x.shape
    hid = w_dn.shape[0]
    nh = hid // th
    mod_map = _mod_row_map(tm, rows_per_mod, mod_row0)
    mod2 = lambda i, j: mod_map(i)
    row = lambda i, j: (i, 0)
    mix_spec = pl.BlockSpec((tm, GROUP), row)
    mix_t_spec = pl.BlockSpec((GROUP, tm), lambda i, j: (0, i))
    mod_spec = pl.BlockSpec((None, 1, d), mod2)
    return pl.pallas_call(
        _ffn_body,
        grid=(t // tm, nh),
        in_specs=[pl.BlockSpec((tm, d), row), mix_spec, mix_t_spec, mix_t_spec, mix_spec]
                 + [_const_spec((d, d)), mod_spec, _const_spec((1, d)), mod_spec, mod_spec, mod_spec,
                    pl.BlockSpec((d, th), lambda i, j: (0, j)),
                    pl.BlockSpec((d, th), lambda i, j: (0, nh + j)),
                    pl.BlockSpec((th, d), lambda i, j: (j, 0))],
        out_specs=pl.BlockSpec((tm, d), row),
        out_shape=_sds((t, d), F32),
        scratch_shapes=[pltpu.VMEM((tm, d), F32), pltpu.VMEM((tm, d), BF16), pltpu.VMEM((tm, d), F32)],
        compiler_params=_cparams("parallel", "arbitrary"),
        name="outproj_ffn",
    )(x, *mixes, w_out, g1, n2, sc2, sh2, g2, w_in, w_in, w_dn)


def _lane_rows(per_head):
    return jnp.repeat(per_head.astype(F32), HEAD_DIM, axis=1)


def _states_to_lanes(s):
    eye = jnp.eye(N_HEADS, dtype=F32)
    st = jnp.swapaxes(s.astype(F32), -1, -2)[..., :, :, None, :] * eye[:, None, :, None]
    return st.reshape(s.shape[:-3] + (GROUP, GROUP))


def _with_ones_rows(vt):
    ones = jnp.ones(vt.shape[:-2] + (V_ROWS - HEAD_DIM, vt.shape[-1]), vt.dtype)
    return jnp.concatenate([vt, ones], axis=-2)


def _lanes_to_states(sf, sb):
    s = jnp.stack([sf, sb], axis=1)
    return s.reshape(s.shape[:3] + (N_HEADS, HEAD_DIM)).transpose(0, 1, 3, 2, 4)


def kernel(x_prompt, x_sample, state_ret, cache_win_k, cache_win_v, cache_diff_k, cache_diff_v, state_hgrn, c, c_ctx, norm1_g, norm2_g, w_ada, b_ada, w_in, ret_decay, ret_norm_g, win_q_norm, win_k_norm, win_sink, diff_q_norm, diff_k_norm, diff_lambda, diff_norm_g, hgrn_lb_logits, hgrn_norm_g, w_out, w_ffn_in, w_ffn_out):
    n_p, seq_p, d = x_prompt.shape
    n_s, seq_s, _ = x_sample.shape
    depth = w_in.shape[0]
    t_s, t_p = n_s * seq_s, n_p * seq_p
    assert seq_p == SEG and seq_s % SEG == 0 and d == N_HEADS * GROUP

    tril_seg, triu_seg = _cumsum_matrices(SEG)
    consts = dict(g64=_group_matrix(GROUP, HEAD_DIM), g32=_group_matrix(GROUP, DIFF_DIM),
                  bd=_group_matrix(GROUP, HEAD_DIM).astype(F32), dupk=_kv_dup_matrix(),
                  tril_seg=tril_seg, triu_seg=triu_seg,
                  pair_f=_pair_level_matrix(SEG, True), pair_b=_pair_level_matrix(SEG, False))
    tab64 = _rope_tables(seq_s, HEAD_DIM)
    tab32 = _rope_tables(seq_s, DIFF_DIM)

    n_rows = -(-(n_s + 1) // 8) * 8
    cond = jnp.zeros((n_rows, d), F32).at[:n_s].set(c).at[n_s].set(c_ctx)
    mod = _adaln(cond, w_ada, b_ada).reshape(depth, n_rows, 6, 1, d)

    lb_p = jax.nn.softmax(hgrn_lb_logits.astype(F32), axis=0)
    lb_all = jnp.cumsum(lb_p, axis=0) - lb_p
    log_gamma = -jnp.exp(ret_decay.astype(F32))

    ctx_states = _states_to_lanes(jnp.stack([state_ret, state_hgrn], axis=2))

    xs = x_sample.reshape(t_s, d)
    xp = x_prompt.reshape(t_p, d)
    new_ret, new_wk, new_wv, new_dk, new_dv, new_hg = [], [], [], [], [], []
    for l in range(depth):
        lam_init = 0.8 - 0.6 * math.exp(-0.3 * l)
        sh1, sc1, g1, sh2, sc2, g2 = [mod[l, :, i] for i in range(6)]
        w_in_l, w_out_l = w_in[l].astype(BF16), w_out[l].astype(BF16)
        w_up_l, w_dn_l = w_ffn_in[l].astype(BF16), w_ffn_out[l].astype(BF16)
        n1, n2 = norm1_g[l][None], norm2_g[l][None]
        gains = (jnp.tile(win_q_norm[l], N_HEADS)[None], jnp.tile(win_k_norm[l], KV_HEADS)[None],
                 jnp.tile(diff_q_norm[l], 2 * N_HEADS)[None], jnp.tile(diff_k_norm[l], 2 * N_HEADS)[None])
        sink = win_sink[l].astype(F32)
        lq1, lk1, lq2, lk2 = diff_lambda[l].astype(F32)
        lam = jnp.exp(jnp.sum(lq1 * lk1)) - jnp.exp(jnp.sum(lq2 * lk2)) + lam_init
        lam2 = jnp.stack([lam, jnp.asarray(1.0 - lam_init, F32)]).reshape(1, 2)
        dgain = diff_norm_g[l].astype(F32)[:, None]
        lg = log_gamma[l]
        lg_rows = _lane_rows(lg)
        rgain = ret_norm_g[l][None]
        lb_rows = lb_all[l]
        hgain = hgrn_norm_g[l][None]

        proj = _inproj(xs, n1, sc1, sh1, w_in_l, seq_s, 0)
        wq, wk, wvt, dq, dk, dvt = _prep(proj, gains, consts, tab64 + tab32, False)
        ctx_wk = jnp.repeat(cache_win_k[:, l].transpose(0, 2, 1, 3), N_HEADS // KV_HEADS, axis=2)
        ctx_wk = ctx_wk.reshape(n_s, -1, GROUP).astype(BF16)
        ctx_wvt = _with_ones_rows(cache_win_v[:, l].swapaxes(-1, -2).astype(BF16))
        o_win = _win_sample(wq, wk, wvt, ctx_wk, ctx_wvt, jnp.repeat(sink * LOG2_E, WIN_BLOCK)[None])
        ctx_dk = cache_diff_k[:, l].transpose(0, 3, 1, 2, 4).reshape(n_s, -1, GROUP).astype(BF16)
        ctx_dvt = _with_ones_rows(cache_diff_v[:, l].swapaxes(-1, -2).astype(BF16))
        o_diff = _diff_attention(dq, dk, dvt, (ctx_dk, ctx_dvt), seq_s, lam2, dgain, 1)
        entry = _segment_states(proj, n_s, seq_s, (ctx_states[:, l, 0, 0], ctx_states[:, l, 0, 1]),
                                (ctx_states[:, l, 1, 0], ctx_states[:, l, 1, 1]), lg_rows, lb_rows, consts, tab64)
        (o_ret,) = _ret_out(proj, n_s, seq_s, lg, lg_rows, rgain, consts, tab64, entry[:2], False)
        (o_h,) = _hgrn_out(proj, n_s, seq_s, lb_rows, hgain, consts, entry[2:], False)
        xs = _outproj_ffn(xs, (o_ret, o_win, o_diff, o_h), w_out_l, g1, n2, sc2, sh2, g2, w_up_l, w_dn_l, seq_s, 0)

        proj = _inproj(xp, n1, sc1, sh1, w_in_l, t_p, n_s)
        wq, wk, wvt, dq, dk, dvt, wk_c, wv_c, dk_c, dv_c = _prep(proj, gains, consts, None, True, tm=seq_p)
        o_win = _win_prompt(wq, wk, wvt, seq_p, jnp.repeat(sink * LOG2_E, seq_p)[None])
        o_diff = _diff_attention(dq, dk, dvt, None, seq_p, lam2, dgain, N_HEADS)
        o_ret, rsf, rsb = _ret_out(proj, n_p, seq_p, lg, lg_rows, rgain, consts, None, None, True)
        o_h, hsf, hsb = _hgrn_out(proj, n_p, seq_p, lb_rows, hgain, consts, None, True)
        xp = _outproj_ffn(xp, (o_ret, o_win, o_diff, o_h), w_out_l, g1, n2, sc2, sh2, g2, w_up_l, w_dn_l, t_p, n_s)

        new_ret.append(_lanes_to_states(rsf, rsb))
        new_hg.append(_lanes_to_states(hsf, hsb))
        new_wk.append(wk_c)
        new_wv.append(wv_c)
        new_dk.append(dk_c)
        new_dv.append(dv_c)

    return (xp.reshape(n_p, seq_p, d), xs.reshape(n_s, seq_s, d),
            jnp.stack(new_ret, axis=1), jnp.stack(new_wk, axis=1), jnp.stack(new_wv, axis=1),
            jnp.stack(new_dk, axis=1), jnp.stack(new_dv, axis=1), jnp.stack(new_hg, axis=1))
```

```python
import functools
import math

import numpy as np
import jax
import jax.numpy as jnp
from jax import lax
from jax.experimental import pallas as pl
from jax.experimental.pallas import tpu as pltpu

F32 = jnp.float32
BF16 = jnp.bfloat16

GROUP = 256
HEAD_DIM = 64
N_HEADS = 4
KV_HEADS = 2
DIFF_DIM = 32
WINDOW = 128
WIN_BLOCK = 128
GRID_W = 64
ROPE_BASE = 10000.0
EPS = 1e-6
MASK_VALUE = -1e30
SEG = 256
LANES = 128
BF16_SUBLANES = 16
LOG2_E = math.log2(math.e)
V_ROWS = HEAD_DIM + BF16_SUBLANES
DIFF_LOOKAHEAD = 2

INPROJ_TM = 512
PREP_TM = 512
FFN_TM = 1024
FFN_TH = 256
DIFF_TQ = 512
DIFF_CK = 512
DIFF_HEADS_LATENT = 2
V7X_VMEM_LIMIT_BYTES = 56 * 1024 * 1024

(CB_RQ, CB_RK, CB_RV, CB_RG, CB_WQ, CB_WKV, CB_DQ, CB_DK, CB_DV,
 CB_HQ, CB_HZF, CB_HZB, CB_HI, CB_HG) = range(14)

_NT = (((1,), (1,)), ((), ()))
_TN = (((0,), (0,)), ((), ()))


def _sds(shape, dtype):
    return jax.ShapeDtypeStruct(shape, dtype)


def _cparams(*sem):
    return pltpu.CompilerParams(dimension_semantics=sem, vmem_limit_bytes=V7X_VMEM_LIMIT_BYTES)


def _const_spec(shape):
    return pl.BlockSpec(shape, lambda *_: (0,) * len(shape))


def _dot(a, b):
    return jnp.dot(a, b, preferred_element_type=F32)


def _dot_nt(a, b):
    return lax.dot_general(a, b, _NT, preferred_element_type=F32)


def _dot_tn(a, b):
    return lax.dot_general(a, b, _TN, preferred_element_type=F32)


def _silu(x):
    return x * jax.nn.sigmoid(x)


def _group_matrix(width, gsize):
    i = np.arange(width)
    return jnp.asarray((i[:, None] // gsize) == (i[None, :] // gsize), BF16)


def _cumsum_matrices(n):
    r, c = np.arange(n)[:, None], np.arange(n)[None, :]
    return jnp.asarray(c <= r, BF16), jnp.asarray(c >= r, BF16)


def _kv_dup_matrix():
    d = np.zeros((KV_HEADS * HEAD_DIM, GROUP), np.float32)
    for h in range(N_HEADS):
        kv = h // (N_HEADS // KV_HEADS)
        for j in range(HEAD_DIM):
            d[kv * HEAD_DIM + j, h * HEAD_DIM + j] = 1.0
    return jnp.asarray(d, BF16)


def _rope_tables(n_tokens, head_dim):
    d = head_dim // 2
    half = d // 2
    inv = ROPE_BASE ** (-jnp.arange(half, dtype=F32) / half)
    t = jnp.arange(n_tokens)
    row = (t // GRID_W).astype(F32)
    col = (t % GRID_W).astype(F32)
    j = np.arange(LANES) % head_dim
    w = j % d
    use_row = jnp.asarray((j // d) == 0)
    pos = jnp.where(use_row[None, :], row[:, None], col[:, None])
    ang = pos * inv[w % half][None, :]
    sign = jnp.asarray(np.where(w >= half, 1.0, -1.0), F32)
    return jnp.cos(ang), jnp.sin(ang) * sign[None, :]


def _group_rms(x, g_mat, gsize, gain):
    ss = _dot((x * x).astype(BF16), g_mat)
    return x * lax.rsqrt(ss * (1.0 / gsize) + EPS) * gain


def _rope(y, cos, sin, half):
    lane = lax.broadcasted_iota(jnp.int32, (1, LANES), 1)
    second = (lane % (2 * half)) >= half
    outs = []
    for p in range(y.shape[1] // LANES):
        z = y[:, p * LANES:(p + 1) * LANES]
        partner = jnp.where(second, pltpu.roll(z, half, 1), pltpu.roll(z, LANES - half, 1))
        outs.append(z * cos + partner * sin)
    return outs[0] if len(outs) == 1 else jnp.concatenate(outs, axis=1)


def _head_of_lane(width=GROUP):
    return lax.broadcasted_iota(jnp.int32, (1, width), 1) // HEAD_DIM


def _stack_heads(q, n_heads=N_HEADS):
    head = _head_of_lane()
    zero = jnp.zeros_like(q)
    return jnp.concatenate([jnp.where(head == h, q, zero) for h in range(n_heads)], axis=0)


def _unstack_heads(o4, rows):
    head = _head_of_lane()
    out = jnp.where(head == 0, o4[:rows], 0.0)
    for h in range(1, N_HEADS):
        out = out + jnp.where(head == h, o4[h * rows:(h + 1) * rows], 0.0)
    return out


def _head_blocks(s):
    head = _head_of_lane()
    out = jnp.where(head == 0, s[:HEAD_DIM], 0.0)
    for h in range(1, N_HEADS):
        out = out + jnp.where(head == h, s[h * HEAD_DIM:(h + 1) * HEAD_DIM], 0.0)
    return out


def _split3(x):
    hi = x.astype(BF16)
    r = x - hi.astype(F32)
    mid = r.astype(BF16)
    lo = (r - mid.astype(F32)).astype(BF16)
    return hi, mid, lo


def _sum_rows(m01, x):
    hi, mid, lo = _split3(x)
    return _dot(m01, hi) + _dot(m01, mid) + _dot(m01, lo)


def _adaln_body(c_ref, w_ref, b_ref, o_ref):
    c = c_ref[...]
    o_ref[...] = jnp.dot(_silu(c), w_ref[...], preferred_element_type=F32,
                         precision=lax.Precision.HIGHEST) + b_ref[...]


def _adaln(cond, w_ada, b_ada):
    depth, d, n = w_ada.shape
    rows = cond.shape[0]
    tn = n // 4
    return pl.pallas_call(
        _adaln_body,
        grid=(depth, n // tn),
        in_specs=[pl.BlockSpec((rows, d), lambda l, j: (0, 0)),
                  pl.BlockSpec((None, d, tn), lambda l, j: (l, 0, j)),
                  pl.BlockSpec((None, 1, tn), lambda l, j: (l, 0, j))],
        out_specs=pl.BlockSpec((None, rows, tn), lambda l, j: (l, 0, j)),
        out_shape=_sds((depth, rows, n), F32),
        compiler_params=_cparams("parallel", "parallel"),
        name="adaln",
    )(cond, w_ada, b_ada.reshape(depth, 1, n))


def _inproj_body(x_ref, g_ref, sc_ref, sh_ref, w_ref, o_ref):
    x = x_ref[...]
    h = x * lax.rsqrt(jnp.mean(x * x, axis=-1, keepdims=True) + EPS) * g_ref[...]
    h = h * (1.0 + sc_ref[...]) + sh_ref[...]
    o_ref[...] = _dot(h.astype(BF16), w_ref[...])


def _mod_row_map(tm, rows_per_mod, mod_row0):
    return lambda i: (mod_row0 + (i * tm) // rows_per_mod, 0, 0)


def _inproj(x, g, sc, sh, w, rows_per_mod, mod_row0, tm=INPROJ_TM):
    t, d = x.shape
    n = w.shape[1]
    mod_map = _mod_row_map(tm, rows_per_mod, mod_row0)
    return pl.pallas_call(
        _inproj_body,
        grid=(t // tm,),
        in_specs=[pl.BlockSpec((tm, d), lambda i: (i, 0)),
                  _const_spec((1, d)),
                  pl.BlockSpec((None, 1, d), mod_map),
                  pl.BlockSpec((None, 1, d), mod_map),
                  _const_spec((d, n))],
        out_specs=pl.BlockSpec((tm, n), lambda i: (i, 0)),
        out_shape=_sds((t, n), F32),
        compiler_params=_cparams("parallel"),
        name="inproj",
    )(x, g, sc, sh, w)


def _prep_body(rope, emit_f32, *refs):
    (wq_ref, wkv_ref, dq_ref, dk_ref, dv_ref, qn_ref, kn_ref, dqn_ref, dkn_ref, g64_ref, g32_ref,
     dup_ref) = refs[:12]
    refs = refs[12:]
    if rope:
        c64_ref, s64_ref, c32_ref, s32_ref = refs[:4]
        refs = refs[4:]
    wq_o, wk_o, wvt_o, dq_o, dk_o, dvt_o = refs[:6]
    g64 = g64_ref[...]
    g32 = g32_ref[...]
    tm = dv_ref.shape[0]

    wq = _group_rms(wq_ref[...], g64, HEAD_DIM, qn_ref[...])
    wkv = wkv_ref[...]
    wk = _group_rms(wkv[:, :LANES], g64[:LANES, :LANES], HEAD_DIM, kn_ref[...])
    dq = _group_rms(dq_ref[...], g32, DIFF_DIM, dqn_ref[...])
    dk = _group_rms(dk_ref[...], g32, DIFF_DIM, dkn_ref[...])
    if rope:
        c64, s64, c32, s32 = c64_ref[...], s64_ref[...], c32_ref[...], s32_ref[...]
        wq = _rope(wq, c64, s64, HEAD_DIM // 4)
        wk = _rope(wk, c64, s64, HEAD_DIM // 4)
        dq = _rope(dq, c32, s32, DIFF_DIM // 4)
        dk = _rope(dk, c32, s32, DIFF_DIM // 4)
    wq_o[...] = (wq * (HEAD_DIM ** -0.5 * LOG2_E)).astype(BF16)
    wk_o[...] = _dot(wk.astype(BF16), dup_ref[...]).astype(BF16)
    wvt_o[:, :HEAD_DIM, :] = wkv[:, LANES:].T.reshape(KV_HEADS, HEAD_DIM, tm).astype(BF16)
    wvt_o[:, HEAD_DIM:, :] = jnp.ones((KV_HEADS, V_ROWS - HEAD_DIM, tm), BF16)
    dq_o[...] = (dq * (DIFF_DIM ** -0.5 * LOG2_E)).astype(BF16)
    dk_o[...] = dk.astype(BF16)
    dvt_o[:, :HEAD_DIM, :] = dv_ref[...].T.reshape(N_HEADS, HEAD_DIM, tm).astype(BF16)
    dvt_o[:, HEAD_DIM:, :] = jnp.ones((N_HEADS, V_ROWS - HEAD_DIM, tm), BF16)
    if emit_f32:
        wk_c, wv_c, dk_c, dv_c = refs[6:10]
        dv = dv_ref[...]
        for kv in range(KV_HEADS):
            wk_c[kv] = wk[:, kv * HEAD_DIM:(kv + 1) * HEAD_DIM]
            wv_c[kv] = wkv[:, LANES + kv * HEAD_DIM:LANES + (kv + 1) * HEAD_DIM]
        for h in range(N_HEADS):
            dv_c[h] = dv[:, h * HEAD_DIM:(h + 1) * HEAD_DIM]
            for c in range(2):
                lane0 = h * HEAD_DIM + c * DIFF_DIM
                dk_c[h, c] = dk[:, lane0:lane0 + DIFF_DIM]


def _prep(proj, gains, consts, tables, emit_f32, tm=PREP_TM):
    rope = tables is not None
    rows = proj.shape[0]

    def col(cb):
        return pl.BlockSpec((tm, GROUP), lambda i, cb=cb: (i, cb))

    def vt(heads):
        return pl.BlockSpec((heads, V_ROWS, tm), lambda i: (0, 0, i))

    in_specs = [col(CB_WQ), col(CB_WKV), col(CB_DQ), col(CB_DK), col(CB_DV),
                _const_spec((1, GROUP)), _const_spec((1, LANES)), _const_spec((1, GROUP)), _const_spec((1, GROUP)),
                _const_spec((GROUP, GROUP)), _const_spec((GROUP, GROUP)), _const_spec((LANES, GROUP))]
    args = [proj] * 5 + list(gains) + [consts["g64"], consts["g32"], consts["dupk"]]
    if rope:
        seq = tables[0].shape[0]
        nper = seq // tm
        in_specs += [pl.BlockSpec((tm, LANES), lambda i: (i % nper, 0))] * 4
        args += list(tables)
    out = pl.BlockSpec((tm, GROUP), lambda i: (i, 0))
    tok = _sds((rows, GROUP), BF16)
    out_specs = [out, out, vt(KV_HEADS), out, out, vt(N_HEADS)]
    out_shape = [tok, tok, _sds((KV_HEADS, V_ROWS, rows), BF16), tok, tok, _sds((N_HEADS, V_ROWS, rows), BF16)]
    if emit_f32:
        nb = rows // tm
        out_specs += [pl.BlockSpec((None, KV_HEADS, tm, HEAD_DIM), lambda i: (i, 0, 0, 0)),
                      pl.BlockSpec((None, KV_HEADS, tm, HEAD_DIM), lambda i: (i, 0, 0, 0)),
                      pl.BlockSpec((None, N_HEADS, 2, tm, DIFF_DIM), lambda i: (i, 0, 0, 0, 0)),
                      pl.BlockSpec((None, N_HEADS, tm, HEAD_DIM), lambda i: (i, 0, 0, 0))]
        out_shape += [_sds((nb, KV_HEADS, tm, HEAD_DIM), F32), _sds((nb, KV_HEADS, tm, HEAD_DIM), F32),
                      _sds((nb, N_HEADS, 2, tm, DIFF_DIM), F32), _sds((nb, N_HEADS, tm, HEAD_DIM), F32)]
    return pl.pallas_call(
        functools.partial(_prep_body, rope, emit_f32),
        grid=(rows // tm,),
        in_specs=in_specs, out_specs=out_specs, out_shape=out_shape,
        compiler_params=_cparams("parallel"),
        name="prep_rope" if rope else "prep",
    )(*args)


def _win_body(banded, *refs):
    if banded:
        (q_ref, kp_ref, kc_ref, kn_ref, vp_ref, vc_ref, vn_ref, ck_ref, cvt_ref, sink_ref, o_ref) = refs
        k_refs, v_refs = (kp_ref, kc_ref, kn_ref), (vp_ref, vc_ref, vn_ref)
    else:
        q_ref, kc_ref, vc_ref, sink_ref, o_ref = refs
        k_refs, v_refs = (kc_ref,), (vc_ref,)
    tq = q_ref.shape[0]
    q4 = _stack_heads(q_ref[...])
    k_loc = k_refs[0][...] if len(k_refs) == 1 else jnp.concatenate([r[...] for r in k_refs], axis=0)
    s_loc = _dot_nt(k_loc, q4)
    sink = sink_ref[...]
    if banded:
        n = pl.program_id(1)
        seq = pl.num_programs(1) * WIN_BLOCK
        t = n * WIN_BLOCK + lax.broadcasted_iota(jnp.int32, s_loc.shape, 1) % WIN_BLOCK
        kpos = (n - 1) * WIN_BLOCK + lax.broadcasted_iota(jnp.int32, s_loc.shape, 0)
        valid = (kpos >= 0) & (kpos < seq) & (jnp.abs(t - kpos) <= WINDOW)
        s_loc = jnp.where(valid, s_loc, MASK_VALUE)
        s_ctx = _dot_nt(ck_ref[...], q4)
        m = jnp.maximum(jnp.maximum(s_loc.max(axis=0, keepdims=True), s_ctx.max(axis=0, keepdims=True)), sink)
        p_ctx = jnp.exp2(s_ctx - m).astype(BF16)
    else:
        m = jnp.maximum(s_loc.max(axis=0, keepdims=True), sink)
    p_loc = jnp.exp2(s_loc - m).astype(BF16)
    p_sink = jnp.exp2(sink - m)
    outs = []
    for h in range(N_HEADS):
        kv = h // (N_HEADS // KV_HEADS)
        c0, c1 = h * tq, (h + 1) * tq
        vt = v_refs[0][kv] if len(v_refs) == 1 else jnp.concatenate([r[kv] for r in v_refs], axis=1)
        acc = _dot(vt, p_loc[:, c0:c1])
        if banded:
            acc = acc + _dot(cvt_ref[kv], p_ctx[:, c0:c1])
        outs.append(acc[:HEAD_DIM] / (acc[HEAD_DIM:HEAD_DIM + 1] + p_sink[:, c0:c1]))
    o_ref[...] = jnp.concatenate(outs, axis=0).astype(BF16)


def _win_sample(wq, wk, wvt, ctx_k, ctx_vt, sink_row):
    b, p, _ = ctx_k.shape
    rows = wq.shape[0]
    nq = rows // b // WIN_BLOCK
    blk = (WIN_BLOCK, GROUP)
    vblk = (KV_HEADS, V_ROWS, WIN_BLOCK)
    prev = lambda i, n: i * nq + jnp.maximum(n - 1, 0)
    cur = lambda i, n: i * nq + n
    nxt = lambda i, n: i * nq + jnp.minimum(n + 1, nq - 1)
    return pl.pallas_call(
        functools.partial(_win_body, True),
        grid=(b, nq),
        in_specs=[pl.BlockSpec(blk, lambda i, n: (cur(i, n), 0)),
                  pl.BlockSpec(blk, lambda i, n: (prev(i, n), 0)),
                  pl.BlockSpec(blk, lambda i, n: (cur(i, n), 0)),
                  pl.BlockSpec(blk, lambda i, n: (nxt(i, n), 0)),
                  pl.BlockSpec(vblk, lambda i, n: (0, 0, prev(i, n))),
                  pl.BlockSpec(vblk, lambda i, n: (0, 0, cur(i, n))),
                  pl.BlockSpec(vblk, lambda i, n: (0, 0, nxt(i, n))),
                  pl.BlockSpec((None, p, GROUP), lambda i, n: (i, 0, 0)),
                  pl.BlockSpec((None, KV_HEADS, V_ROWS, p), lambda i, n: (i, 0, 0, 0)),
                  _const_spec((1, N_HEADS * WIN_BLOCK))],
        out_specs=pl.BlockSpec((GROUP, WIN_BLOCK), lambda i, n: (0, cur(i, n))),
        out_shape=_sds((GROUP, rows), BF16),
        compiler_params=_cparams("parallel", "parallel"),
        name="win_sample",
    )(wq, wk, wk, wk, wvt, wvt, wvt, ctx_k, ctx_vt, sink_row)


def _win_prompt(wq, wk, wvt, seq, sink_row):
    rows = wq.shape[0]
    blk = (seq, GROUP)
    return pl.pallas_call(
        functools.partial(_win_body, False),
        grid=(rows // seq,),
        in_specs=[pl.BlockSpec(blk, lambda i: (i, 0)), pl.BlockSpec(blk, lambda i: (i, 0)),
                  pl.BlockSpec((KV_HEADS, V_ROWS, seq), lambda i: (0, 0, i)),
                  _const_spec((1, N_HEADS * seq))],
        out_specs=pl.BlockSpec((GROUP, seq), lambda i: (0, i)),
        out_shape=_sds((GROUP, rows), BF16),
        compiler_params=_cparams("parallel"),
        name="win_prompt",
    )(wq, wk, wvt, sink_row)


def _diff_body(ck, has_ctx, heads, *refs):
    if has_ctx:
        q_ref, k_ref, vt_ref, ck_ref, cvt_ref, lam_ref, gain_ref, o_ref = refs
    else:
        q_ref, k_ref, vt_ref, lam_ref, gain_ref, o_ref = refs
    for j in range(heads):
        sources = [(k_ref, vt_ref.at[j])] + ([(ck_ref, cvt_ref.at[j])] if has_ctx else [])
        rows = slice(j * HEAD_DIM, (j + 1) * HEAD_DIM)
        o_ref[rows, :] = _diff_head(ck, pl.program_id(2) * heads + j, q_ref[...], sources,
                                    lam_ref, gain_ref[rows, :])


def _diff_head(ck, h, q, sources, lam_ref, gain):
    tq = q.shape[0]
    sub = lax.broadcasted_iota(jnp.int32, (1, GROUP), 1) // DIFF_DIM
    zero = jnp.zeros_like(q)
    q2 = jnp.concatenate([jnp.where(sub == 2 * h, q, zero), jnp.where(sub == 2 * h + 1, q, zero)], axis=0)
    chunks = []
    for kr, vr in sources:
        n_keys = kr.shape[0]
        step = min(ck, n_keys)
        chunks += [(kr, vr, c0, step) for c0 in range(0, n_keys, step)]

    def scores(i):
        kr, _, c0, step = chunks[i]
        return _dot_nt(kr[c0:c0 + step, :], q2)

    pending = [scores(i) for i in range(min(DIFF_LOOKAHEAD, len(chunks)))]
    m = acc = None
    for i, (_, vr, c0, step) in enumerate(chunks):
        s = pending.pop(0)
        if i + DIFF_LOOKAHEAD < len(chunks):
            pending.append(scores(i + DIFF_LOOKAHEAD))
        vt = vr[:, c0:c0 + step]
        mc = s.max(axis=0, keepdims=True)
        if m is None:
            m = mc
            acc = _dot(vt, jnp.exp2(s - m).astype(BF16))
        else:
            m_new = jnp.maximum(m, mc)
            acc = jnp.exp2(m - m_new) * acc + _dot(vt, jnp.exp2(s - m_new).astype(BF16))
            m = m_new
    o2 = acc[:HEAD_DIM] / acc[HEAD_DIM:HEAD_DIM + 1]
    od = o2[:, :tq] - lam_ref[0, 0] * o2[:, tq:]
    y = od * lax.rsqrt(jnp.mean(od * od, axis=0, keepdims=True) + EPS) * gain * lam_ref[0, 1]
    return y.astype(BF16)


def _diff_attention(dq, dk, dvt, ctx, seq, lam2, gain_col, heads, tq=DIFF_TQ, ck=DIFF_CK):
    rows = dq.shape[0]
    b = rows // seq
    tq = min(tq, seq)
    nq = seq // tq
    in_specs = [pl.BlockSpec((tq, GROUP), lambda i, n, h: (i * nq + n, 0)),
                pl.BlockSpec((seq, GROUP), lambda i, n, h: (i, 0)),
                pl.BlockSpec((heads, V_ROWS, seq), lambda i, n, h: (h, 0, i))]
    args = [dq, dk, dvt]
    if ctx is not None:
        p = ctx[0].shape[1]
        in_specs += [pl.BlockSpec((None, p, GROUP), lambda i, n, h: (i, 0, 0)),
                     pl.BlockSpec((None, heads, V_ROWS, p), lambda i, n, h: (i, h, 0, 0))]
        args += list(ctx)
    in_specs += [pl.BlockSpec(memory_space=pltpu.SMEM),
                 pl.BlockSpec((heads * HEAD_DIM, 1), lambda i, n, h: (h, 0))]
    args += [lam2, gain_col]
    return pl.pallas_call(
        functools.partial(_diff_body, ck, ctx is not None, heads),
        grid=(b, nq, N_HEADS // heads),
        in_specs=in_specs,
        out_specs=pl.BlockSpec((heads * HEAD_DIM, tq), lambda i, n, h: (h, i * nq + n)),
        out_shape=_sds((GROUP, rows), BF16),
        compiler_params=_cparams("parallel", "parallel", "parallel"),
        name="diff_ctx" if ctx is not None else "diff",
    )(*args)


def _ret_operands(rope, rq, rk, cos, sin):
    q = rq
    k = rk * HEAD_DIM ** -0.5
    if rope:
        q = _rope(q, cos, sin, HEAD_DIM // 4)
        k = _rope(k, cos, sin, HEAD_DIM // 4)
    return q, k


def _hgrn_gate(z, lb):
    f = lb + (1.0 - lb) * jax.nn.sigmoid(z)
    return jnp.log(jnp.maximum(f, 1e-30)), (1.0 - lb) * jax.nn.sigmoid(-z)


def _row_index(rows):
    return lax.broadcasted_iota(jnp.int32, (rows, 1), 0)


def _segment_state_step(st_ref, k, v, cum, total, bd):
    ku = k * jnp.exp(total - cum)
    st_ref[...] = st_ref[...] * jnp.exp(total) + _dot_tn(v.astype(BF16), ku.astype(BF16)) * bd


def _states_body(rkf_ref, rvf_ref, rkb_ref, rvb_ref, cf_ref, sf_ref, cb_ref, sb_ref, lg_ref,
                 zf_ref, hvf_ref, zb_ref, hvb_ref, lb_ref, tril_ref, triu_ref,
                 r0f_ref, r0b_ref, h0f_ref, h0b_ref, bd_ref,
                 ref_ref, reb_ref, hef_ref, heb_ref, rstf, rstb, hstf, hstb):
    s = pl.program_id(1)

    @pl.when(s == 0)
    def _():
        rstf[...] = r0f_ref[...]
        rstb[...] = r0b_ref[...]
        hstf[...] = h0f_ref[...]
        hstb[...] = h0b_ref[...]

    ref_ref[...] = rstf[...]
    reb_ref[...] = rstb[...]
    hef_ref[...] = hstf[...]
    heb_ref[...] = hstb[...]
    bd = bd_ref[...]
    rows = rvf_ref.shape[0]
    lff, kf = _hgrn_gate(zf_ref[...], lb_ref[0:1, :])
    lfb, kb = _hgrn_gate(zb_ref[...], lb_ref[1:2, :])
    cumf = _sum_rows(tril_ref[...], lff)
    cumb = _sum_rows(triu_ref[...], lfb)
    _segment_state_step(hstf, kf, hvf_ref[...], cumf, cumf[rows - 1:rows, :], bd)
    _segment_state_step(hstb, kb, hvb_ref[...], cumb, cumb[0:1, :], bd)
    _, kf = _ret_operands(True, rkf_ref[...], rkf_ref[...], cf_ref[...], sf_ref[...])
    _, kb = _ret_operands(True, rkb_ref[...], rkb_ref[...], cb_ref[...], sb_ref[...])
    i = _row_index(rows).astype(F32)
    lgf, lgb = lg_ref[0:1, :], lg_ref[1:2, :]
    _segment_state_step(rstf, kf, rvf_ref[...], (i + 1.0) * lgf, rows * lgf, bd)
    _segment_state_step(rstb, kb, rvb_ref[...], (rows - i) * lgb, rows * lgb, bd)


def _segment_states(proj, n_b, seq, ret_s0, hgrn_s0, lg_rows, lb_rows, consts, tables):
    ns = seq // SEG
    blk = (SEG, GROUP)

    def fwd(cb):
        return pl.BlockSpec(blk, lambda i, s, cb=cb: (i * ns + s, cb))

    def bwd(cb):
        return pl.BlockSpec(blk, lambda i, s, cb=cb: (i * ns + ns - 1 - s, cb))

    tf = pl.BlockSpec((SEG, LANES), lambda i, s: (s, 0))
    tb = pl.BlockSpec((SEG, LANES), lambda i, s: (ns - 1 - s, 0))
    st_spec = pl.BlockSpec((None, GROUP, GROUP), lambda i, s: (i, 0, 0))
    ef_spec = pl.BlockSpec((None, None, GROUP, GROUP), lambda i, s: (i, s, 0, 0))
    eb_spec = pl.BlockSpec((None, None, GROUP, GROUP), lambda i, s: (i, ns - 1 - s, 0, 0))
    e_shape = _sds((n_b, ns, GROUP, GROUP), F32)
    return pl.pallas_call(
        _states_body,
        grid=(n_b, ns),
        in_specs=[fwd(CB_RK), fwd(CB_RV), bwd(CB_RK), bwd(CB_RV), tf, tf, tb, tb, _const_spec((2, GROUP)),
                  fwd(CB_HZF), fwd(CB_HI), bwd(CB_HZB), bwd(CB_HI), _const_spec((2, GROUP)),
                  _const_spec((SEG, SEG)), _const_spec((SEG, SEG)),
                  st_spec, st_spec, st_spec, st_spec, _const_spec((GROUP, GROUP))],
        out_specs=[ef_spec, eb_spec, ef_spec, eb_spec],
        out_shape=[e_shape] * 4,
        scratch_shapes=[pltpu.VMEM((GROUP, GROUP), F32)] * 4,
        compiler_params=_cparams("parallel", "arbitrary"),
        name="seg_states",
    )(proj, proj, proj, proj, tables[0], tables[1], tables[0], tables[1], lg_rows,
      proj, proj, proj, proj, lb_rows, consts["tril_seg"], consts["triu_seg"],
      *ret_s0, *hgrn_s0, consts["bd"])


def _ret_out_body(rope, has_state, emit_state, *refs):
    rq_ref, rk_ref, rv_ref, rg_ref = refs[:4]
    refs = refs[4:]
    cos = sin = None
    if rope:
        cos, sin = refs[0][...], refs[1][...]
        refs = refs[2:]
    lgs_ref, lg_ref, gain_ref, g64_ref = refs[:4]
    refs = refs[4:]
    if has_state:
        ef_ref, eb_ref = refs[:2]
        refs = refs[2:]
    o_ref = refs[0]
    mask_ref = refs[-1]
    rows = rq_ref.shape[0]

    @pl.when((pl.program_id(0) == 0) & (pl.program_id(1) == 0))
    def _():
        d = (lax.broadcasted_iota(jnp.int32, (rows, rows), 0)
             - lax.broadcasted_iota(jnp.int32, (rows, rows), 1)).astype(F32)
        for h in range(N_HEADS):
            mask_ref[h * rows:(h + 1) * rows, :] = (
                jnp.where(d >= 0, jnp.exp(jnp.maximum(d, 0.0) * lgs_ref[0, h]), 0.0)
                + jnp.where(d <= 0, jnp.exp(jnp.maximum(-d, 0.0) * lgs_ref[1, h]), 0.0))

    q, k = _ret_operands(rope, rq_ref[...], rk_ref[...], cos, sin)
    v = rv_ref[...]
    kb16, vb16 = k.astype(BF16), v.astype(BF16)
    s4 = _dot_nt(_stack_heads(q.astype(BF16)), kb16)
    o = _unstack_heads(_dot((s4 * mask_ref[...]).astype(BF16), vb16), rows)
    i = _row_index(rows).astype(F32)
    lgf, lgb = lg_ref[0:1, :], lg_ref[1:2, :]
    if has_state:
        o = o + _dot_nt((q * jnp.exp((i + 1.0) * lgf)).astype(BF16), ef_ref[...].astype(BF16))
        o = o + _dot_nt((q * jnp.exp((rows - i) * lgb)).astype(BF16), eb_ref[...].astype(BF16))
    y = _group_rms(o, g64_ref[...], HEAD_DIM, gain_ref[...]) * _silu(rg_ref[...])
    o_ref[...] = y.astype(BF16)
    if emit_state:
        sf_ref, sb_ref = refs[1:3]
        sf_ref[...] = _head_blocks(_dot_tn((k * jnp.exp((rows - 1.0 - i) * lgf)).astype(BF16), vb16))
        sb_ref[...] = _head_blocks(_dot_tn((k * jnp.exp(i * lgb)).astype(BF16), vb16))


def _ret_out(proj, n_b, seq, lg_smem, lg_rows, gain, consts, tables, states, emit_state):
    ns = seq // SEG
    rope = tables is not None
    blk = (SEG, GROUP)

    def col(cb):
        return pl.BlockSpec(blk, lambda i, s, cb=cb: (i * ns + s, cb))

    in_specs = [col(CB_RQ), col(CB_RK), col(CB_RV), col(CB_RG)]
    args = [proj] * 4
    if rope:
        in_specs += [pl.BlockSpec((SEG, LANES), lambda i, s: (s, 0))] * 2
        args += list(tables)
    in_specs += [pl.BlockSpec(memory_space=pltpu.SMEM), _const_spec((2, GROUP)), _const_spec((1, GROUP)),
                 _const_spec((GROUP, GROUP))]
    args += [lg_smem, lg_rows, gain, consts["g64"]]
    if states is not None:
        in_specs += [pl.BlockSpec((None, None, GROUP, GROUP), lambda i, s: (i, s, 0, 0))] * 2
        args += list(states)
    out_specs = [pl.BlockSpec(blk, lambda i, s: (i * ns + s, 0))]
    out_shape = [_sds((n_b * seq, GROUP), BF16)]
    if emit_state:
        assert ns == 1 and states is None
        out_specs += [pl.BlockSpec((None, HEAD_DIM, GROUP), lambda i, s: (i, 0, 0))] * 2
        out_shape += [_sds((n_b, HEAD_DIM, GROUP), F32)] * 2
    return pl.pallas_call(
        functools.partial(_ret_out_body, rope, states is not None, emit_state),
        grid=(n_b, ns),
        in_specs=in_specs, out_specs=out_specs, out_shape=out_shape,
        scratch_shapes=[pltpu.VMEM((N_HEADS * SEG, SEG), F32)],
        compiler_params=_cparams("arbitrary", "arbitrary"),
        name="ret_out_ctx" if states is not None else "ret_out",
    )(*args)


PAIR_LEVELS_MATMUL = (2, 4, 8)
PAIR_LEVELS_CUMSUM = (16, 32, 64, 128, 256)
PAIR_FOLD = 64


def _pair_level_matrix(n, forward):
    mats = []
    j = np.arange(n)[None, :]
    t = np.arange(n)[:, None]
    for g in PAIR_LEVELS_MATMUL:
        half = g // 2
        pos = t % g
        if forward:
            mid = t - pos + half - 1
            m = np.where(pos >= half, (j > mid) & (j <= t), (j > t) & (j <= mid))
        else:
            mid = t - pos + half
            m = np.where(pos < half, (j >= t) & (j < mid), (j >= mid) & (j < t))
        mats.append(m)
    return jnp.asarray(np.concatenate(mats, axis=0), BF16)


def _pair_decays(lf, cs, small_mat, forward):
    rows = lf.shape[0]
    hi = lf.astype(BF16)
    lo = (lf - hi.astype(F32)).astype(BF16)
    d = _dot(small_mat, hi) + _dot(small_mat, lo)
    out = [jnp.exp(d[i * rows:(i + 1) * rows]) for i in range(len(PAIR_LEVELS_MATMUL))]
    for g in PAIR_LEVELS_CUMSUM:
        half = g // 2
        pieces = []
        for grp in range(rows // g):
            mid = grp * g + (half - 1 if forward else half)
            pieces.append(jnp.broadcast_to(cs[mid:mid + 1, :], (g, cs.shape[1])))
        diff = cs - (pieces[0] if len(pieces) == 1 else jnp.concatenate(pieces, axis=0))
        out.append(jnp.exp(jnp.minimum(diff, -diff)))
    return out


def _hgrn_pair_weights(q, kf, kb, decays_f, decays_b):
    rows = q.shape[0]
    nfold = rows // PAIR_FOLD
    levels = PAIR_LEVELS_MATMUL + PAIR_LEVELS_CUMSUM
    pos = _row_index(rows)
    t_loc = lax.broadcasted_iota(jnp.int32, (PAIR_FOLD, GROUP), 0)
    s_loc = lax.broadcasted_iota(jnp.int32, (PAIR_FOLD, GROUP), 1) % PAIR_FOLD
    r_big = lax.broadcasted_iota(jnp.int32, (N_HEADS * rows, rows), 0) % rows
    c_big = lax.broadcasted_iota(jnp.int32, (N_HEADS * rows, rows), 1)
    folded = [None] * nfold
    stacked = None
    for g, ef, eb in reversed(list(zip(levels, decays_f, decays_b))):
        right = (pos % g) >= (g // 2)
        qs = (q * jnp.where(right, ef, eb)).astype(BF16)
        ks = (jnp.where(right, kb, kf) * jnp.where(right, eb, ef)).astype(BF16)
        if g > PAIR_FOLD:
            s = _dot_nt(_stack_heads(qs), ks)
            stacked = s if stacked is None else jnp.where((r_big // g) == (c_big // g), s, stacked)
        else:
            same = None if g == PAIR_FOLD else (t_loc // g) == (s_loc // g)
            for j in range(nfold):
                r0, r1 = j * PAIR_FOLD, (j + 1) * PAIR_FOLD
                s = _dot_nt(qs[r0:r1], _stack_heads(ks[r0:r1]))
                folded[j] = s if same is None else jnp.where(same, s, folded[j])
    stacked = jnp.where((r_big // PAIR_FOLD) == (c_big // PAIR_FOLD), 0.0, stacked)
    folded = [jnp.where(t_loc == s_loc, 0.0, f) for f in folded]
    return folded, stacked


def _hgrn_out_body(has_state, emit_state, *refs):
    (q_ref, zf_ref, zb_ref, v_ref, g_ref, lb_ref, gain_ref, g64_ref,
     tril_ref, triu_ref, smf_ref, smb_ref) = refs[:12]
    refs = refs[12:]
    g64 = g64_ref[...]
    if has_state:
        ef, eb = refs[0][...], refs[1][...]
        refs = refs[2:]
    o_ref = refs[0]
    q, v = q_ref[...], v_ref[...]
    rows = q.shape[0]
    lff, kf = _hgrn_gate(zf_ref[...], lb_ref[0:1, :])
    lfb, kb = _hgrn_gate(zb_ref[...], lb_ref[1:2, :])
    csf = _sum_rows(tril_ref[...], lff)
    csb = _sum_rows(triu_ref[...], lfb)
    folded, stacked = _hgrn_pair_weights(q, kf, kb, _pair_decays(lff, csf, smf_ref[...], True),
                                         _pair_decays(lfb, csb, smb_ref[...], False))
    vb = v.astype(BF16)
    o = _dot((q * (kf + kb)).astype(BF16), g64) * v
    o = o + _unstack_heads(_dot(stacked.astype(BF16), vb), rows)
    tiles = []
    for j in range(rows // PAIR_FOLD):
        r0, r1 = j * PAIR_FOLD, (j + 1) * PAIR_FOLD
        tiles.append(_dot(folded[j].astype(BF16), _stack_heads(vb[r0:r1])))
    o = o + jnp.concatenate(tiles, axis=0)
    if has_state:
        o = o + _dot_nt((q * jnp.exp(csf)).astype(BF16), ef.astype(BF16))
        o = o + _dot_nt((q * jnp.exp(csb)).astype(BF16), eb.astype(BF16))
    y = _group_rms(o, g64, HEAD_DIM, gain_ref[...]) * _silu(g_ref[...])
    o_ref[...] = y.astype(BF16)
    if emit_state:
        totf, totb = csf[rows - 1:rows, :], csb[0:1, :]
        refs[1][...] = _head_blocks(_dot_tn((kf * jnp.exp(totf - csf)).astype(BF16), vb))
        refs[2][...] = _head_blocks(_dot_tn((kb * jnp.exp(totb - csb)).astype(BF16), vb))


def _hgrn_out(proj, n_b, seq, lb_rows, gain, consts, states, emit_state):
    ns = seq // SEG
    blk = (SEG, GROUP)

    def col(cb):
        return pl.BlockSpec(blk, lambda i, s, cb=cb: (i * ns + s, cb))

    n_small = len(PAIR_LEVELS_MATMUL) * SEG
    in_specs = [col(CB_HQ), col(CB_HZF), col(CB_HZB), col(CB_HI), col(CB_HG),
                _const_spec((2, GROUP)), _const_spec((1, GROUP)), _const_spec((GROUP, GROUP)),
                _const_spec((SEG, SEG)), _const_spec((SEG, SEG)),
                _const_spec((n_small, SEG)), _const_spec((n_small, SEG))]
    args = [proj] * 5 + [lb_rows, gain, consts["g64"],
                         consts["tril_seg"], consts["triu_seg"], consts["pair_f"], consts["pair_b"]]
    if states is not None:
        in_specs += [pl.BlockSpec((None, None, GROUP, GROUP), lambda i, s: (i, s, 0, 0))] * 2
        args += list(states)
    out_specs = [pl.BlockSpec(blk, lambda i, s: (i * ns + s, 0))]
    out_shape = [_sds((n_b * seq, GROUP), BF16)]
    if emit_state:
        assert ns == 1 and states is None
        out_specs += [pl.BlockSpec((None, HEAD_DIM, GROUP), lambda i, s: (i, 0, 0))] * 2
        out_shape += [_sds((n_b, HEAD_DIM, GROUP), F32)] * 2
    return pl.pallas_call(
        functools.partial(_hgrn_out_body, states is not None, emit_state),
        grid=(n_b, ns),
        in_specs=in_specs, out_specs=out_specs, out_shape=out_shape,
        compiler_params=_cparams("parallel", "parallel"),
        name="hgrn_out_ctx" if states is not None else "hgrn_out",
    )(*args)


def _ffn_body(x_ref, oa_ref, ob_ref, oc_ref, od_ref, wo_ref, g1_ref, n2_ref, sc_ref, sh_ref, g2_ref,
              wg_ref, wu_ref, wd_ref, y_ref, x1_ref, h_ref, acc_ref):
    j = pl.program_id(1)

    @pl.when(j == 0)
    def _():
        mix = _dot(oa_ref[...], wo_ref[0:GROUP, :])
        mix += _dot_tn(ob_ref[...], wo_ref[GROUP:2 * GROUP, :])
        mix += _dot_tn(oc_ref[...], wo_ref[2 * GROUP:3 * GROUP, :])
        mix += _dot(od_ref[...], wo_ref[3 * GROUP:4 * GROUP, :])
        x1 = x_ref[...] + g1_ref[...] * mix
        x1_ref[...] = x1
        h = x1 * lax.rsqrt(jnp.mean(x1 * x1, axis=-1, keepdims=True) + EPS) * n2_ref[...]
        h_ref[...] = (h * (1.0 + sc_ref[...]) + sh_ref[...]).astype(BF16)
        acc_ref[...] = jnp.zeros_like(acc_ref)

    h = h_ref[...]
    a = _silu(_dot(h, wg_ref[...])) * _dot(h, wu_ref[...])
    acc_ref[...] += _dot(a.astype(BF16), wd_ref[...])

    @pl.when(j == pl.num_programs(1) - 1)
    def _():
        y_ref[...] = x1_ref[...] + g2_ref[...] * acc_ref[...]


def _outproj_ffn(x, mixes, w_out, g1, n2, sc2, sh2, g2, w_in, w_dn, rows_per_mod, mod_row0, tm=FFN_TM, th=FFN_TH):
    t, d = x.shape
    hid = w_dn.shape[0]
    nh = hid // th
    mod_map = _mod_row_map(tm, rows_per_mod, mod_row0)
    mod2 = lambda i, j: mod_map(i)
    row = lambda i, j: (i, 0)
    mix_spec = pl.BlockSpec((tm, GROUP), row)
    mix_t_spec = pl.BlockSpec((GROUP, tm), lambda i, j: (0, i))
    mod_spec = pl.BlockSpec((None, 1, d), mod2)
    return pl.pallas_call(
        _ffn_body,
        grid=(t // tm, nh),
        in_specs=[pl.BlockSpec((tm, d), row), mix_spec, mix_t_spec, mix_t_spec, mix_spec]
                 + [_const_spec((d, d)), mod_spec, _const_spec((1, d)), mod_spec, mod_spec, mod_spec,
                    pl.BlockSpec((d, th), lambda i, j: (0, j)),
                    pl.BlockSpec((d, th), lambda i, j: (0, nh + j)),
                    pl.BlockSpec((th, d), lambda i, j: (j, 0))],
        out_specs=pl.BlockSpec((tm, d), row),
        out_shape=_sds((t, d), F32),
        scratch_shapes=[pltpu.VMEM((tm, d), F32), pltpu.VMEM((tm, d), BF16), pltpu.VMEM((tm, d), F32)],
        compiler_params=_cparams("parallel", "arbitrary"),
        name="outproj_ffn",
    )(x, *mixes, w_out, g1, n2, sc2, sh2, g2, w_in, w_in, w_dn)


def _lane_rows(per_head):
    return jnp.repeat(per_head.astype(F32), HEAD_DIM, axis=1)


def _states_to_lanes(s):
    eye = jnp.eye(N_HEADS, dtype=F32)
    st = jnp.swapaxes(s.astype(F32), -1, -2)[..., :, :, None, :] * eye[:, None, :, None]
    return st.reshape(s.shape[:-3] + (GROUP, GROUP))


def _with_ones_rows(vt):
    ones = jnp.ones(vt.shape[:-2] + (V_ROWS - HEAD_DIM, vt.shape[-1]), vt.dtype)
    return jnp.concatenate([vt, ones], axis=-2)


def _lanes_to_states(sf, sb):
    s = jnp.stack([sf, sb], axis=1)
    return s.reshape(s.shape[:3] + (N_HEADS, HEAD_DIM)).transpose(0, 1, 3, 2, 4)


def kernel(x_prompt, x_sample, state_ret, cache_win_k, cache_win_v, cache_diff_k, cache_diff_v, state_hgrn, c, c_ctx, norm1_g, norm2_g, w_ada, b_ada, w_in, ret_decay, ret_norm_g, win_q_norm, win_k_norm, win_sink, diff_q_norm, diff_k_norm, diff_lambda, diff_norm_g, hgrn_lb_logits, hgrn_norm_g, w_out, w_ffn_in, w_ffn_out):
    n_p, seq_p, d = x_prompt.shape
    n_s, seq_s, _ = x_sample.shape
    depth = w_in.shape[0]
    t_s, t_p = n_s * seq_s, n_p * seq_p
    assert seq_p == SEG and seq_s % SEG == 0 and d == N_HEADS * GROUP

    tril_seg, triu_seg = _cumsum_matrices(SEG)
    consts = dict(g64=_group_matrix(GROUP, HEAD_DIM), g32=_group_matrix(GROUP, DIFF_DIM),
                  bd=_group_matrix(GROUP, HEAD_DIM).astype(F32), dupk=_kv_dup_matrix(),
                  tril_seg=tril_seg, triu_seg=triu_seg,
                  pair_f=_pair_level_matrix(SEG, True), pair_b=_pair_level_matrix(SEG, False))
    tab64 = _rope_tables(seq_s, HEAD_DIM)
    tab32 = _rope_tables(seq_s, DIFF_DIM)

    n_rows = -(-(n_s + 1) // 8) * 8
    cond = jnp.zeros((n_rows, d), F32).at[:n_s].set(c).at[n_s].set(c_ctx)
    mod = _adaln(cond, w_ada, b_ada).reshape(depth, n_rows, 6, 1, d)

    lb_p = jax.nn.softmax(hgrn_lb_logits.astype(F32), axis=0)
    lb_all = jnp.cumsum(lb_p, axis=0) - lb_p
    log_gamma = -jnp.exp(ret_decay.astype(F32))

    ctx_states = _states_to_lanes(jnp.stack([state_ret, state_hgrn], axis=2))

    xs = x_sample.reshape(t_s, d)
    xp = x_prompt.reshape(t_p, d)
    new_ret, new_wk, new_wv, new_dk, new_dv, new_hg = [], [], [], [], [], []
    for l in range(depth):
        lam_init = 0.8 - 0.6 * math.exp(-0.3 * l)
        sh1, sc1, g1, sh2, sc2, g2 = [mod[l, :, i] for i in range(6)]
        w_in_l, w_out_l = w_in[l].astype(BF16), w_out[l].astype(BF16)
        w_up_l, w_dn_l = w_ffn_in[l].astype(BF16), w_ffn_out[l].astype(BF16)
        n1, n2 = norm1_g[l][None], norm2_g[l][None]
        gains = (jnp.tile(win_q_norm[l], N_HEADS)[None], jnp.tile(win_k_norm[l], KV_HEADS)[None],
                 jnp.tile(diff_q_norm[l], 2 * N_HEADS)[None], jnp.tile(diff_k_norm[l], 2 * N_HEADS)[None])
        sink = win_sink[l].astype(F32)
        lq1, lk1, lq2, lk2 = diff_lambda[l].astype(F32)
        lam = jnp.exp(jnp.sum(lq1 * lk1)) - jnp.exp(jnp.sum(lq2 * lk2)) + lam_init
        lam2 = jnp.stack([lam, jnp.asarray(1.0 - lam_init, F32)]).reshape(1, 2)
        dgain = diff_norm_g[l].astype(F32)[:, None]
        lg = log_gamma[l]
        lg_rows = _lane_rows(lg)
        rgain = ret_norm_g[l][None]
        lb_rows = lb_all[l]
        hgain = hgrn_norm_g[l][None]

        proj = _inproj(xs, n1, sc1, sh1, w_in_l, seq_s, 0)
        wq, wk, wvt, dq, dk, dvt = _prep(proj, gains, consts, tab64 + tab32, False)
        ctx_wk = jnp.repeat(cache_win_k[:, l].transpose(0, 2, 1, 3), N_HEADS // KV_HEADS, axis=2)
        ctx_wk = ctx_wk.reshape(n_s, -1, GROUP).astype(BF16)
        ctx_wvt = _with_ones_rows(cache_win_v[:, l].swapaxes(-1, -2).astype(BF16))
        o_win = _win_sample(wq, wk, wvt, ctx_wk, ctx_wvt, jnp.repeat(sink * LOG2_E, WIN_BLOCK)[None])
        ctx_dk = cache_diff_k[:, l].transpose(0, 3, 1, 2, 4).reshape(n_s, -1, GROUP).astype(BF16)
        ctx_dvt = _with_ones_rows(cache_diff_v[:, l].swapaxes(-1, -2).astype(BF16))
        o_diff = _diff_attention(dq, dk, dvt, (ctx_dk, ctx_dvt), seq_s, lam2, dgain, DIFF_HEADS_LATENT)
        entry = _segment_states(proj, n_s, seq_s, (ctx_states[:, l, 0, 0], ctx_states[:, l, 0, 1]),
                                (ctx_states[:, l, 1, 0], ctx_states[:, l, 1, 1]), lg_rows, lb_rows, consts, tab64)
        (o_ret,) = _ret_out(proj, n_s, seq_s, lg, lg_rows, rgain, consts, tab64, entry[:2], False)
        (o_h,) = _hgrn_out(proj, n_s, seq_s, lb_rows, hgain, consts, entry[2:], False)
        xs = _outproj_ffn(xs, (o_ret, o_win, o_diff, o_h), w_out_l, g1, n2, sc2, sh2, g2, w_up_l, w_dn_l, seq_s, 0)

        proj = _inproj(xp, n1, sc1, sh1, w_in_l, t_p, n_s)
        wq, wk, wvt, dq, dk, dvt, wk_c, wv_c, dk_c, dv_c = _prep(proj, gains, consts, None, True, tm=seq_p)
        o_win = _win_prompt(wq, wk, wvt, seq_p, jnp.repeat(sink * LOG2_E, seq_p)[None])
        o_diff = _diff_attention(dq, dk, dvt, None, seq_p, lam2, dgain, N_HEADS)
        o_ret, rsf, rsb = _ret_out(proj, n_p, seq_p, lg, lg_rows, rgain, consts, None, None, True)
        o_h, hsf, hsb = _hgrn_out(proj, n_p, seq_p, lb_rows, hgain, consts, None, True)
        xp = _outproj_ffn(xp, (o_ret, o_win, o_diff, o_h), w_out_l, g1, n2, sc2, sh2, g2, w_up_l, w_dn_l, t_p, n_s)

        new_ret.append(_lanes_to_states(rsf, rsb))
        new_hg.append(_lanes_to_states(hsf, hsb))
        new_wk.append(wk_c)
        new_wv.append(wv_c)
        new_dk.append(dk_c)
        new_dv.append(dv_c)

    return (xp.reshape(n_p, seq_p, d), xs.reshape(n_s, seq_s, d),
            jnp.stack(new_ret, axis=1), jnp.stack(new_wk, axis=1), jnp.stack(new_wv, axis=1),
            jnp.stack(new_dk, axis=1), jnp.stack(new_dv, axis=1), jnp.stack(new_hg, axis=1))
```

```python
import functools
import math

import numpy as np
import jax
import jax.numpy as jnp
from jax import lax
from jax.experimental import pallas as pl
from jax.experimental.pallas import tpu as pltpu

F32 = jnp.float32
BF16 = jnp.bfloat16

GROUP = 256
HEAD_DIM = 64
N_HEADS = 4
KV_HEADS = 2
DIFF_DIM = 32
WINDOW = 128
WIN_BLOCK = 128
GRID_W = 64
ROPE_BASE = 10000.0
EPS = 1e-6
MASK_VALUE = -1e30
SEG = 256
LANES = 128
BF16_SUBLANES = 16
LOG2_E = math.log2(math.e)
V_ROWS = HEAD_DIM + BF16_SUBLANES
DIFF_LOOKAHEAD = 2

INPROJ_TM = 512
PREP_TM = 512
FFN_TM = 1024
FFN_TH = 256
DIFF_TQ = 512
DIFF_CK = 512
DIFF_HEADS_LATENT = 2
V7X_VMEM_LIMIT_BYTES = 56 * 1024 * 1024

(CB_RQ, CB_RK, CB_RV, CB_RG, CB_WQ, CB_WKV, CB_DQ, CB_DK, CB_DV,
 CB_HQ, CB_HZF, CB_HZB, CB_HI, CB_HG) = range(14)

_NT = (((1,), (1,)), ((), ()))
_TN = (((0,), (0,)), ((), ()))


def _sds(shape, dtype):
    return jax.ShapeDtypeStruct(shape, dtype)


def _cparams(*sem):
    return pltpu.CompilerParams(dimension_semantics=sem, vmem_limit_bytes=V7X_VMEM_LIMIT_BYTES)


def _const_spec(shape):
    return pl.BlockSpec(shape, lambda *_: (0,) * len(shape))


def _dot(a, b):
    return jnp.dot(a, b, preferred_element_type=F32)


def _dot_nt(a, b):
    return lax.dot_general(a, b, _NT, preferred_element_type=F32)


def _dot_tn(a, b):
    return lax.dot_general(a, b, _TN, preferred_element_type=F32)


def _silu(x):
    return x * jax.nn.sigmoid(x)


def _group_matrix(width, gsize):
    i = np.arange(width)
    return jnp.asarray((i[:, None] // gsize) == (i[None, :] // gsize), BF16)


def _cumsum_matrices(n):
    r, c = np.arange(n)[:, None], np.arange(n)[None, :]
    return jnp.asarray(c <= r, BF16), jnp.asarray(c >= r, BF16)


def _kv_dup_matrix():
    d = np.zeros((KV_HEADS * HEAD_DIM, GROUP), np.float32)
    for h in range(N_HEADS):
        kv = h // (N_HEADS // KV_HEADS)
        for j in range(HEAD_DIM):
            d[kv * HEAD_DIM + j, h * HEAD_DIM + j] = 1.0
    return jnp.asarray(d, BF16)


def _rope_tables(n_tokens, head_dim):
    d = head_dim // 2
    half = d // 2
    inv = ROPE_BASE ** (-jnp.arange(half, dtype=F32) / half)
    t = jnp.arange(n_tokens)
    row = (t // GRID_W).astype(F32)
    col = (t % GRID_W).astype(F32)
    j = np.arange(LANES) % head_dim
    w = j % d
    use_row = jnp.asarray((j // d) == 0)
    pos = jnp.where(use_row[None, :], row[:, None], col[:, None])
    ang = pos * inv[w % half][None, :]
    sign = jnp.asarray(np.where(w >= half, 1.0, -1.0), F32)
    return jnp.cos(ang), jnp.sin(ang) * sign[None, :]


def _group_rms(x, g_mat, gsize, gain):
    ss = _dot((x * x).astype(BF16), g_mat)
    return x * lax.rsqrt(ss * (1.0 / gsize) + EPS) * gain


def _rope(y, cos, sin, half):
    lane = lax.broadcasted_iota(jnp.int32, (1, LANES), 1)
    second = (lane % (2 * half)) >= half
    outs = []
    for p in range(y.shape[1] // LANES):
        z = y[:, p * LANES:(p + 1) * LANES]
        partner = jnp.where(second, pltpu.roll(z, half, 1), pltpu.roll(z, LANES - half, 1))
        outs.append(z * cos + partner * sin)
    return outs[0] if len(outs) == 1 else jnp.concatenate(outs, axis=1)


def _head_of_lane(width=GROUP):
    return lax.broadcasted_iota(jnp.int32, (1, width), 1) // HEAD_DIM


def _stack_heads(q, n_heads=N_HEADS):
    head = _head_of_lane()
    zero = jnp.zeros_like(q)
    return jnp.concatenate([jnp.where(head == h, q, zero) for h in range(n_heads)], axis=0)


def _unstack_heads(o4, rows):
    head = _head_of_lane()
    out = jnp.where(head == 0, o4[:rows], 0.0)
    for h in range(1, N_HEADS):
        out = out + jnp.where(head == h, o4[h * rows:(h + 1) * rows], 0.0)
    return out


def _head_blocks(s):
    head = _head_of_lane()
    out = jnp.where(head == 0, s[:HEAD_DIM], 0.0)
    for h in range(1, N_HEADS):
        out = out + jnp.where(head == h, s[h * HEAD_DIM:(h + 1) * HEAD_DIM], 0.0)
    return out


def _split3(x):
    hi = x.astype(BF16)
    r = x - hi.astype(F32)
    mid = r.astype(BF16)
    lo = (r - mid.astype(F32)).astype(BF16)
    return hi, mid, lo


def _sum_rows(m01, x):
    hi, mid, lo = _split3(x)
    return _dot(m01, hi) + _dot(m01, mid) + _dot(m01, lo)


def _adaln_body(c_ref, w_ref, b_ref, o_ref):
    c = c_ref[...]
    o_ref[...] = jnp.dot(_silu(c), w_ref[...], preferred_element_type=F32,
                         precision=lax.Precision.HIGHEST) + b_ref[...]


def _adaln(cond, w_ada, b_ada):
    depth, d, n = w_ada.shape
    rows = cond.shape[0]
    tn = n // 4
    return pl.pallas_call(
        _adaln_body,
        grid=(depth, n // tn),
        in_specs=[pl.BlockSpec((rows, d), lambda l, j: (0, 0)),
                  pl.BlockSpec((None, d, tn), lambda l, j: (l, 0, j)),
                  pl.BlockSpec((None, 1, tn), lambda l, j: (l, 0, j))],
        out_specs=pl.BlockSpec((None, rows, tn), lambda l, j: (l, 0, j)),
        out_shape=_sds((depth, rows, n), F32),
        compiler_params=_cparams("parallel", "parallel"),
        name="adaln",
    )(cond, w_ada, b_ada.reshape(depth, 1, n))


def _inproj_body(x_ref, g_ref, sc_ref, sh_ref, w_ref, o_ref):
    x = x_ref[...]
    h = x * lax.rsqrt(jnp.mean(x * x, axis=-1, keepdims=True) + EPS) * g_ref[...]
    h = h * (1.0 + sc_ref[...]) + sh_ref[...]
    o_ref[...] = _dot(h.astype(BF16), w_ref[...])


def _mod_row_map(tm, rows_per_mod, mod_row0):
    return lambda i: (mod_row0 + (i * tm) // rows_per_mod, 0, 0)


def _inproj(x, g, sc, sh, w, rows_per_mod, mod_row0, tm=INPROJ_TM):
    t, d = x.shape
    n = w.shape[1]
    mod_map = _mod_row_map(tm, rows_per_mod, mod_row0)
    return pl.pallas_call(
        _inproj_body,
        grid=(t // tm,),
        in_specs=[pl.BlockSpec((tm, d), lambda i: (i, 0)),
                  _const_spec((1, d)),
                  pl.BlockSpec((None, 1, d), mod_map),
                  pl.BlockSpec((None, 1, d), mod_map),
                  _const_spec((d, n))],
        out_specs=pl.BlockSpec((tm, n), lambda i: (i, 0)),
        out_shape=_sds((t, n), F32),
        compiler_params=_cparams("parallel"),
        name="inproj",
    )(x, g, sc, sh, w)


def _prep_body(rope, emit_f32, *refs):
    (wq_ref, wkv_ref, dq_ref, dk_ref, dv_ref, qn_ref, kn_ref, dqn_ref, dkn_ref, g64_ref, g32_ref,
     dup_ref) = refs[:12]
    refs = refs[12:]
    if rope:
        c64_ref, s64_ref, c32_ref, s32_ref = refs[:4]
        refs = refs[4:]
    wq_o, wk_o, wvt_o, dq_o, dk_o, dvt_o = refs[:6]
    g64 = g64_ref[...]
    g32 = g32_ref[...]
    tm = dv_ref.shape[0]

    wq = _group_rms(wq_ref[...], g64, HEAD_DIM, qn_ref[...])
    wkv = wkv_ref[...]
    wk = _group_rms(wkv[:, :LANES], g64[:LANES, :LANES], HEAD_DIM, kn_ref[...])
    dq = _group_rms(dq_ref[...], g32, DIFF_DIM, dqn_ref[...])
    dk = _group_rms(dk_ref[...], g32, DIFF_DIM, dkn_ref[...])
    if rope:
        c64, s64, c32, s32 = c64_ref[...], s64_ref[...], c32_ref[...], s32_ref[...]
        wq = _rope(wq, c64, s64, HEAD_DIM // 4)
        wk = _rope(wk, c64, s64, HEAD_DIM // 4)
        dq = _rope(dq, c32, s32, DIFF_DIM // 4)
        dk = _rope(dk, c32, s32, DIFF_DIM // 4)
    wq_o[...] = (wq * (HEAD_DIM ** -0.5 * LOG2_E)).astype(BF16)
    wk_o[...] = _dot(wk.astype(BF16), dup_ref[...]).astype(BF16)
    wvt_o[:, :HEAD_DIM, :] = wkv[:, LANES:].T.reshape(KV_HEADS, HEAD_DIM, tm).astype(BF16)
    wvt_o[:, HEAD_DIM:, :] = jnp.ones((KV_HEADS, V_ROWS - HEAD_DIM, tm), BF16)
    dq_o[...] = (dq * (DIFF_DIM ** -0.5 * LOG2_E)).astype(BF16)
    dk_o[...] = dk.astype(BF16)
    dvt_o[:, :HEAD_DIM, :] = dv_ref[...].T.reshape(N_HEADS, HEAD_DIM, tm).astype(BF16)
    dvt_o[:, HEAD_DIM:, :] = jnp.ones((N_HEADS, V_ROWS - HEAD_DIM, tm), BF16)
    if emit_f32:
        wk_c, wv_c, dk_c, dv_c = refs[6:10]
        dv = dv_ref[...]
        for kv in range(KV_HEADS):
            wk_c[kv] = wk[:, kv * HEAD_DIM:(kv + 1) * HEAD_DIM]
            wv_c[kv] = wkv[:, LANES + kv * HEAD_DIM:LANES + (kv + 1) * HEAD_DIM]
        for h in range(N_HEADS):
            dv_c[h] = dv[:, h * HEAD_DIM:(h + 1) * HEAD_DIM]
            for c in range(2):
                lane0 = h * HEAD_DIM + c * DIFF_DIM
                dk_c[h, c] = dk[:, lane0:lane0 + DIFF_DIM]


def _prep(proj, gains, consts, tables, emit_f32, tm=PREP_TM):
    rope = tables is not None
    rows = proj.shape[0]

    def col(cb):
        return pl.BlockSpec((tm, GROUP), lambda i, cb=cb: (i, cb))

    def vt(heads):
        return pl.BlockSpec((heads, V_ROWS, tm), lambda i: (0, 0, i))

    in_specs = [col(CB_WQ), col(CB_WKV), col(CB_DQ), col(CB_DK), col(CB_DV),
                _const_spec((1, GROUP)), _const_spec((1, LANES)), _const_spec((1, GROUP)), _const_spec((1, GROUP)),
                _const_spec((GROUP, GROUP)), _const_spec((GROUP, GROUP)), _const_spec((LANES, GROUP))]
    args = [proj] * 5 + list(gains) + [consts["g64"], consts["g32"], consts["dupk"]]
    if rope:
        seq = tables[0].shape[0]
        nper = seq // tm
        in_specs += [pl.BlockSpec((tm, LANES), lambda i: (i % nper, 0))] * 4
        args += list(tables)
    out = pl.BlockSpec((tm, GROUP), lambda i: (i, 0))
    tok = _sds((rows, GROUP), BF16)
    out_specs = [out, out, vt(KV_HEADS), out, out, vt(N_HEADS)]
    out_shape = [tok, tok, _sds((KV_HEADS, V_ROWS, rows), BF16), tok, tok, _sds((N_HEADS, V_ROWS, rows), BF16)]
    if emit_f32:
        nb = rows // tm
        out_specs += [pl.BlockSpec((None, KV_HEADS, tm, HEAD_DIM), lambda i: (i, 0, 0, 0)),
                      pl.BlockSpec((None, KV_HEADS, tm, HEAD_DIM), lambda i: (i, 0, 0, 0)),
                      pl.BlockSpec((None, N_HEADS, 2, tm, DIFF_DIM), lambda i: (i, 0, 0, 0, 0)),
                      pl.BlockSpec((None, N_HEADS, tm, HEAD_DIM), lambda i: (i, 0, 0, 0))]
        out_shape += [_sds((nb, KV_HEADS, tm, HEAD_DIM), F32), _sds((nb, KV_HEADS, tm, HEAD_DIM), F32),
                      _sds((nb, N_HEADS, 2, tm, DIFF_DIM), F32), _sds((nb, N_HEADS, tm, HEAD_DIM), F32)]
    return pl.pallas_call(
        functools.partial(_prep_body, rope, emit_f32),
        grid=(rows // tm,),
        in_specs=in_specs, out_specs=out_specs, out_shape=out_shape,
        compiler_params=_cparams("parallel"),
        name="prep_rope" if rope else "prep",
    )(*args)


def _win_body(banded, *refs):
    if banded:
        (q_ref, kp_ref, kc_ref, kn_ref, vp_ref, vc_ref, vn_ref, ck_ref, cvt_ref, sink_ref, o_ref) = refs
        k_refs, v_refs = (kp_ref, kc_ref, kn_ref), (vp_ref, vc_ref, vn_ref)
    else:
        q_ref, kc_ref, vc_ref, sink_ref, o_ref = refs
        k_refs, v_refs = (kc_ref,), (vc_ref,)
    tq = q_ref.shape[0]
    q4 = _stack_heads(q_ref[...])
    k_loc = k_refs[0][...] if len(k_refs) == 1 else jnp.concatenate([r[...] for r in k_refs], axis=0)
    s_loc = _dot_nt(k_loc, q4)
    sink = sink_ref[...]
    if banded:
        n = pl.program_id(1)
        seq = pl.num_programs(1) * WIN_BLOCK
        t = n * WIN_BLOCK + lax.broadcasted_iota(jnp.int32, s_loc.shape, 1) % WIN_BLOCK
        kpos = (n - 1) * WIN_BLOCK + lax.broadcasted_iota(jnp.int32, s_loc.shape, 0)
        valid = (kpos >= 0) & (kpos < seq) & (jnp.abs(t - kpos) <= WINDOW)
        s_loc = jnp.where(valid, s_loc, MASK_VALUE)
        s_ctx = _dot_nt(ck_ref[...], q4)
        m = jnp.maximum(jnp.maximum(s_loc.max(axis=0, keepdims=True), s_ctx.max(axis=0, keepdims=True)), sink)
        p_ctx = jnp.exp2(s_ctx - m).astype(BF16)
    else:
        m = jnp.maximum(s_loc.max(axis=0, keepdims=True), sink)
    p_loc = jnp.exp2(s_loc - m).astype(BF16)
    p_sink = jnp.exp2(sink - m)
    outs = []
    for h in range(N_HEADS):
        kv = h // (N_HEADS // KV_HEADS)
        c0, c1 = h * tq, (h + 1) * tq
        vt = v_refs[0][kv] if len(v_refs) == 1 else jnp.concatenate([r[kv] for r in v_refs], axis=1)
        acc = _dot(vt, p_loc[:, c0:c1])
        if banded:
            acc = acc + _dot(cvt_ref[kv], p_ctx[:, c0:c1])
        outs.append(acc[:HEAD_DIM] / (acc[HEAD_DIM:HEAD_DIM + 1] + p_sink[:, c0:c1]))
    o_ref[...] = jnp.concatenate(outs, axis=0).astype(BF16)


def _win_sample(wq, wk, wvt, ctx_k, ctx_vt, sink_row):
    b, p, _ = ctx_k.shape
    rows = wq.shape[0]
    nq = rows // b // WIN_BLOCK
    blk = (WIN_BLOCK, GROUP)
    vblk = (KV_HEADS, V_ROWS, WIN_BLOCK)
    prev = lambda i, n: i * nq + jnp.maximum(n - 1, 0)
    cur = lambda i, n: i * nq + n
    nxt = lambda i, n: i * nq + jnp.minimum(n + 1, nq - 1)
    return pl.pallas_call(
        functools.partial(_win_body, True),
        grid=(b, nq),
        in_specs=[pl.BlockSpec(blk, lambda i, n: (cur(i, n), 0)),
                  pl.BlockSpec(blk, lambda i, n: (prev(i, n), 0)),
                  pl.BlockSpec(blk, lambda i, n: (cur(i, n), 0)),
                  pl.BlockSpec(blk, lambda i, n: (nxt(i, n), 0)),
                  pl.BlockSpec(vblk, lambda i, n: (0, 0, prev(i, n))),
                  pl.BlockSpec(vblk, lambda i, n: (0, 0, cur(i, n))),
                  pl.BlockSpec(vblk, lambda i, n: (0, 0, nxt(i, n))),
                  pl.BlockSpec((None, p, GROUP), lambda i, n: (i, 0, 0)),
                  pl.BlockSpec((None, KV_HEADS, V_ROWS, p), lambda i, n: (i, 0, 0, 0)),
                  _const_spec((1, N_HEADS * WIN_BLOCK))],
        out_specs=pl.BlockSpec((GROUP, WIN_BLOCK), lambda i, n: (0, cur(i, n))),
        out_shape=_sds((GROUP, rows), BF16),
        compiler_params=_cparams("parallel", "parallel"),
        name="win_sample",
    )(wq, wk, wk, wk, wvt, wvt, wvt, ctx_k, ctx_vt, sink_row)


def _win_prompt(wq, wk, wvt, seq, sink_row):
    rows = wq.shape[0]
    blk = (seq, GROUP)
    return pl.pallas_call(
        functools.partial(_win_body, False),
        grid=(rows // seq,),
        in_specs=[pl.BlockSpec(blk, lambda i: (i, 0)), pl.BlockSpec(blk, lambda i: (i, 0)),
                  pl.BlockSpec((KV_HEADS, V_ROWS, seq), lambda i: (0, 0, i)),
                  _const_spec((1, N_HEADS * seq))],
        out_specs=pl.BlockSpec((GROUP, seq), lambda i: (0, i)),
        out_shape=_sds((GROUP, rows), BF16),
        compiler_params=_cparams("parallel"),
        name="win_prompt",
    )(wq, wk, wvt, sink_row)


def _diff_body(ck, has_ctx, heads, *refs):
    if has_ctx:
        q_ref, k_ref, vt_ref, ck_ref, cvt_ref, lam_ref, gain_ref, o_ref = refs
    else:
        q_ref, k_ref, vt_ref, lam_ref, gain_ref, o_ref = refs
    for j in range(heads):
        sources = [(k_ref, vt_ref.at[j])] + ([(ck_ref, cvt_ref.at[j])] if has_ctx else [])
        rows = slice(j * HEAD_DIM, (j + 1) * HEAD_DIM)
        o_ref[rows, :] = _diff_head(ck, pl.program_id(2) * heads + j, q_ref[...], sources,
                                    lam_ref, gain_ref[rows, :])


def _diff_head(ck, h, q, sources, lam_ref, gain):
    tq = q.shape[0]
    sub = lax.broadcasted_iota(jnp.int32, (1, GROUP), 1) // DIFF_DIM
    zero = jnp.zeros_like(q)
    q2 = jnp.concatenate([jnp.where(sub == 2 * h, q, zero), jnp.where(sub == 2 * h + 1, q, zero)], axis=0)
    chunks = []
    for kr, vr in sources:
        n_keys = kr.shape[0]
        step = min(ck, n_keys)
        chunks += [(kr, vr, c0, step) for c0 in range(0, n_keys, step)]

    def scores(i):
        kr, _, c0, step = chunks[i]
        return _dot_nt(kr[c0:c0 + step, :], q2)

    pending = [scores(i) for i in range(min(DIFF_LOOKAHEAD, len(chunks)))]
    m = acc = None
    for i, (_, vr, c0, step) in enumerate(chunks):
        s = pending.pop(0)
        if i + DIFF_LOOKAHEAD < len(chunks):
            pending.append(scores(i + DIFF_LOOKAHEAD))
        vt = vr[:, c0:c0 + step]
        mc = s.max(axis=0, keepdims=True)
        if m is None:
            m = mc
            acc = _dot(vt, jnp.exp2(s - m).astype(BF16))
        else:
            m_new = jnp.maximum(m, mc)
            acc = jnp.exp2(m - m_new) * acc + _dot(vt, jnp.exp2(s - m_new).astype(BF16))
            m = m_new
    o2 = acc[:HEAD_DIM] / acc[HEAD_DIM:HEAD_DIM + 1]
    od = o2[:, :tq] - lam_ref[0, 0] * o2[:, tq:]
    y = od * lax.rsqrt(jnp.mean(od * od, axis=0, keepdims=True) + EPS) * gain * lam_ref[0, 1]
    return y.astype(BF16)


def _diff_attention(dq, dk, dvt, ctx, seq, lam2, gain_col, heads, tq=DIFF_TQ, ck=DIFF_CK):
    rows = dq.shape[0]
    b = rows // seq
    tq = min(tq, seq)
    nq = seq // tq
    in_specs = [pl.BlockSpec((tq, GROUP), lambda i, n, h: (i * nq + n, 0)),
                pl.BlockSpec((seq, GROUP), lambda i, n, h: (i, 0)),
                pl.BlockSpec((heads, V_ROWS, seq), lambda i, n, h: (h, 0, i))]
    args = [dq, dk, dvt]
    if ctx is not None:
        p = ctx[0].shape[1]
        in_specs += [pl.BlockSpec((None, p, GROUP), lambda i, n, h: (i, 0, 0)),
                     pl.BlockSpec((None, heads, V_ROWS, p), lambda i, n, h: (i, h, 0, 0))]
        args += list(ctx)
    in_specs += [pl.BlockSpec(memory_space=pltpu.SMEM),
                 pl.BlockSpec((heads * HEAD_DIM, 1), lambda i, n, h: (h, 0))]
    args += [lam2, gain_col]
    return pl.pallas_call(
        functools.partial(_diff_body, ck, ctx is not None, heads),
        grid=(b, nq, N_HEADS // heads),
        in_specs=in_specs,
        out_specs=pl.BlockSpec((heads * HEAD_DIM, tq), lambda i, n, h: (h, i * nq + n)),
        out_shape=_sds((GROUP, rows), BF16),
        compiler_params=_cparams("parallel", "parallel", "parallel"),
        name="diff_ctx" if ctx is not None else "diff",
    )(*args)


def _ret_operands(rope, rq, rk, cos, sin):
    q = rq
    k = rk * HEAD_DIM ** -0.5
    if rope:
        q = _rope(q, cos, sin, HEAD_DIM // 4)
        k = _rope(k, cos, sin, HEAD_DIM // 4)
    return q, k


def _hgrn_gate(z, lb):
    f = lb + (1.0 - lb) * jax.nn.sigmoid(z)
    return jnp.log(jnp.maximum(f, 1e-30)), (1.0 - lb) * jax.nn.sigmoid(-z)


def _row_index(rows):
    return lax.broadcasted_iota(jnp.int32, (rows, 1), 0)


def _segment_state_step(st_ref, k, v, cum, total, bd):
    ku = k * jnp.exp(total - cum)
    st_ref[...] = st_ref[...] * jnp.exp(total) + _dot_tn(v.astype(BF16), ku.astype(BF16)) * bd


def _states_body(rkf_ref, rvf_ref, rkb_ref, rvb_ref, cf_ref, sf_ref, cb_ref, sb_ref, lg_ref,
                 zf_ref, hvf_ref, zb_ref, hvb_ref, lb_ref, tril_ref, triu_ref,
                 r0f_ref, r0b_ref, h0f_ref, h0b_ref, bd_ref,
                 ref_ref, reb_ref, hef_ref, heb_ref, rstf, rstb, hstf, hstb):
    s = pl.program_id(1)

    @pl.when(s == 0)
    def _():
        rstf[...] = r0f_ref[...]
        rstb[...] = r0b_ref[...]
        hstf[...] = h0f_ref[...]
        hstb[...] = h0b_ref[...]

    ref_ref[...] = rstf[...].astype(BF16)
    reb_ref[...] = rstb[...].astype(BF16)
    hef_ref[...] = hstf[...].astype(BF16)
    heb_ref[...] = hstb[...].astype(BF16)
    bd = bd_ref[...]
    rows = rvf_ref.shape[0]
    lff, kf = _hgrn_gate(zf_ref[...], lb_ref[0:1, :])
    lfb, kb = _hgrn_gate(zb_ref[...], lb_ref[1:2, :])
    cumf = _sum_rows(tril_ref[...], lff)
    cumb = _sum_rows(triu_ref[...], lfb)
    _segment_state_step(hstf, kf, hvf_ref[...], cumf, cumf[rows - 1:rows, :], bd)
    _segment_state_step(hstb, kb, hvb_ref[...], cumb, cumb[0:1, :], bd)
    _, kf = _ret_operands(True, rkf_ref[...], rkf_ref[...], cf_ref[...], sf_ref[...])
    _, kb = _ret_operands(True, rkb_ref[...], rkb_ref[...], cb_ref[...], sb_ref[...])
    i = _row_index(rows).astype(F32)
    lgf, lgb = lg_ref[0:1, :], lg_ref[1:2, :]
    _segment_state_step(rstf, kf, rvf_ref[...], (i + 1.0) * lgf, rows * lgf, bd)
    _segment_state_step(rstb, kb, rvb_ref[...], (rows - i) * lgb, rows * lgb, bd)


def _segment_states(proj, n_b, seq, ret_s0, hgrn_s0, lg_rows, lb_rows, consts, tables):
    ns = seq // SEG
    blk = (SEG, GROUP)

    def fwd(cb):
        return pl.BlockSpec(blk, lambda i, s, cb=cb: (i * ns + s, cb))

    def bwd(cb):
        return pl.BlockSpec(blk, lambda i, s, cb=cb: (i * ns + ns - 1 - s, cb))

    tf = pl.BlockSpec((SEG, LANES), lambda i, s: (s, 0))
    tb = pl.BlockSpec((SEG, LANES), lambda i, s: (ns - 1 - s, 0))
    st_spec = pl.BlockSpec((None, GROUP, GROUP), lambda i, s: (i, 0, 0))
    ef_spec = pl.BlockSpec((None, None, GROUP, GROUP), lambda i, s: (i, s, 0, 0))
    eb_spec = pl.BlockSpec((None, None, GROUP, GROUP), lambda i, s: (i, ns - 1 - s, 0, 0))
    e_shape = _sds((n_b, ns, GROUP, GROUP), BF16)
    return pl.pallas_call(
        _states_body,
        grid=(n_b, ns),
        in_specs=[fwd(CB_RK), fwd(CB_RV), bwd(CB_RK), bwd(CB_RV), tf, tf, tb, tb, _const_spec((2, GROUP)),
                  fwd(CB_HZF), fwd(CB_HI), bwd(CB_HZB), bwd(CB_HI), _const_spec((2, GROUP)),
                  _const_spec((SEG, SEG)), _const_spec((SEG, SEG)),
                  st_spec, st_spec, st_spec, st_spec, _const_spec((GROUP, GROUP))],
        out_specs=[ef_spec, eb_spec, ef_spec, eb_spec],
        out_shape=[e_shape] * 4,
        scratch_shapes=[pltpu.VMEM((GROUP, GROUP), F32)] * 4,
        compiler_params=_cparams("parallel", "arbitrary"),
        name="seg_states",
    )(proj, proj, proj, proj, tables[0], tables[1], tables[0], tables[1], lg_rows,
      proj, proj, proj, proj, lb_rows, consts["tril_seg"], consts["triu_seg"],
      *ret_s0, *hgrn_s0, consts["bd"])


def _ret_out_body(rope, has_state, emit_state, *refs):
    rq_ref, rk_ref, rv_ref, rg_ref = refs[:4]
    refs = refs[4:]
    cos = sin = None
    if rope:
        cos, sin = refs[0][...], refs[1][...]
        refs = refs[2:]
    lgs_ref, lg_ref, gain_ref, g64_ref = refs[:4]
    refs = refs[4:]
    if has_state:
        ef_ref, eb_ref = refs[:2]
        refs = refs[2:]
    o_ref = refs[0]
    mask_ref = refs[-1]
    rows = rq_ref.shape[0]

    @pl.when((pl.program_id(0) == 0) & (pl.program_id(1) == 0))
    def _():
        d = (lax.broadcasted_iota(jnp.int32, (rows, rows), 0)
             - lax.broadcasted_iota(jnp.int32, (rows, rows), 1)).astype(F32)
        for h in range(N_HEADS):
            mask_ref[h * rows:(h + 1) * rows, :] = (
                jnp.where(d >= 0, jnp.exp(jnp.maximum(d, 0.0) * lgs_ref[0, h]), 0.0)
                + jnp.where(d <= 0, jnp.exp(jnp.maximum(-d, 0.0) * lgs_ref[1, h]), 0.0))

    q, k = _ret_operands(rope, rq_ref[...], rk_ref[...], cos, sin)
    v = rv_ref[...]
    kb16, vb16 = k.astype(BF16), v.astype(BF16)
    s4 = _dot_nt(_stack_heads(q.astype(BF16)), kb16)
    o = _unstack_heads(_dot((s4 * mask_ref[...]).astype(BF16), vb16), rows)
    i = _row_index(rows).astype(F32)
    lgf, lgb = lg_ref[0:1, :], lg_ref[1:2, :]
    if has_state:
        o = o + _dot_nt((q * jnp.exp((i + 1.0) * lgf)).astype(BF16), ef_ref[...])
        o = o + _dot_nt((q * jnp.exp((rows - i) * lgb)).astype(BF16), eb_ref[...])
    y = _group_rms(o, g64_ref[...], HEAD_DIM, gain_ref[...]) * _silu(rg_ref[...])
    o_ref[...] = y.astype(BF16)
    if emit_state:
        sf_ref, sb_ref = refs[1:3]
        sf_ref[...] = _head_blocks(_dot_tn((k * jnp.exp((rows - 1.0 - i) * lgf)).astype(BF16), vb16))
        sb_ref[...] = _head_blocks(_dot_tn((k * jnp.exp(i * lgb)).astype(BF16), vb16))


def _ret_out(proj, n_b, seq, lg_smem, lg_rows, gain, consts, tables, states, emit_state):
    ns = seq // SEG
    rope = tables is not None
    blk = (SEG, GROUP)

    def col(cb):
        return pl.BlockSpec(blk, lambda i, s, cb=cb: (i * ns + s, cb))

    in_specs = [col(CB_RQ), col(CB_RK), col(CB_RV), col(CB_RG)]
    args = [proj] * 4
    if rope:
        in_specs += [pl.BlockSpec((SEG, LANES), lambda i, s: (s, 0))] * 2
        args += list(tables)
    in_specs += [pl.BlockSpec(memory_space=pltpu.SMEM), _const_spec((2, GROUP)), _const_spec((1, GROUP)),
                 _const_spec((GROUP, GROUP))]
    args += [lg_smem, lg_rows, gain, consts["g64"]]
    if states is not None:
        in_specs += [pl.BlockSpec((None, None, GROUP, GROUP), lambda i, s: (i, s, 0, 0))] * 2
        args += list(states)
    out_specs = [pl.BlockSpec(blk, lambda i, s: (i * ns + s, 0))]
    out_shape = [_sds((n_b * seq, GROUP), BF16)]
    if emit_state:
        assert ns == 1 and states is None
        out_specs += [pl.BlockSpec((None, HEAD_DIM, GROUP), lambda i, s: (i, 0, 0))] * 2
        out_shape += [_sds((n_b, HEAD_DIM, GROUP), F32)] * 2
    return pl.pallas_call(
        functools.partial(_ret_out_body, rope, states is not None, emit_state),
        grid=(n_b, ns),
        in_specs=in_specs, out_specs=out_specs, out_shape=out_shape,
        scratch_shapes=[pltpu.VMEM((N_HEADS * SEG, SEG), F32)],
        compiler_params=_cparams("arbitrary", "arbitrary"),
        name="ret_out_ctx" if states is not None else "ret_out",
    )(*args)


PAIR_LEVELS_MATMUL = (2, 4, 8)
PAIR_LEVELS_CUMSUM = (16, 32, 64, 128, 256)
PAIR_FOLD = 64


def _pair_level_matrix(n, forward):
    mats = []
    j = np.arange(n)[None, :]
    t = np.arange(n)[:, None]
    for g in PAIR_LEVELS_MATMUL:
        half = g // 2
        pos = t % g
        if forward:
            mid = t - pos + half - 1
            m = np.where(pos >= half, (j > mid) & (j <= t), (j > t) & (j <= mid))
        else:
            mid = t - pos + half
            m = np.where(pos < half, (j >= t) & (j < mid), (j >= mid) & (j < t))
        mats.append(m)
    return jnp.asarray(np.concatenate(mats, axis=0), BF16)


def _pair_decays(lf, cs, small_mat, forward):
    rows = lf.shape[0]
    hi = lf.astype(BF16)
    lo = (lf - hi.astype(F32)).astype(BF16)
    d = _dot(small_mat, hi) + _dot(small_mat, lo)
    out = [jnp.exp(d[i * rows:(i + 1) * rows]) for i in range(len(PAIR_LEVELS_MATMUL))]
    for g in PAIR_LEVELS_CUMSUM:
        half = g // 2
        pieces = []
        for grp in range(rows // g):
            mid = grp * g + (half - 1 if forward else half)
            pieces.append(jnp.broadcast_to(cs[mid:mid + 1, :], (g, cs.shape[1])))
        diff = cs - (pieces[0] if len(pieces) == 1 else jnp.concatenate(pieces, axis=0))
        out.append(jnp.exp(jnp.minimum(diff, -diff)))
    return out


def _hgrn_pair_weights(q, kf, kb, decays_f, decays_b):
    rows = q.shape[0]
    nfold = rows // PAIR_FOLD
    levels = PAIR_LEVELS_MATMUL + PAIR_LEVELS_CUMSUM
    pos = _row_index(rows)
    t_loc = lax.broadcasted_iota(jnp.int32, (PAIR_FOLD, GROUP), 0)
    s_loc = lax.broadcasted_iota(jnp.int32, (PAIR_FOLD, GROUP), 1) % PAIR_FOLD
    r_big = lax.broadcasted_iota(jnp.int32, (N_HEADS * rows, rows), 0) % rows
    c_big = lax.broadcasted_iota(jnp.int32, (N_HEADS * rows, rows), 1)
    folded = [None] * nfold
    stacked = None
    for g, ef, eb in reversed(list(zip(levels, decays_f, decays_b))):
        right = (pos % g) >= (g // 2)
        qs = (q * jnp.where(right, ef, eb)).astype(BF16)
        ks = (jnp.where(right, kb, kf) * jnp.where(right, eb, ef)).astype(BF16)
        if g > PAIR_FOLD:
            s = _dot_nt(_stack_heads(qs), ks)
            stacked = s if stacked is None else jnp.where((r_big // g) == (c_big // g), s, stacked)
        else:
            same = None if g == PAIR_FOLD else (t_loc // g) == (s_loc // g)
            for j in range(nfold):
                r0, r1 = j * PAIR_FOLD, (j + 1) * PAIR_FOLD
                s = _dot_nt(qs[r0:r1], _stack_heads(ks[r0:r1]))
                folded[j] = s if same is None else jnp.where(same, s, folded[j])
    stacked = jnp.where((r_big // PAIR_FOLD) == (c_big // PAIR_FOLD), 0.0, stacked)
    folded = [jnp.where(t_loc == s_loc, 0.0, f) for f in folded]
    return folded, stacked


def _hgrn_out_body(has_state, emit_state, *refs):
    (q_ref, zf_ref, zb_ref, v_ref, g_ref, lb_ref, gain_ref, g64_ref,
     tril_ref, triu_ref, smf_ref, smb_ref) = refs[:12]
    refs = refs[12:]
    g64 = g64_ref[...]
    if has_state:
        ef, eb = refs[0][...], refs[1][...]
        refs = refs[2:]
    o_ref = refs[0]
    q, v = q_ref[...], v_ref[...]
    rows = q.shape[0]
    lff, kf = _hgrn_gate(zf_ref[...], lb_ref[0:1, :])
    lfb, kb = _hgrn_gate(zb_ref[...], lb_ref[1:2, :])
    csf = _sum_rows(tril_ref[...], lff)
    csb = _sum_rows(triu_ref[...], lfb)
    folded, stacked = _hgrn_pair_weights(q, kf, kb, _pair_decays(lff, csf, smf_ref[...], True),
                                         _pair_decays(lfb, csb, smb_ref[...], False))
    vb = v.astype(BF16)
    o = _dot((q * (kf + kb)).astype(BF16), g64) * v
    o = o + _unstack_heads(_dot(stacked.astype(BF16), vb), rows)
    tiles = []
    for j in range(rows // PAIR_FOLD):
        r0, r1 = j * PAIR_FOLD, (j + 1) * PAIR_FOLD
        tiles.append(_dot(folded[j].astype(BF16), _stack_heads(vb[r0:r1])))
    o = o + jnp.concatenate(tiles, axis=0)
    if has_state:
        o = o + _dot_nt((q * jnp.exp(csf)).astype(BF16), ef)
        o = o + _dot_nt((q * jnp.exp(csb)).astype(BF16), eb)
    y = _group_rms(o, g64, HEAD_DIM, gain_ref[...]) * _silu(g_ref[...])
    o_ref[...] = y.astype(BF16)
    if emit_state:
        totf, totb = csf[rows - 1:rows, :], csb[0:1, :]
        refs[1][...] = _head_blocks(_dot_tn((kf * jnp.exp(totf - csf)).astype(BF16), vb))
        refs[2][...] = _head_blocks(_dot_tn((kb * jnp.exp(totb - csb)).astype(BF16), vb))


def _hgrn_out(proj, n_b, seq, lb_rows, gain, consts, states, emit_state):
    ns = seq // SEG
    blk = (SEG, GROUP)

    def col(cb):
        return pl.BlockSpec(blk, lambda i, s, cb=cb: (i * ns + s, cb))

    n_small = len(PAIR_LEVELS_MATMUL) * SEG
    in_specs = [col(CB_HQ), col(CB_HZF), col(CB_HZB), col(CB_HI), col(CB_HG),
                _const_spec((2, GROUP)), _const_spec((1, GROUP)), _const_spec((GROUP, GROUP)),
                _const_spec((SEG, SEG)), _const_spec((SEG, SEG)),
                _const_spec((n_small, SEG)), _const_spec((n_small, SEG))]
    args = [proj] * 5 + [lb_rows, gain, consts["g64"],
                         consts["tril_seg"], consts["triu_seg"], consts["pair_f"], consts["pair_b"]]
    if states is not None:
        in_specs += [pl.BlockSpec((None, None, GROUP, GROUP), lambda i, s: (i, s, 0, 0))] * 2
        args += list(states)
    out_specs = [pl.BlockSpec(blk, lambda i, s: (i * ns + s, 0))]
    out_shape = [_sds((n_b * seq, GROUP), BF16)]
    if emit_state:
        assert ns == 1 and states is None
        out_specs += [pl.BlockSpec((None, HEAD_DIM, GROUP), lambda i, s: (i, 0, 0))] * 2
        out_shape += [_sds((n_b, HEAD_DIM, GROUP), F32)] * 2
    return pl.pallas_call(
        functools.partial(_hgrn_out_body, states is not None, emit_state),
        grid=(n_b, ns),
        in_specs=in_specs, out_specs=out_specs, out_shape=out_shape,
        compiler_params=_cparams("parallel", "parallel"),
        name="hgrn_out_ctx" if states is not None else "hgrn_out",
    )(*args)


def _ffn_body(x_ref, oa_ref, ob_ref, oc_ref, od_ref, wo_ref, g1_ref, n2_ref, sc_ref, sh_ref, g2_ref,
              wg_ref, wu_ref, wd_ref, y_ref, x1_ref, h_ref, acc_ref):
    j = pl.program_id(1)

    @pl.when(j == 0)
    def _():
        mix = _dot(oa_ref[...], wo_ref[0:GROUP, :])
        mix += _dot_tn(ob_ref[...], wo_ref[GROUP:2 * GROUP, :])
        mix += _dot_tn(oc_ref[...], wo_ref[2 * GROUP:3 * GROUP, :])
        mix += _dot(od_ref[...], wo_ref[3 * GROUP:4 * GROUP, :])
        x1 = x_ref[...] + g1_ref[...] * mix
        x1_ref[...] = x1
        h = x1 * lax.rsqrt(jnp.mean(x1 * x1, axis=-1, keepdims=True) + EPS) * n2_ref[...]
        h_ref[...] = (h * (1.0 + sc_ref[...]) + sh_ref[...]).astype(BF16)
        acc_ref[...] = jnp.zeros_like(acc_ref)

    h = h_ref[...]
    a = _silu(_dot(h, wg_ref[...])) * _dot(h, wu_ref[...])
    acc_ref[...] += _dot(a.astype(BF16), wd_ref[...])

    @pl.when(j == pl.num_programs(1) - 1)
    def _():
        y_ref[...] = x1_ref[...] + g2_ref[...] * acc_ref[...]


def _outproj_ffn(x, mixes, w_out, g1, n2, sc2, sh2, g2, w_in, w_dn, rows_per_mod, mod_row0, tm=FFN_TM, th=FFN_TH):
    t, d = x.shape
    hid = w_dn.shape[0]
    nh = hid // th
    mod_map = _mod_row_map(tm, rows_per_mod, mod_row0)
    mod2 = lambda i, j: mod_map(i)
    row = lambda i, j: (i, 0)
    mix_spec = pl.BlockSpec((tm, GROUP), row)
    mix_t_spec = pl.BlockSpec((GROUP, tm), lambda i, j: (0, i))
    mod_spec = pl.BlockSpec((None, 1, d), mod2)
    return pl.pallas_call(
        _ffn_body,
        grid=(t // tm, nh),
        in_specs=[pl.BlockSpec((tm, d), row), mix_spec, mix_t_spec, mix_t_spec, mix_spec]
                 + [_const_spec((d, d)), mod_spec, _const_spec((1, d)), mod_spec, mod_spec, mod_spec,
                    pl.BlockSpec((d, th), lambda i, j: (0, j)),
                    pl.BlockSpec((d, th), lambda i, j: (0, nh + j)),
                    pl.BlockSpec((th, d), lambda i, j: (j, 0))],
        out_specs=pl.BlockSpec((tm, d), row),
        out_shape=_sds((t, d), F32),
        scratch_shapes=[pltpu.VMEM((tm, d), F32), pltpu.VMEM((tm, d), BF16), pltpu.VMEM((tm, d), F32)],
        compiler_params=_cparams("parallel", "arbitrary"),
        name="outproj_ffn",
    )(x, *mixes, w_out, g1, n2, sc2, sh2, g2, w_in, w_in, w_dn)


def _lane_rows(per_head):
    return jnp.repeat(per_head.astype(F32), HEAD_DIM, axis=1)


def _states_to_lanes(s):
    eye = jnp.eye(N_HEADS, dtype=F32)
    st = jnp.swapaxes(s.astype(F32), -1, -2)[..., :, :, None, :] * eye[:, None, :, None]
    return st.reshape(s.shape[:-3] + (GROUP, GROUP))


def _with_ones_rows(vt):
    ones = jnp.ones(vt.shape[:-2] + (V_ROWS - HEAD_DIM, vt.shape[-1]), vt.dtype)
    return jnp.concatenate([vt, ones], axis=-2)


def _lanes_to_states(sf, sb):
    s = jnp.stack([sf, sb], axis=1)
    return s.reshape(s.shape[:3] + (N_HEADS, HEAD_DIM)).transpose(0, 1, 3, 2, 4)


def kernel(x_prompt, x_sample, state_ret, cache_win_k, cache_win_v, cache_diff_k, cache_diff_v, state_hgrn, c, c_ctx, norm1_g, norm2_g, w_ada, b_ada, w_in, ret_decay, ret_norm_g, win_q_norm, win_k_norm, win_sink, diff_q_norm, diff_k_norm, diff_lambda, diff_norm_g, hgrn_lb_logits, hgrn_norm_g, w_out, w_ffn_in, w_ffn_out):
    n_p, seq_p, d = x_prompt.shape
    n_s, seq_s, _ = x_sample.shape
    depth = w_in.shape[0]
    t_s, t_p = n_s * seq_s, n_p * seq_p
    assert seq_p == SEG and seq_s % SEG == 0 and d == N_HEADS * GROUP

    tril_seg, triu_seg = _cumsum_matrices(SEG)
    consts = dict(g64=_group_matrix(GROUP, HEAD_DIM), g32=_group_matrix(GROUP, DIFF_DIM),
                  bd=_group_matrix(GROUP, HEAD_DIM).astype(F32), dupk=_kv_dup_matrix(),
                  tril_seg=tril_seg, triu_seg=triu_seg,
                  pair_f=_pair_level_matrix(SEG, True), pair_b=_pair_level_matrix(SEG, False))
    tab64 = _rope_tables(seq_s, HEAD_DIM)
    tab32 = _rope_tables(seq_s, DIFF_DIM)

    n_rows = -(-(n_s + 1) // 8) * 8
    cond = jnp.zeros((n_rows, d), F32).at[:n_s].set(c).at[n_s].set(c_ctx)
    mod = _adaln(cond, w_ada, b_ada).reshape(depth, n_rows, 6, 1, d)

    lb_p = jax.nn.softmax(hgrn_lb_logits.astype(F32), axis=0)
    lb_all = jnp.cumsum(lb_p, axis=0) - lb_p
    log_gamma = -jnp.exp(ret_decay.astype(F32))

    ctx_states = _states_to_lanes(jnp.stack([state_ret, state_hgrn], axis=2))

    xs = x_sample.reshape(t_s, d)
    xp = x_prompt.reshape(t_p, d)
    new_ret, new_wk, new_wv, new_dk, new_dv, new_hg = [], [], [], [], [], []
    for l in range(depth):
        lam_init = 0.8 - 0.6 * math.exp(-0.3 * l)
        sh1, sc1, g1, sh2, sc2, g2 = [mod[l, :, i] for i in range(6)]
        w_in_l, w_out_l = w_in[l].astype(BF16), w_out[l].astype(BF16)
        w_up_l, w_dn_l = w_ffn_in[l].astype(BF16), w_ffn_out[l].astype(BF16)
        n1, n2 = norm1_g[l][None], norm2_g[l][None]
        gains = (jnp.tile(win_q_norm[l], N_HEADS)[None], jnp.tile(win_k_norm[l], KV_HEADS)[None],
                 jnp.tile(diff_q_norm[l], 2 * N_HEADS)[None], jnp.tile(diff_k_norm[l], 2 * N_HEADS)[None])
        sink = win_sink[l].astype(F32)
        lq1, lk1, lq2, lk2 = diff_lambda[l].astype(F32)
        lam = jnp.exp(jnp.sum(lq1 * lk1)) - jnp.exp(jnp.sum(lq2 * lk2)) + lam_init
        lam2 = jnp.stack([lam, jnp.asarray(1.0 - lam_init, F32)]).reshape(1, 2)
        dgain = diff_norm_g[l].astype(F32)[:, None]
        lg = log_gamma[l]
        lg_rows = _lane_rows(lg)
        rgain = ret_norm_g[l][None]
        lb_rows = lb_all[l]
        hgain = hgrn_norm_g[l][None]

        proj = _inproj(xs, n1, sc1, sh1, w_in_l, seq_s, 0)
        wq, wk, wvt, dq, dk, dvt = _prep(proj, gains, consts, tab64 + tab32, False)
        ctx_wk = jnp.repeat(cache_win_k[:, l].transpose(0, 2, 1, 3), N_HEADS // KV_HEADS, axis=2)
        ctx_wk = ctx_wk.reshape(n_s, -1, GROUP).astype(BF16)
        ctx_wvt = _with_ones_rows(cache_win_v[:, l].swapaxes(-1, -2).astype(BF16))
        o_win = _win_sample(wq, wk, wvt, ctx_wk, ctx_wvt, jnp.repeat(sink * LOG2_E, WIN_BLOCK)[None])
        ctx_dk = cache_diff_k[:, l].transpose(0, 3, 1, 2, 4).reshape(n_s, -1, GROUP).astype(BF16)
        ctx_dvt = _with_ones_rows(cache_diff_v[:, l].swapaxes(-1, -2).astype(BF16))
        o_diff = _diff_attention(dq, dk, dvt, (ctx_dk, ctx_dvt), seq_s, lam2, dgain, DIFF_HEADS_LATENT)
        entry = _segment_states(proj, n_s, seq_s, (ctx_states[:, l, 0, 0], ctx_states[:, l, 0, 1]),
                                (ctx_states[:, l, 1, 0], ctx_states[:, l, 1, 1]), lg_rows, lb_rows, consts, tab64)
        (o_ret,) = _ret_out(proj, n_s, seq_s, lg, lg_rows, rgain, consts, tab64, entry[:2], False)
        (o_h,) = _hgrn_out(proj, n_s, seq_s, lb_rows, hgain, consts, entry[2:], False)
        xs = _outproj_ffn(xs, (o_ret, o_win, o_diff, o_h), w_out_l, g1, n2, sc2, sh2, g2, w_up_l, w_dn_l, seq_s, 0)

        proj = _inproj(xp, n1, sc1, sh1, w_in_l, t_p, n_s)
        wq, wk, wvt, dq, dk, dvt, wk_c, wv_c, dk_c, dv_c = _prep(proj, gains, consts, None, True, tm=seq_p)
        o_win = _win_prompt(wq, wk, wvt, seq_p, jnp.repeat(sink * LOG2_E, seq_p)[None])
        o_diff = _diff_attention(dq, dk, dvt, None, seq_p, lam2, dgain, N_HEADS)
        o_ret, rsf, rsb = _ret_out(proj, n_p, seq_p, lg, lg_rows, rgain, consts, None, None, True)
        o_h, hsf, hsb = _hgrn_out(proj, n_p, seq_p, lb_rows, hgain, consts, None, True)
        xp = _outproj_ffn(xp, (o_ret, o_win, o_diff, o_h), w_out_l, g1, n2, sc2, sh2, g2, w_up_l, w_dn_l, t_p, n_s)

        new_ret.append(_lanes_to_states(rsf, rsb))
        new_hg.append(_lanes_to_states(hsf, hsb))
        new_wk.append(wk_c)
        new_wv.append(wv_c)
        new_dk.append(dk_c)
        new_dv.append(dv_c)

    return (xp.reshape(n_p, seq_p, d), xs.reshape(n_s, seq_s, d),
            jnp.stack(new_ret, axis=1), jnp.stack(new_wk, axis=1), jnp.stack(new_wv, axis=1),
            jnp.stack(new_dk, axis=1), jnp.stack(new_dv, axis=1), jnp.stack(new_hg, axis=1))
```

```python
import functools
import math

import numpy as np
import jax
import jax.numpy as jnp
from jax import lax
from jax.experimental import pallas as pl
from jax.experimental.pallas import tpu as pltpu

F32 = jnp.float32
BF16 = jnp.bfloat16

GROUP = 256
HEAD_DIM = 64
N_HEADS = 4
KV_HEADS = 2
DIFF_DIM = 32
WINDOW = 128
WIN_BLOCK = 128
GRID_W = 64
ROPE_BASE = 10000.0
EPS = 1e-6
MASK_VALUE = -1e30
SEG = 256
LANES = 128
BF16_SUBLANES = 16
LOG2_E = math.log2(math.e)
V_ROWS = HEAD_DIM + BF16_SUBLANES
DIFF_LOOKAHEAD = 2

INPROJ_TM = 512
PREP_TM = 512
FFN_TM = 1024
FFN_TH = 256
DIFF_TQ = 512
DIFF_CK = 512
DIFF_HEADS_LATENT = 4
V7X_VMEM_LIMIT_BYTES = 56 * 1024 * 1024

(CB_RQ, CB_RK, CB_RV, CB_RG, CB_WQ, CB_WKV, CB_DQ, CB_DK, CB_DV,
 CB_HQ, CB_HZF, CB_HZB, CB_HI, CB_HG) = range(14)

_NT = (((1,), (1,)), ((), ()))
_TN = (((0,), (0,)), ((), ()))


def _sds(shape, dtype):
    return jax.ShapeDtypeStruct(shape, dtype)


def _cparams(*sem):
    return pltpu.CompilerParams(dimension_semantics=sem, vmem_limit_bytes=V7X_VMEM_LIMIT_BYTES)


def _const_spec(shape):
    return pl.BlockSpec(shape, lambda *_: (0,) * len(shape))


def _dot(a, b):
    return jnp.dot(a, b, preferred_element_type=F32)


def _dot_nt(a, b):
    return lax.dot_general(a, b, _NT, preferred_element_type=F32)


def _dot_tn(a, b):
    return lax.dot_general(a, b, _TN, preferred_element_type=F32)


def _silu(x):
    return x * jax.nn.sigmoid(x)


def _group_matrix(width, gsize):
    i = np.arange(width)
    return jnp.asarray((i[:, None] // gsize) == (i[None, :] // gsize), BF16)


def _cumsum_matrices(n):
    r, c = np.arange(n)[:, None], np.arange(n)[None, :]
    return jnp.asarray(c <= r, BF16), jnp.asarray(c >= r, BF16)


def _kv_dup_matrix():
    d = np.zeros((KV_HEADS * HEAD_DIM, GROUP), np.float32)
    for h in range(N_HEADS):
        kv = h // (N_HEADS // KV_HEADS)
        for j in range(HEAD_DIM):
            d[kv * HEAD_DIM + j, h * HEAD_DIM + j] = 1.0
    return jnp.asarray(d, BF16)


def _rope_tables(n_tokens, head_dim):
    d = head_dim // 2
    half = d // 2
    inv = ROPE_BASE ** (-jnp.arange(half, dtype=F32) / half)
    t = jnp.arange(n_tokens)
    row = (t // GRID_W).astype(F32)
    col = (t % GRID_W).astype(F32)
    j = np.arange(LANES) % head_dim
    w = j % d
    use_row = jnp.asarray((j // d) == 0)
    pos = jnp.where(use_row[None, :], row[:, None], col[:, None])
    ang = pos * inv[w % half][None, :]
    sign = jnp.asarray(np.where(w >= half, 1.0, -1.0), F32)
    return jnp.cos(ang), jnp.sin(ang) * sign[None, :]


def _group_rms(x, g_mat, gsize, gain):
    ss = _dot((x * x).astype(BF16), g_mat)
    return x * lax.rsqrt(ss * (1.0 / gsize) + EPS) * gain


def _rope(y, cos, sin, half):
    lane = lax.broadcasted_iota(jnp.int32, (1, LANES), 1)
    second = (lane % (2 * half)) >= half
    outs = []
    for p in range(y.shape[1] // LANES):
        z = y[:, p * LANES:(p + 1) * LANES]
        partner = jnp.where(second, pltpu.roll(z, half, 1), pltpu.roll(z, LANES - half, 1))
        outs.append(z * cos + partner * sin)
    return outs[0] if len(outs) == 1 else jnp.concatenate(outs, axis=1)


def _head_of_lane(width=GROUP):
    return lax.broadcasted_iota(jnp.int32, (1, width), 1) // HEAD_DIM


def _stack_heads(q, n_heads=N_HEADS):
    head = _head_of_lane()
    zero = jnp.zeros_like(q)
    return jnp.concatenate([jnp.where(head == h, q, zero) for h in range(n_heads)], axis=0)


def _unstack_heads(o4, rows):
    head = _head_of_lane()
    out = jnp.where(head == 0, o4[:rows], 0.0)
    for h in range(1, N_HEADS):
        out = out + jnp.where(head == h, o4[h * rows:(h + 1) * rows], 0.0)
    return out


def _head_blocks(s):
    head = _head_of_lane()
    out = jnp.where(head == 0, s[:HEAD_DIM], 0.0)
    for h in range(1, N_HEADS):
        out = out + jnp.where(head == h, s[h * HEAD_DIM:(h + 1) * HEAD_DIM], 0.0)
    return out


def _split3(x):
    hi = x.astype(BF16)
    r = x - hi.astype(F32)
    mid = r.astype(BF16)
    lo = (r - mid.astype(F32)).astype(BF16)
    return hi, mid, lo


def _sum_rows(m01, x):
    hi, mid, lo = _split3(x)
    return _dot(m01, hi) + _dot(m01, mid) + _dot(m01, lo)


def _adaln_body(c_ref, w_ref, b_ref, o_ref):
    c = c_ref[...]
    o_ref[...] = jnp.dot(_silu(c), w_ref[...], preferred_element_type=F32,
                         precision=lax.Precision.HIGHEST) + b_ref[...]


def _adaln(cond, w_ada, b_ada):
    depth, d, n = w_ada.shape
    rows = cond.shape[0]
    tn = n // 4
    return pl.pallas_call(
        _adaln_body,
        grid=(depth, n // tn),
        in_specs=[pl.BlockSpec((rows, d), lambda l, j: (0, 0)),
                  pl.BlockSpec((None, d, tn), lambda l, j: (l, 0, j)),
                  pl.BlockSpec((None, 1, tn), lambda l, j: (l, 0, j))],
        out_specs=pl.BlockSpec((None, rows, tn), lambda l, j: (l, 0, j)),
        out_shape=_sds((depth, rows, n), F32),
        compiler_params=_cparams("parallel", "parallel"),
        name="adaln",
    )(cond, w_ada, b_ada.reshape(depth, 1, n))


def _inproj_body(x_ref, g_ref, sc_ref, sh_ref, w_ref, o_ref):
    x = x_ref[...]
    h = x * lax.rsqrt(jnp.mean(x * x, axis=-1, keepdims=True) + EPS) * g_ref[...]
    h = h * (1.0 + sc_ref[...]) + sh_ref[...]
    o_ref[...] = _dot(h.astype(BF16), w_ref[...])


def _mod_row_map(tm, rows_per_mod, mod_row0):
    return lambda i: (mod_row0 + (i * tm) // rows_per_mod, 0, 0)


def _inproj(x, g, sc, sh, w, rows_per_mod, mod_row0, tm=INPROJ_TM):
    t, d = x.shape
    n = w.shape[1]
    mod_map = _mod_row_map(tm, rows_per_mod, mod_row0)
    return pl.pallas_call(
        _inproj_body,
        grid=(t // tm,),
        in_specs=[pl.BlockSpec((tm, d), lambda i: (i, 0)),
                  _const_spec((1, d)),
                  pl.BlockSpec((None, 1, d), mod_map),
                  pl.BlockSpec((None, 1, d), mod_map),
                  _const_spec((d, n))],
        out_specs=pl.BlockSpec((tm, n), lambda i: (i, 0)),
        out_shape=_sds((t, n), F32),
        compiler_params=_cparams("parallel"),
        name="inproj",
    )(x, g, sc, sh, w)


def _prep_body(rope, emit_f32, *refs):
    (wq_ref, wkv_ref, dq_ref, dk_ref, dv_ref, qn_ref, kn_ref, dqn_ref, dkn_ref, g64_ref, g32_ref,
     dup_ref) = refs[:12]
    refs = refs[12:]
    if rope:
        c64_ref, s64_ref, c32_ref, s32_ref = refs[:4]
        refs = refs[4:]
    wq_o, wk_o, wvt_o, dq_o, dk_o, dvt_o = refs[:6]
    g64 = g64_ref[...]
    g32 = g32_ref[...]
    tm = dv_ref.shape[0]

    wq = _group_rms(wq_ref[...], g64, HEAD_DIM, qn_ref[...])
    wkv = wkv_ref[...]
    wk = _group_rms(wkv[:, :LANES], g64[:LANES, :LANES], HEAD_DIM, kn_ref[...])
    dq = _group_rms(dq_ref[...], g32, DIFF_DIM, dqn_ref[...])
    dk = _group_rms(dk_ref[...], g32, DIFF_DIM, dkn_ref[...])
    if rope:
        c64, s64, c32, s32 = c64_ref[...], s64_ref[...], c32_ref[...], s32_ref[...]
        wq = _rope(wq, c64, s64, HEAD_DIM // 4)
        wk = _rope(wk, c64, s64, HEAD_DIM // 4)
        dq = _rope(dq, c32, s32, DIFF_DIM // 4)
        dk = _rope(dk, c32, s32, DIFF_DIM // 4)
    wq_o[...] = (wq * (HEAD_DIM ** -0.5 * LOG2_E)).astype(BF16)
    wk_o[...] = _dot(wk.astype(BF16), dup_ref[...]).astype(BF16)
    wvt_o[:, :HEAD_DIM, :] = wkv[:, LANES:].T.reshape(KV_HEADS, HEAD_DIM, tm).astype(BF16)
    wvt_o[:, HEAD_DIM:, :] = jnp.ones((KV_HEADS, V_ROWS - HEAD_DIM, tm), BF16)
    dq_o[...] = (dq * (DIFF_DIM ** -0.5 * LOG2_E)).astype(BF16)
    dk_o[...] = dk.astype(BF16)
    dvt_o[:, :HEAD_DIM, :] = dv_ref[...].T.reshape(N_HEADS, HEAD_DIM, tm).astype(BF16)
    dvt_o[:, HEAD_DIM:, :] = jnp.ones((N_HEADS, V_ROWS - HEAD_DIM, tm), BF16)
    if emit_f32:
        wk_c, wv_c, dk_c, dv_c = refs[6:10]
        dv = dv_ref[...]
        for kv in range(KV_HEADS):
            wk_c[kv] = wk[:, kv * HEAD_DIM:(kv + 1) * HEAD_DIM]
            wv_c[kv] = wkv[:, LANES + kv * HEAD_DIM:LANES + (kv + 1) * HEAD_DIM]
        for h in range(N_HEADS):
            dv_c[h] = dv[:, h * HEAD_DIM:(h + 1) * HEAD_DIM]
            for c in range(2):
                lane0 = h * HEAD_DIM + c * DIFF_DIM
                dk_c[h, c] = dk[:, lane0:lane0 + DIFF_DIM]


def _prep(proj, gains, consts, tables, emit_f32, tm=PREP_TM):
    rope = tables is not None
    rows = proj.shape[0]

    def col(cb):
        return pl.BlockSpec((tm, GROUP), lambda i, cb=cb: (i, cb))

    def vt(heads):
        return pl.BlockSpec((heads, V_ROWS, tm), lambda i: (0, 0, i))

    in_specs = [col(CB_WQ), col(CB_WKV), col(CB_DQ), col(CB_DK), col(CB_DV),
                _const_spec((1, GROUP)), _const_spec((1, LANES)), _const_spec((1, GROUP)), _const_spec((1, GROUP)),
                _const_spec((GROUP, GROUP)), _const_spec((GROUP, GROUP)), _const_spec((LANES, GROUP))]
    args = [proj] * 5 + list(gains) + [consts["g64"], consts["g32"], consts["dupk"]]
    if rope:
        seq = tables[0].shape[0]
        nper = seq // tm
        in_specs += [pl.BlockSpec((tm, LANES), lambda i: (i % nper, 0))] * 4
        args += list(tables)
    out = pl.BlockSpec((tm, GROUP), lambda i: (i, 0))
    tok = _sds((rows, GROUP), BF16)
    out_specs = [out, out, vt(KV_HEADS), out, out, vt(N_HEADS)]
    out_shape = [tok, tok, _sds((KV_HEADS, V_ROWS, rows), BF16), tok, tok, _sds((N_HEADS, V_ROWS, rows), BF16)]
    if emit_f32:
        nb = rows // tm
        out_specs += [pl.BlockSpec((None, KV_HEADS, tm, HEAD_DIM), lambda i: (i, 0, 0, 0)),
                      pl.BlockSpec((None, KV_HEADS, tm, HEAD_DIM), lambda i: (i, 0, 0, 0)),
                      pl.BlockSpec((None, N_HEADS, 2, tm, DIFF_DIM), lambda i: (i, 0, 0, 0, 0)),
                      pl.BlockSpec((None, N_HEADS, tm, HEAD_DIM), lambda i: (i, 0, 0, 0))]
        out_shape += [_sds((nb, KV_HEADS, tm, HEAD_DIM), F32), _sds((nb, KV_HEADS, tm, HEAD_DIM), F32),
                      _sds((nb, N_HEADS, 2, tm, DIFF_DIM), F32), _sds((nb, N_HEADS, tm, HEAD_DIM), F32)]
    return pl.pallas_call(
        functools.partial(_prep_body, rope, emit_f32),
        grid=(rows // tm,),
        in_specs=in_specs, out_specs=out_specs, out_shape=out_shape,
        compiler_params=_cparams("parallel"),
        name="prep_rope" if rope else "prep",
    )(*args)


def _win_body(banded, *refs):
    if banded:
        (q_ref, kp_ref, kc_ref, kn_ref, vp_ref, vc_ref, vn_ref, ck_ref, cvt_ref, sink_ref, o_ref) = refs
        k_refs, v_refs = (kp_ref, kc_ref, kn_ref), (vp_ref, vc_ref, vn_ref)
    else:
        q_ref, kc_ref, vc_ref, sink_ref, o_ref = refs
        k_refs, v_refs = (kc_ref,), (vc_ref,)
    tq = q_ref.shape[0]
    q4 = _stack_heads(q_ref[...])
    k_loc = k_refs[0][...] if len(k_refs) == 1 else jnp.concatenate([r[...] for r in k_refs], axis=0)
    s_loc = _dot_nt(k_loc, q4)
    sink = sink_ref[...]
    if banded:
        n = pl.program_id(1)
        seq = pl.num_programs(1) * WIN_BLOCK
        t = n * WIN_BLOCK + lax.broadcasted_iota(jnp.int32, s_loc.shape, 1) % WIN_BLOCK
        kpos = (n - 1) * WIN_BLOCK + lax.broadcasted_iota(jnp.int32, s_loc.shape, 0)
        valid = (kpos >= 0) & (kpos < seq) & (jnp.abs(t - kpos) <= WINDOW)
        s_loc = jnp.where(valid, s_loc, MASK_VALUE)
        s_ctx = _dot_nt(ck_ref[...], q4)
        m = jnp.maximum(jnp.maximum(s_loc.max(axis=0, keepdims=True), s_ctx.max(axis=0, keepdims=True)), sink)
        p_ctx = jnp.exp2(s_ctx - m).astype(BF16)
    else:
        m = jnp.maximum(s_loc.max(axis=0, keepdims=True), sink)
    p_loc = jnp.exp2(s_loc - m).astype(BF16)
    p_sink = jnp.exp2(sink - m)
    outs = []
    for h in range(N_HEADS):
        kv = h // (N_HEADS // KV_HEADS)
        c0, c1 = h * tq, (h + 1) * tq
        vt = v_refs[0][kv] if len(v_refs) == 1 else jnp.concatenate([r[kv] for r in v_refs], axis=1)
        acc = _dot(vt, p_loc[:, c0:c1])
        if banded:
            acc = acc + _dot(cvt_ref[kv], p_ctx[:, c0:c1])
        outs.append(acc[:HEAD_DIM] / (acc[HEAD_DIM:HEAD_DIM + 1] + p_sink[:, c0:c1]))
    o_ref[...] = jnp.concatenate(outs, axis=0).astype(BF16)


def _win_sample(wq, wk, wvt, ctx_k, ctx_vt, sink_row):
    b, p, _ = ctx_k.shape
    rows = wq.shape[0]
    nq = rows // b // WIN_BLOCK
    blk = (WIN_BLOCK, GROUP)
    vblk = (KV_HEADS, V_ROWS, WIN_BLOCK)
    prev = lambda i, n: i * nq + jnp.maximum(n - 1, 0)
    cur = lambda i, n: i * nq + n
    nxt = lambda i, n: i * nq + jnp.minimum(n + 1, nq - 1)
    return pl.pallas_call(
        functools.partial(_win_body, True),
        grid=(b, nq),
        in_specs=[pl.BlockSpec(blk, lambda i, n: (cur(i, n), 0)),
                  pl.BlockSpec(blk, lambda i, n: (prev(i, n), 0)),
                  pl.BlockSpec(blk, lambda i, n: (cur(i, n), 0)),
                  pl.BlockSpec(blk, lambda i, n: (nxt(i, n), 0)),
                  pl.BlockSpec(vblk, lambda i, n: (0, 0, prev(i, n))),
                  pl.BlockSpec(vblk, lambda i, n: (0, 0, cur(i, n))),
                  pl.BlockSpec(vblk, lambda i, n: (0, 0, nxt(i, n))),
                  pl.BlockSpec((None, p, GROUP), lambda i, n: (i, 0, 0)),
                  pl.BlockSpec((None, KV_HEADS, V_ROWS, p), lambda i, n: (i, 0, 0, 0)),
                  _const_spec((1, N_HEADS * WIN_BLOCK))],
        out_specs=pl.BlockSpec((GROUP, WIN_BLOCK), lambda i, n: (0, cur(i, n))),
        out_shape=_sds((GROUP, rows), BF16),
        compiler_params=_cparams("parallel", "parallel"),
        name="win_sample",
    )(wq, wk, wk, wk, wvt, wvt, wvt, ctx_k, ctx_vt, sink_row)


def _win_prompt(wq, wk, wvt, seq, sink_row):
    rows = wq.shape[0]
    blk = (seq, GROUP)
    return pl.pallas_call(
        functools.partial(_win_body, False),
        grid=(rows // seq,),
        in_specs=[pl.BlockSpec(blk, lambda i: (i, 0)), pl.BlockSpec(blk, lambda i: (i, 0)),
                  pl.BlockSpec((KV_HEADS, V_ROWS, seq), lambda i: (0, 0, i)),
                  _const_spec((1, N_HEADS * seq))],
        out_specs=pl.BlockSpec((GROUP, seq), lambda i: (0, i)),
        out_shape=_sds((GROUP, rows), BF16),
        compiler_params=_cparams("parallel"),
        name="win_prompt",
    )(wq, wk, wvt, sink_row)


def _diff_body(ck, has_ctx, heads, *refs):
    if has_ctx:
        q_ref, k_ref, vt_ref, ck_ref, cvt_ref, lam_ref, gain_ref, o_ref = refs
    else:
        q_ref, k_ref, vt_ref, lam_ref, gain_ref, o_ref = refs
    for j in range(heads):
        sources = [(k_ref, vt_ref.at[j])] + ([(ck_ref, cvt_ref.at[j])] if has_ctx else [])
        rows = slice(j * HEAD_DIM, (j + 1) * HEAD_DIM)
        o_ref[rows, :] = _diff_head(ck, pl.program_id(2) * heads + j, q_ref[...], sources,
                                    lam_ref, gain_ref[rows, :])


def _diff_head(ck, h, q, sources, lam_ref, gain):
    tq = q.shape[0]
    sub = lax.broadcasted_iota(jnp.int32, (1, GROUP), 1) // DIFF_DIM
    zero = jnp.zeros_like(q)
    q2 = jnp.concatenate([jnp.where(sub == 2 * h, q, zero), jnp.where(sub == 2 * h + 1, q, zero)], axis=0)
    chunks = []
    for kr, vr in sources:
        n_keys = kr.shape[0]
        step = min(ck, n_keys)
        chunks += [(kr, vr, c0, step) for c0 in range(0, n_keys, step)]

    def scores(i):
        kr, _, c0, step = chunks[i]
        return _dot_nt(kr[c0:c0 + step, :], q2)

    pending = [scores(i) for i in range(min(DIFF_LOOKAHEAD, len(chunks)))]
    m = acc = None
    for i, (_, vr, c0, step) in enumerate(chunks):
        s = pending.pop(0)
        if i + DIFF_LOOKAHEAD < len(chunks):
            pending.append(scores(i + DIFF_LOOKAHEAD))
        vt = vr[:, c0:c0 + step]
        mc = s.max(axis=0, keepdims=True)
        if m is None:
            m = mc
            acc = _dot(vt, jnp.exp2(s - m).astype(BF16))
        else:
            m_new = jnp.maximum(m, mc)
            acc = jnp.exp2(m - m_new) * acc + _dot(vt, jnp.exp2(s - m_new).astype(BF16))
            m = m_new
    o2 = acc[:HEAD_DIM] / acc[HEAD_DIM:HEAD_DIM + 1]
    od = o2[:, :tq] - lam_ref[0, 0] * o2[:, tq:]
    y = od * lax.rsqrt(jnp.mean(od * od, axis=0, keepdims=True) + EPS) * gain * lam_ref[0, 1]
    return y.astype(BF16)


def _diff_attention(dq, dk, dvt, ctx, seq, lam2, gain_col, heads, tq=DIFF_TQ, ck=DIFF_CK):
    rows = dq.shape[0]
    b = rows // seq
    tq = min(tq, seq)
    nq = seq // tq
    in_specs = [pl.BlockSpec((tq, GROUP), lambda i, n, h: (i * nq + n, 0)),
                pl.BlockSpec((seq, GROUP), lambda i, n, h: (i, 0)),
                pl.BlockSpec((heads, V_ROWS, seq), lambda i, n, h: (h, 0, i))]
    args = [dq, dk, dvt]
    if ctx is not None:
        p = ctx[0].shape[1]
        in_specs += [pl.BlockSpec((None, p, GROUP), lambda i, n, h: (i, 0, 0)),
                     pl.BlockSpec((None, heads, V_ROWS, p), lambda i, n, h: (i, h, 0, 0))]
        args += list(ctx)
    in_specs += [pl.BlockSpec(memory_space=pltpu.SMEM),
                 pl.BlockSpec((heads * HEAD_DIM, 1), lambda i, n, h: (h, 0))]
    args += [lam2, gain_col]
    return pl.pallas_call(
        functools.partial(_diff_body, ck, ctx is not None, heads),
        grid=(b, nq, N_HEADS // heads),
        in_specs=in_specs,
        out_specs=pl.BlockSpec((heads * HEAD_DIM, tq), lambda i, n, h: (h, i * nq + n)),
        out_shape=_sds((GROUP, rows), BF16),
        compiler_params=_cparams("parallel", "parallel", "parallel"),
        name="diff_ctx" if ctx is not None else "diff",
    )(*args)


def _ret_operands(rope, rq, rk, cos, sin):
    q = rq
    k = rk * HEAD_DIM ** -0.5
    if rope:
        q = _rope(q, cos, sin, HEAD_DIM // 4)
        k = _rope(k, cos, sin, HEAD_DIM // 4)
    return q, k


def _hgrn_gate(z, lb):
    f = lb + (1.0 - lb) * jax.nn.sigmoid(z)
    return jnp.log(jnp.maximum(f, 1e-30)), (1.0 - lb) * jax.nn.sigmoid(-z)


def _row_index(rows):
    return lax.broadcasted_iota(jnp.int32, (rows, 1), 0)


def _segment_state_step(st_ref, k, v, cum, total, bd):
    ku = k * jnp.exp(total - cum)
    st_ref[...] = st_ref[...] * jnp.exp(total) + _dot_tn(v.astype(BF16), ku.astype(BF16)) * bd


def _states_body(rkf_ref, rvf_ref, rkb_ref, rvb_ref, cf_ref, sf_ref, cb_ref, sb_ref, lg_ref,
                 zf_ref, hvf_ref, zb_ref, hvb_ref, lb_ref, tril_ref, triu_ref,
                 r0f_ref, r0b_ref, h0f_ref, h0b_ref, bd_ref,
                 ref_ref, reb_ref, hef_ref, heb_ref, rstf, rstb, hstf, hstb):
    s = pl.program_id(1)

    @pl.when(s == 0)
    def _():
        rstf[...] = r0f_ref[...]
        rstb[...] = r0b_ref[...]
        hstf[...] = h0f_ref[...]
        hstb[...] = h0b_ref[...]

    ref_ref[...] = rstf[...].astype(BF16)
    reb_ref[...] = rstb[...].astype(BF16)
    hef_ref[...] = hstf[...].astype(BF16)
    heb_ref[...] = hstb[...].astype(BF16)
    bd = bd_ref[...]
    rows = rvf_ref.shape[0]
    lff, kf = _hgrn_gate(zf_ref[...], lb_ref[0:1, :])
    lfb, kb = _hgrn_gate(zb_ref[...], lb_ref[1:2, :])
    cumf = _sum_rows(tril_ref[...], lff)
    cumb = _sum_rows(triu_ref[...], lfb)
    _segment_state_step(hstf, kf, hvf_ref[...], cumf, cumf[rows - 1:rows, :], bd)
    _segment_state_step(hstb, kb, hvb_ref[...], cumb, cumb[0:1, :], bd)
    _, kf = _ret_operands(True, rkf_ref[...], rkf_ref[...], cf_ref[...], sf_ref[...])
    _, kb = _ret_operands(True, rkb_ref[...], rkb_ref[...], cb_ref[...], sb_ref[...])
    i = _row_index(rows).astype(F32)
    lgf, lgb = lg_ref[0:1, :], lg_ref[1:2, :]
    _segment_state_step(rstf, kf, rvf_ref[...], (i + 1.0) * lgf, rows * lgf, bd)
    _segment_state_step(rstb, kb, rvb_ref[...], (rows - i) * lgb, rows * lgb, bd)


def _segment_states(proj, n_b, seq, ret_s0, hgrn_s0, lg_rows, lb_rows, consts, tables):
    ns = seq // SEG
    blk = (SEG, GROUP)

    def fwd(cb):
        return pl.BlockSpec(blk, lambda i, s, cb=cb: (i * ns + s, cb))

    def bwd(cb):
        return pl.BlockSpec(blk, lambda i, s, cb=cb: (i * ns + ns - 1 - s, cb))

    tf = pl.BlockSpec((SEG, LANES), lambda i, s: (s, 0))
    tb = pl.BlockSpec((SEG, LANES), lambda i, s: (ns - 1 - s, 0))
    st_spec = pl.BlockSpec((None, GROUP, GROUP), lambda i, s: (i, 0, 0))
    ef_spec = pl.BlockSpec((None, None, GROUP, GROUP), lambda i, s: (i, s, 0, 0))
    eb_spec = pl.BlockSpec((None, None, GROUP, GROUP), lambda i, s: (i, ns - 1 - s, 0, 0))
    e_shape = _sds((n_b, ns, GROUP, GROUP), BF16)
    return pl.pallas_call(
        _states_body,
        grid=(n_b, ns),
        in_specs=[fwd(CB_RK), fwd(CB_RV), bwd(CB_RK), bwd(CB_RV), tf, tf, tb, tb, _const_spec((2, GROUP)),
                  fwd(CB_HZF), fwd(CB_HI), bwd(CB_HZB), bwd(CB_HI), _const_spec((2, GROUP)),
                  _const_spec((SEG, SEG)), _const_spec((SEG, SEG)),
                  st_spec, st_spec, st_spec, st_spec, _const_spec((GROUP, GROUP))],
        out_specs=[ef_spec, eb_spec, ef_spec, eb_spec],
        out_shape=[e_shape] * 4,
        scratch_shapes=[pltpu.VMEM((GROUP, GROUP), F32)] * 4,
        compiler_params=_cparams("parallel", "arbitrary"),
        name="seg_states",
    )(proj, proj, proj, proj, tables[0], tables[1], tables[0], tables[1], lg_rows,
      proj, proj, proj, proj, lb_rows, consts["tril_seg"], consts["triu_seg"],
      *ret_s0, *hgrn_s0, consts["bd"])


def _ret_out_body(rope, has_state, emit_state, *refs):
    rq_ref, rk_ref, rv_ref, rg_ref = refs[:4]
    refs = refs[4:]
    cos = sin = None
    if rope:
        cos, sin = refs[0][...], refs[1][...]
        refs = refs[2:]
    lgs_ref, lg_ref, gain_ref, g64_ref = refs[:4]
    refs = refs[4:]
    if has_state:
        ef_ref, eb_ref = refs[:2]
        refs = refs[2:]
    o_ref = refs[0]
    mask_ref = refs[-1]
    rows = rq_ref.shape[0]

    @pl.when((pl.program_id(0) == 0) & (pl.program_id(1) == 0))
    def _():
        d = (lax.broadcasted_iota(jnp.int32, (rows, rows), 0)
             - lax.broadcasted_iota(jnp.int32, (rows, rows), 1)).astype(F32)
        for h in range(N_HEADS):
            mask_ref[h * rows:(h + 1) * rows, :] = (
                jnp.where(d >= 0, jnp.exp(jnp.maximum(d, 0.0) * lgs_ref[0, h]), 0.0)
                + jnp.where(d <= 0, jnp.exp(jnp.maximum(-d, 0.0) * lgs_ref[1, h]), 0.0))

    q, k = _ret_operands(rope, rq_ref[...], rk_ref[...], cos, sin)
    v = rv_ref[...]
    kb16, vb16 = k.astype(BF16), v.astype(BF16)
    s4 = _dot_nt(_stack_heads(q.astype(BF16)), kb16)
    o = _unstack_heads(_dot((s4 * mask_ref[...]).astype(BF16), vb16), rows)
    i = _row_index(rows).astype(F32)
    lgf, lgb = lg_ref[0:1, :], lg_ref[1:2, :]
    if has_state:
        o = o + _dot_nt((q * jnp.exp((i + 1.0) * lgf)).astype(BF16), ef_ref[...])
        o = o + _dot_nt((q * jnp.exp((rows - i) * lgb)).astype(BF16), eb_ref[...])
    y = _group_rms(o, g64_ref[...], HEAD_DIM, gain_ref[...]) * _silu(rg_ref[...])
    o_ref[...] = y.astype(BF16)
    if emit_state:
        sf_ref, sb_ref = refs[1:3]
        sf_ref[...] = _head_blocks(_dot_tn((k * jnp.exp((rows - 1.0 - i) * lgf)).astype(BF16), vb16))
        sb_ref[...] = _head_blocks(_dot_tn((k * jnp.exp(i * lgb)).astype(BF16), vb16))


def _ret_out(proj, n_b, seq, lg_smem, lg_rows, gain, consts, tables, states, emit_state):
    ns = seq // SEG
    rope = tables is not None
    blk = (SEG, GROUP)

    def col(cb):
        return pl.BlockSpec(blk, lambda i, s, cb=cb: (i * ns + s, cb))

    in_specs = [col(CB_RQ), col(CB_RK), col(CB_RV), col(CB_RG)]
    args = [proj] * 4
    if rope:
        in_specs += [pl.BlockSpec((SEG, LANES), lambda i, s: (s, 0))] * 2
        args += list(tables)
    in_specs += [pl.BlockSpec(memory_space=pltpu.SMEM), _const_spec((2, GROUP)), _const_spec((1, GROUP)),
                 _const_spec((GROUP, GROUP))]
    args += [lg_smem, lg_rows, gain, consts["g64"]]
    if states is not None:
        in_specs += [pl.BlockSpec((None, None, GROUP, GROUP), lambda i, s: (i, s, 0, 0))] * 2
        args += list(states)
    out_specs = [pl.BlockSpec(blk, lambda i, s: (i * ns + s, 0))]
    out_shape = [_sds((n_b * seq, GROUP), BF16)]
    if emit_state:
        assert ns == 1 and states is None
        out_specs += [pl.BlockSpec((None, HEAD_DIM, GROUP), lambda i, s: (i, 0, 0))] * 2
        out_shape += [_sds((n_b, HEAD_DIM, GROUP), F32)] * 2
    return pl.pallas_call(
        functools.partial(_ret_out_body, rope, states is not None, emit_state),
        grid=(n_b, ns),
        in_specs=in_specs, out_specs=out_specs, out_shape=out_shape,
        scratch_shapes=[pltpu.VMEM((N_HEADS * SEG, SEG), F32)],
        compiler_params=_cparams("arbitrary", "arbitrary"),
        name="ret_out_ctx" if states is not None else "ret_out",
    )(*args)


PAIR_LEVELS_MATMUL = (2, 4, 8)
PAIR_LEVELS_CUMSUM = (16, 32, 64, 128, 256)
PAIR_FOLD = 64


def _pair_level_matrix(n, forward):
    mats = []
    j = np.arange(n)[None, :]
    t = np.arange(n)[:, None]
    for g in PAIR_LEVELS_MATMUL:
        half = g // 2
        pos = t % g
        if forward:
            mid = t - pos + half - 1
            m = np.where(pos >= half, (j > mid) & (j <= t), (j > t) & (j <= mid))
        else:
            mid = t - pos + half
            m = np.where(pos < half, (j >= t) & (j < mid), (j >= mid) & (j < t))
        mats.append(m)
    return jnp.asarray(np.concatenate(mats, axis=0), BF16)


def _pair_decays(lf, cs, small_mat, forward):
    rows = lf.shape[0]
    hi = lf.astype(BF16)
    lo = (lf - hi.astype(F32)).astype(BF16)
    d = _dot(small_mat, hi) + _dot(small_mat, lo)
    out = [jnp.exp(d[i * rows:(i + 1) * rows]) for i in range(len(PAIR_LEVELS_MATMUL))]
    for g in PAIR_LEVELS_CUMSUM:
        half = g // 2
        pieces = []
        for grp in range(rows // g):
            mid = grp * g + (half - 1 if forward else half)
            pieces.append(jnp.broadcast_to(cs[mid:mid + 1, :], (g, cs.shape[1])))
        diff = cs - (pieces[0] if len(pieces) == 1 else jnp.concatenate(pieces, axis=0))
        out.append(jnp.exp(jnp.minimum(diff, -diff)))
    return out


def _hgrn_pair_weights(q, kf, kb, decays_f, decays_b):
    rows = q.shape[0]
    nfold = rows // PAIR_FOLD
    levels = PAIR_LEVELS_MATMUL + PAIR_LEVELS_CUMSUM
    pos = _row_index(rows)
    t_loc = lax.broadcasted_iota(jnp.int32, (PAIR_FOLD, GROUP), 0)
    s_loc = lax.broadcasted_iota(jnp.int32, (PAIR_FOLD, GROUP), 1) % PAIR_FOLD
    r_big = lax.broadcasted_iota(jnp.int32, (N_HEADS * rows, rows), 0) % rows
    c_big = lax.broadcasted_iota(jnp.int32, (N_HEADS * rows, rows), 1)
    folded = [None] * nfold
    stacked = None
    for g, ef, eb in reversed(list(zip(levels, decays_f, decays_b))):
        right = (pos % g) >= (g // 2)
        qs = (q * jnp.where(right, ef, eb)).astype(BF16)
        ks = (jnp.where(right, kb, kf) * jnp.where(right, eb, ef)).astype(BF16)
        if g > PAIR_FOLD:
            s = _dot_nt(_stack_heads(qs), ks)
            stacked = s if stacked is None else jnp.where((r_big // g) == (c_big // g), s, stacked)
        else:
            same = None if g == PAIR_FOLD else (t_loc // g) == (s_loc // g)
            for j in range(nfold):
                r0, r1 = j * PAIR_FOLD, (j + 1) * PAIR_FOLD
                s = _dot_nt(qs[r0:r1], _stack_heads(ks[r0:r1]))
                folded[j] = s if same is None else jnp.where(same, s, folded[j])
    stacked = jnp.where((r_big // PAIR_FOLD) == (c_big // PAIR_FOLD), 0.0, stacked)
    folded = [jnp.where(t_loc == s_loc, 0.0, f) for f in folded]
    return folded, stacked


def _hgrn_out_body(has_state, emit_state, *refs):
    (q_ref, zf_ref, zb_ref, v_ref, g_ref, lb_ref, gain_ref, g64_ref,
     tril_ref, triu_ref, smf_ref, smb_ref) = refs[:12]
    refs = refs[12:]
    g64 = g64_ref[...]
    if has_state:
        ef, eb = refs[0][...], refs[1][...]
        refs = refs[2:]
    o_ref = refs[0]
    q, v = q_ref[...], v_ref[...]
    rows = q.shape[0]
    lff, kf = _hgrn_gate(zf_ref[...], lb_ref[0:1, :])
    lfb, kb = _hgrn_gate(zb_ref[...], lb_ref[1:2, :])
    csf = _sum_rows(tril_ref[...], lff)
    csb = _sum_rows(triu_ref[...], lfb)
    folded, stacked = _hgrn_pair_weights(q, kf, kb, _pair_decays(lff, csf, smf_ref[...], True),
                                         _pair_decays(lfb, csb, smb_ref[...], False))
    vb = v.astype(BF16)
    o = _dot((q * (kf + kb)).astype(BF16), g64) * v
    o = o + _unstack_heads(_dot(stacked.astype(BF16), vb), rows)
    tiles = []
    for j in range(rows // PAIR_FOLD):
        r0, r1 = j * PAIR_FOLD, (j + 1) * PAIR_FOLD
        tiles.append(_dot(folded[j].astype(BF16), _stack_heads(vb[r0:r1])))
    o = o + jnp.concatenate(tiles, axis=0)
    if has_state:
        o = o + _dot_nt((q * jnp.exp(csf)).astype(BF16), ef)
        o = o + _dot_nt((q * jnp.exp(csb)).astype(BF16), eb)
    y = _group_rms(o, g64, HEAD_DIM, gain_ref[...]) * _silu(g_ref[...])
    o_ref[...] = y.astype(BF16)
    if emit_state:
        totf, totb = csf[rows - 1:rows, :], csb[0:1, :]
        refs[1][...] = _head_blocks(_dot_tn((kf * jnp.exp(totf - csf)).astype(BF16), vb))
        refs[2][...] = _head_blocks(_dot_tn((kb * jnp.exp(totb - csb)).astype(BF16), vb))


def _hgrn_out(proj, n_b, seq, lb_rows, gain, consts, states, emit_state):
    ns = seq // SEG
    blk = (SEG, GROUP)

    def col(cb):
        return pl.BlockSpec(blk, lambda i, s, cb=cb: (i * ns + s, cb))

    n_small = len(PAIR_LEVELS_MATMUL) * SEG
    in_specs = [col(CB_HQ), col(CB_HZF), col(CB_HZB), col(CB_HI), col(CB_HG),
                _const_spec((2, GROUP)), _const_spec((1, GROUP)), _const_spec((GROUP, GROUP)),
                _const_spec((SEG, SEG)), _const_spec((SEG, SEG)),
                _const_spec((n_small, SEG)), _const_spec((n_small, SEG))]
    args = [proj] * 5 + [lb_rows, gain, consts["g64"],
                         consts["tril_seg"], consts["triu_seg"], consts["pair_f"], consts["pair_b"]]
    if states is not None:
        in_specs += [pl.BlockSpec((None, None, GROUP, GROUP), lambda i, s: (i, s, 0, 0))] * 2
        args += list(states)
    out_specs = [pl.BlockSpec(blk, lambda i, s: (i * ns + s, 0))]
    out_shape = [_sds((n_b * seq, GROUP), BF16)]
    if emit_state:
        assert ns == 1 and states is None
        out_specs += [pl.BlockSpec((None, HEAD_DIM, GROUP), lambda i, s: (i, 0, 0))] * 2
        out_shape += [_sds((n_b, HEAD_DIM, GROUP), F32)] * 2
    return pl.pallas_call(
        functools.partial(_hgrn_out_body, states is not None, emit_state),
        grid=(n_b, ns),
        in_specs=in_specs, out_specs=out_specs, out_shape=out_shape,
        compiler_params=_cparams("parallel", "parallel"),
        name="hgrn_out_ctx" if states is not None else "hgrn_out",
    )(*args)


def _ffn_body(x_ref, oa_ref, ob_ref, oc_ref, od_ref, wo_ref, g1_ref, n2_ref, sc_ref, sh_ref, g2_ref,
              wg_ref, wu_ref, wd_ref, y_ref, x1_ref, h_ref, acc_ref):
    j = pl.program_id(1)

    @pl.when(j == 0)
    def _():
        mix = _dot(oa_ref[...], wo_ref[0:GROUP, :])
        mix += _dot_tn(ob_ref[...], wo_ref[GROUP:2 * GROUP, :])
        mix += _dot_tn(oc_ref[...], wo_ref[2 * GROUP:3 * GROUP, :])
        mix += _dot(od_ref[...], wo_ref[3 * GROUP:4 * GROUP, :])
        x1 = x_ref[...] + g1_ref[...] * mix
        x1_ref[...] = x1
        h = x1 * lax.rsqrt(jnp.mean(x1 * x1, axis=-1, keepdims=True) + EPS) * n2_ref[...]
        h_ref[...] = (h * (1.0 + sc_ref[...]) + sh_ref[...]).astype(BF16)
        acc_ref[...] = jnp.zeros_like(acc_ref)

    h = h_ref[...]
    a = _silu(_dot(h, wg_ref[...])) * _dot(h, wu_ref[...])
    acc_ref[...] += _dot(a.astype(BF16), wd_ref[...])

    @pl.when(j == pl.num_programs(1) - 1)
    def _():
        y_ref[...] = x1_ref[...] + g2_ref[...] * acc_ref[...]


def _outproj_ffn(x, mixes, w_out, g1, n2, sc2, sh2, g2, w_in, w_dn, rows_per_mod, mod_row0, tm=FFN_TM, th=FFN_TH):
    t, d = x.shape
    hid = w_dn.shape[0]
    nh = hid // th
    mod_map = _mod_row_map(tm, rows_per_mod, mod_row0)
    mod2 = lambda i, j: mod_map(i)
    row = lambda i, j: (i, 0)
    mix_spec = pl.BlockSpec((tm, GROUP), row)
    mix_t_spec = pl.BlockSpec((GROUP, tm), lambda i, j: (0, i))
    mod_spec = pl.BlockSpec((None, 1, d), mod2)
    return pl.pallas_call(
        _ffn_body,
        grid=(t // tm, nh),
        in_specs=[pl.BlockSpec((tm, d), row), mix_spec, mix_t_spec, mix_t_spec, mix_spec]
                 + [_const_spec((d, d)), mod_spec, _const_spec((1, d)), mod_spec, mod_spec, mod_spec,
                    pl.BlockSpec((d, th), lambda i, j: (0, j)),
                    pl.BlockSpec((d, th), lambda i, j: (0, nh + j)),
                    pl.BlockSpec((th, d), lambda i, j: (j, 0))],
        out_specs=pl.BlockSpec((tm, d), row),
        out_shape=_sds((t, d), F32),
        scratch_shapes=[pltpu.VMEM((tm, d), F32), pltpu.VMEM((tm, d), BF16), pltpu.VMEM((tm, d), F32)],
        compiler_params=_cparams("parallel", "arbitrary"),
        name="outproj_ffn",
    )(x, *mixes, w_out, g1, n2, sc2, sh2, g2, w_in, w_in, w_dn)


def _lane_rows(per_head):
    return jnp.repeat(per_head.astype(F32), HEAD_DIM, axis=1)


def _states_to_lanes(s):
    eye = jnp.eye(N_HEADS, dtype=F32)
    st = jnp.swapaxes(s.astype(F32), -1, -2)[..., :, :, None, :] * eye[:, None, :, None]
    return st.reshape(s.shape[:-3] + (GROUP, GROUP))


def _with_ones_rows(vt):
    ones = jnp.ones(vt.shape[:-2] + (V_ROWS - HEAD_DIM, vt.shape[-1]), vt.dtype)
    return jnp.concatenate([vt, ones], axis=-2)


def _lanes_to_states(sf, sb):
    s = jnp.stack([sf, sb], axis=1)
    return s.reshape(s.shape[:3] + (N_HEADS, HEAD_DIM)).transpose(0, 1, 3, 2, 4)


def kernel(x_prompt, x_sample, state_ret, cache_win_k, cache_win_v, cache_diff_k, cache_diff_v, state_hgrn, c, c_ctx, norm1_g, norm2_g, w_ada, b_ada, w_in, ret_decay, ret_norm_g, win_q_norm, win_k_norm, win_sink, diff_q_norm, diff_k_norm, diff_lambda, diff_norm_g, hgrn_lb_logits, hgrn_norm_g, w_out, w_ffn_in, w_ffn_out):
    n_p, seq_p, d = x_prompt.shape
    n_s, seq_s, _ = x_sample.shape
    depth = w_in.shape[0]
    t_s, t_p = n_s * seq_s, n_p * seq_p
    assert seq_p == SEG and seq_s % SEG == 0 and d == N_HEADS * GROUP

    tril_seg, triu_seg = _cumsum_matrices(SEG)
    consts = dict(g64=_group_matrix(GROUP, HEAD_DIM), g32=_group_matrix(GROUP, DIFF_DIM),
                  bd=_group_matrix(GROUP, HEAD_DIM).astype(F32), dupk=_kv_dup_matrix(),
                  tril_seg=tril_seg, triu_seg=triu_seg,
                  pair_f=_pair_level_matrix(SEG, True), pair_b=_pair_level_matrix(SEG, False))
    tab64 = _rope_tables(seq_s, HEAD_DIM)
    tab32 = _rope_tables(seq_s, DIFF_DIM)

    n_rows = -(-(n_s + 1) // 8) * 8
    cond = jnp.zeros((n_rows, d), F32).at[:n_s].set(c).at[n_s].set(c_ctx)
    mod = _adaln(cond, w_ada, b_ada).reshape(depth, n_rows, 6, 1, d)

    lb_p = jax.nn.softmax(hgrn_lb_logits.astype(F32), axis=0)
    lb_all = jnp.cumsum(lb_p, axis=0) - lb_p
    log_gamma = -jnp.exp(ret_decay.astype(F32))

    ctx_states = _states_to_lanes(jnp.stack([state_ret, state_hgrn], axis=2))

    xs = x_sample.reshape(t_s, d)
    xp = x_prompt.reshape(t_p, d)
    new_ret, new_wk, new_wv, new_dk, new_dv, new_hg = [], [], [], [], [], []
    for l in range(depth):
        lam_init = 0.8 - 0.6 * math.exp(-0.3 * l)
        sh1, sc1, g1, sh2, sc2, g2 = [mod[l, :, i] for i in range(6)]
        w_in_l, w_out_l = w_in[l].astype(BF16), w_out[l].astype(BF16)
        w_up_l, w_dn_l = w_ffn_in[l].astype(BF16), w_ffn_out[l].astype(BF16)
        n1, n2 = norm1_g[l][None], norm2_g[l][None]
        gains = (jnp.tile(win_q_norm[l], N_HEADS)[None], jnp.tile(win_k_norm[l], KV_HEADS)[None],
                 jnp.tile(diff_q_norm[l], 2 * N_HEADS)[None], jnp.tile(diff_k_norm[l], 2 * N_HEADS)[None])
        sink = win_sink[l].astype(F32)
        lq1, lk1, lq2, lk2 = diff_lambda[l].astype(F32)
        lam = jnp.exp(jnp.sum(lq1 * lk1)) - jnp.exp(jnp.sum(lq2 * lk2)) + lam_init
        lam2 = jnp.stack([lam, jnp.asarray(1.0 - lam_init, F32)]).reshape(1, 2)
        dgain = diff_norm_g[l].astype(F32)[:, None]
        lg = log_gamma[l]
        lg_rows = _lane_rows(lg)
        rgain = ret_norm_g[l][None]
        lb_rows = lb_all[l]
        hgain = hgrn_norm_g[l][None]

        proj = _inproj(xs, n1, sc1, sh1, w_in_l, seq_s, 0)
        wq, wk, wvt, dq, dk, dvt = _prep(proj, gains, consts, tab64 + tab32, False)
        ctx_wk = jnp.repeat(cache_win_k[:, l].transpose(0, 2, 1, 3), N_HEADS // KV_HEADS, axis=2)
        ctx_wk = ctx_wk.reshape(n_s, -1, GROUP).astype(BF16)
        ctx_wvt = _with_ones_rows(cache_win_v[:, l].swapaxes(-1, -2).astype(BF16))
        o_win = _win_sample(wq, wk, wvt, ctx_wk, ctx_wvt, jnp.repeat(sink * LOG2_E, WIN_BLOCK)[None])
        ctx_dk = cache_diff_k[:, l].transpose(0, 3, 1, 2, 4).reshape(n_s, -1, GROUP).astype(BF16)
        ctx_dvt = _with_ones_rows(cache_diff_v[:, l].swapaxes(-1, -2).astype(BF16))
        o_diff = _diff_attention(dq, dk, dvt, (ctx_dk, ctx_dvt), seq_s, lam2, dgain, DIFF_HEADS_LATENT)
        entry = _segment_states(proj, n_s, seq_s, (ctx_states[:, l, 0, 0], ctx_states[:, l, 0, 1]),
                                (ctx_states[:, l, 1, 0], ctx_states[:, l, 1, 1]), lg_rows, lb_rows, consts, tab64)
        (o_ret,) = _ret_out(proj, n_s, seq_s, lg, lg_rows, rgain, consts, tab64, entry[:2], False)
        (o_h,) = _hgrn_out(proj, n_s, seq_s, lb_rows, hgain, consts, entry[2:], False)
        xs = _outproj_ffn(xs, (o_ret, o_win, o_diff, o_h), w_out_l, g1, n2, sc2, sh2, g2, w_up_l, w_dn_l, seq_s, 0)

        proj = _inproj(xp, n1, sc1, sh1, w_in_l, t_p, n_s)
        wq, wk, wvt, dq, dk, dvt, wk_c, wv_c, dk_c, dv_c = _prep(proj, gains, consts, None, True, tm=seq_p)
        o_win = _win_prompt(wq, wk, wvt, seq_p, jnp.repeat(sink * LOG2_E, seq_p)[None])
        o_diff = _diff_attention(dq, dk, dvt, None, seq_p, lam2, dgain, N_HEADS)
        o_ret, rsf, rsb = _ret_out(proj, n_p, seq_p, lg, lg_rows, rgain, consts, None, None, True)
        o_h, hsf, hsb = _hgrn_out(proj, n_p, seq_p, lb_rows, hgain, consts, None, True)
        xp = _outproj_ffn(xp, (o_ret, o_win, o_diff, o_h), w_out_l, g1, n2, sc2, sh2, g2, w_up_l, w_dn_l, t_p, n_s)

        new_ret.append(_lanes_to_states(rsf, rsb))
        new_hg.append(_lanes_to_states(hsf, hsb))
        new_wk.append(wk_c)
        new_wv.append(wv_c)
        new_dk.append(dk_c)
        new_dv.append(dv_c)

    return (xp.reshape(n_p, seq_p, d), xs.reshape(n_s, seq_s, d),
            jnp.stack(new_ret, axis=1), jnp.stack(new_wk, axis=1), jnp.stack(new_wv, axis=1),
            jnp.stack(new_dk, axis=1), jnp.stack(new_dv, axis=1), jnp.stack(new_hg, axis=1))
```
